```python
import math
import jax, jax.numpy as jnp
from jax import lax
import numpy as np

D_MODEL = 1024
BATCH = 32
SEQ = 2048
DEPTH = 4
DEC_BATCH = 8
DEC_SEQ = 8192
PAST_LEN = 128

MIX_WIDTH = D_MODEL
ATTN_WIDTH = MIX_WIDTH // 2
HYENA_WIDTH = MIX_WIDTH - ATTN_WIDTH
DA_HEADS = 4
DA_HEAD_DIM = ATTN_WIDTH // (2 * DA_HEADS)
DA_V_DIM = 2 * DA_HEAD_DIM
ROPE_DIM = DA_HEAD_DIM // 4
ROPE_THETA = 500000.0
Q_BLOCK = 128
IN_WIDTH = 3 * ATTN_WIDTH + 3 * HYENA_WIDTH
SHORT_CONV = 3
FILTER_EMB = 33
FILTER_ORDER = 64
FAST_DECAY_PCT = 0.3
SLOW_DECAY_PCT = 1.5
DECAY_TARGET = 1e-2
N_MEM = 256
X_HEADS = 4
X_HEAD_DIM = D_MODEL // X_HEADS
D_FF = -(-8 * D_MODEL // (3 * 256)) * 256
EPS = 1e-6

kernel_name = "hymba_diffattn_hyena_encoder"

F32 = jnp.float32


def rms_norm(x, g):
    xf = x.astype(F32)
    y = xf * lax.rsqrt(jnp.mean(xf * xf, axis=-1, keepdims=True) + EPS)
    return (y * g.astype(F32)).astype(x.dtype)


def rope_partial(x):
    L = x.shape[1]
    inv = ROPE_THETA ** (-jnp.arange(0, ROPE_DIM, 2, dtype=F32) / ROPE_DIM)
    ang = jnp.arange(L, dtype=F32)[:, None] * inv[None, :]
    cos = jnp.cos(ang)[:, None, None, :]
    sin = jnp.sin(ang)[:, None, None, :]
    xr = x[..., :ROPE_DIM].astype(F32)
    x1, x2 = xr[..., :ROPE_DIM // 2], xr[..., ROPE_DIM // 2:]
    rot = jnp.concatenate([x1 * cos - x2 * sin, x2 * cos + x1 * sin], axis=-1).astype(x.dtype)
    return jnp.concatenate([rot, x[..., ROPE_DIM:]], axis=-1)


def diff_attention(q, k, v, lam):
    B, L = q.shape[0], q.shape[1]
    nb = L // Q_BLOCK
    scale = DA_HEAD_DIM ** -0.5
    qb = q.reshape(B, nb, Q_BLOCK, DA_HEADS, 2, DA_HEAD_DIM).transpose(1, 0, 2, 3, 4, 5)

    def block(qblk):
        s = jnp.einsum('bqhcd,bkhcd->bchqk', qblk, k).astype(F32) * scale
        p = jax.nn.softmax(s, axis=-1)
        a = (p[:, 0] - lam * p[:, 1]).astype(v.dtype)
        return jnp.einsum('bhqk,bkhe->bqhe', a, v)

    o = lax.map(block, qb)
    return o.transpose(1, 0, 2, 3, 4).reshape(B, L, DA_HEADS, DA_V_DIM)


def short_conv(u, w, b):
    L = u.shape[1]
    pad = SHORT_CONV // 2
    up = jnp.pad(u, ((0, 0), (pad, pad), (0, 0)))
    y = w[0] * up[:, 0:L]
    for j in range(1, SHORT_CONV):
        y = y + w[j] * up[:, j:j + L]
    return y + b


def hyena_filters(L, w1, b1, fr1, w2, b2, fr2, w3):
    t = jnp.linspace(0.0, 1.0, L, dtype=F32)[:, None]
    bands = (FILTER_EMB - 1) // 2
    w = 2.0 * math.pi * jnp.arange(L, dtype=F32)[:, None] / L
    f = jnp.linspace(1e-4, bands - 1, bands, dtype=F32)[None, :]
    fw = f * w
    z = jnp.concatenate([t, jnp.cos(fw), -jnp.sin(fw)], axis=-1)
    h = jnp.sin(fr1 * (z.astype(w1.dtype) @ w1 + b1))
    h = jnp.sin(fr2 * (h @ w2 + b2))
    h = (h @ w3).astype(F32).reshape(L, 2, HYENA_WIDTH)
    max_decay = math.log(DECAY_TARGET) / FAST_DECAY_PCT
    min_decay = math.log(DECAY_TARGET) / SLOW_DECAY_PCT
    deltas = jnp.abs(jnp.linspace(min_decay, max_decay, HYENA_WIDTH, dtype=F32))
    h = h * jnp.exp(-t[:, :, None] * deltas)
    hf, hb = h[:, 0], h[:, 1]
    kern = jnp.concatenate([hf[:1] + hb[:1], hf[1:], jnp.zeros_like(hf[:1]), hb[:0:-1]], axis=0)
    return kern / jnp.sum(jnp.abs(kern), axis=0, keepdims=True)


def long_conv(u, kern):
    L = u.shape[1]
    uf = jnp.fft.rfft(u.astype(F32), n=2 * L, axis=1)
    kf = jnp.fft.rfft(kern, n=2 * L, axis=0)
    return jnp.fft.irfft(uf * kf[None], n=2 * L, axis=1)[:, :L]


def hybrid_mixer(h, l, P):
    B, L = h.shape[0], h.shape[1]
    proj = h @ P['w_in'][l]
    A = ATTN_WIDTH
    qa, ka, va, hy = proj[..., :A], proj[..., A:2 * A], proj[..., 2 * A:3 * A], proj[..., 3 * A:]
    q = rope_partial(qa.reshape(B, L, DA_HEADS, 2, DA_HEAD_DIM))
    k = rope_partial(ka.reshape(B, L, DA_HEADS, 2, DA_HEAD_DIM))
    v = va.reshape(B, L, DA_HEADS, DA_V_DIM)
    lam_init = 0.8 - 0.6 * math.exp(-0.3 * l)
    lam = (jnp.exp(jnp.sum(P['lambda_q1'][l].astype(F32) * P['lambda_k1'][l].astype(F32)))
           - jnp.exp(jnp.sum(P['lambda_q2'][l].astype(F32) * P['lambda_k2'][l].astype(F32))) + lam_init)
    o = diff_attention(q, k, v, lam)
    o = rms_norm(o, P['subln_g'][l]) * (1.0 - lam_init)
    attn_out = o.reshape(B, L, ATTN_WIDTH)
    hy = short_conv(hy, P['conv_w'][l], P['conv_b'][l])
    C = HYENA_WIDTH
    x0, x1, vh = hy[..., :C], hy[..., C:2 * C], hy[..., 2 * C:]
    kern = hyena_filters(L, P['filt_w1'][l], P['filt_b1'][l], P['filt_freq1'][l],
                         P['filt_w2'][l], P['filt_b2'][l], P['filt_freq2'][l], P['filt_w3'][l])
    u = x1 * vh
    y = (long_conv(u, kern).astype(u.dtype) + P['hyena_d'][l] * u) * x0
    return jnp.concatenate([attn_out, y], axis=-1) @ P['w_out'][l]


def memory_cross_attention(h, m, wq, wk, wv, wo):
    B, L = h.shape[0], h.shape[1]
    q = (h @ wq).reshape(B, L, X_HEADS, X_HEAD_DIM)
    k = (m @ wk).reshape(B, N_MEM, X_HEADS, X_HEAD_DIM)
    v = (m @ wv).reshape(B, N_MEM, X_HEADS, X_HEAD_DIM)
    s = jnp.einsum('bqhd,bkhd->bhqk', q, k).astype(F32) * (X_HEAD_DIM ** -0.5)
    p = jax.nn.softmax(s, axis=-1).astype(v.dtype)
    o = jnp.einsum('bhqk,bkhd->bqhd', p, v).reshape(B, L, D_MODEL)
    return o @ wo


def swiglu(h, wg, wu, wd):
    return (jax.nn.silu(h @ wg) * (h @ wu)) @ wd


def trunk(x, mem, P):
    for l in range(DEPTH):
        x = x + rms_norm(hybrid_mixer(rms_norm(x, P['ln_mix_pre'][l]), l, P), P['ln_mix_post'][l])
        m = rms_norm(mem, P['ln_mem'][l])
        x = x + rms_norm(memory_cross_attention(rms_norm(x, P['ln_x_pre'][l]), m, P['wq_x'][l], P['wk_x'][l],
                                                P['wv_x'][l], P['wo_x'][l]), P['ln_x_post'][l])
        x = x + rms_norm(swiglu(rms_norm(x, P['ln_ffn_pre'][l]), P['w_gate'][l], P['w_up'][l], P['w_down'][l]),
                         P['ln_ffn_post'][l])
    return x


def setup_inputs(seed: int = 0) -> dict:
    key = jax.random.key(seed)
    ks = iter(jax.random.split(key, 40))

    def nrm(shape, scale):
        return jax.random.normal(next(ks), shape, F32) * scale

    def gain(n):
        return 1.0 + nrm((DEPTH, n), 0.02)

    D = D_MODEL
    return {
        'x_prompt': nrm((BATCH, SEQ, D), 1.0),
        'x_sample': nrm((DEC_BATCH, DEC_SEQ, D), 1.0),
        'mem_prompt': nrm((BATCH, N_MEM, D), 1.0),
        'mem_sample': nrm((DEC_BATCH, N_MEM, D), 1.0),
        'ln_mix_pre': gain(D),
        'ln_mix_post': gain(D),
        'w_in': nrm((DEPTH, D, IN_WIDTH), D ** -0.5),
        'lambda_q1': nrm((DEPTH, DA_HEAD_DIM), 0.1),
        'lambda_k1': nrm((DEPTH, DA_HEAD_DIM), 0.1),
        'lambda_q2': nrm((DEPTH, DA_HEAD_DIM), 0.1),
        'lambda_k2': nrm((DEPTH, DA_HEAD_DIM), 0.1),
        'subln_g': gain(DA_V_DIM),
        'conv_w': nrm((DEPTH, SHORT_CONV, 3 * HYENA_WIDTH), SHORT_CONV ** -0.5),
        'conv_b': nrm((DEPTH, 3 * HYENA_WIDTH), 0.01),
        'filt_w1': nrm((DEPTH, FILTER_EMB, FILTER_ORDER), FILTER_EMB ** -0.5),
        'filt_b1': nrm((DEPTH, FILTER_ORDER), 0.01),
        'filt_freq1': gain(FILTER_ORDER),
        'filt_w2': nrm((DEPTH, FILTER_ORDER, FILTER_ORDER), FILTER_ORDER ** -0.5),
        'filt_b2': nrm((DEPTH, FILTER_ORDER), 0.01),
        'filt_freq2': gain(FILTER_ORDER),
        'filt_w3': nrm((DEPTH, FILTER_ORDER, 2 * HYENA_WIDTH), FILTER_ORDER ** -0.5),
        'hyena_d': nrm((DEPTH, HYENA_WIDTH), 1.0),
        'w_out': nrm((DEPTH, MIX_WIDTH, D), MIX_WIDTH ** -0.5),
        'ln_x_pre': gain(D),
        'ln_x_post': gain(D),
        'ln_mem': gain(D),
        'wq_x': nrm((DEPTH, D, D), D ** -0.5),
        'wk_x': nrm((DEPTH, D, D), D ** -0.5),
        'wv_x': nrm((DEPTH, D, D), D ** -0.5),
        'wo_x': nrm((DEPTH, D, D), D ** -0.5),
        'ln_ffn_pre': gain(D),
        'ln_ffn_post': gain(D),
        'w_gate': nrm((DEPTH, D, D_FF), D ** -0.5),
        'w_up': nrm((DEPTH, D, D_FF), D ** -0.5),
        'w_down': nrm((DEPTH, D_FF, D), D_FF ** -0.5),
    }


def reference(x_prompt, x_sample, mem_prompt, mem_sample, ln_mix_pre, ln_mix_post, w_in,
              lambda_q1, lambda_k1, lambda_q2, lambda_k2, subln_g, conv_w, conv_b,
              filt_w1, filt_b1, filt_freq1, filt_w2, filt_b2, filt_freq2, filt_w3, hyena_d, w_out,
              ln_x_pre, ln_x_post, ln_mem, wq_x, wk_x, wv_x, wo_x,
              ln_ffn_pre, ln_ffn_post, w_gate, w_up, w_down):
    P = dict(ln_mix_pre=ln_mix_pre, ln_mix_post=ln_mix_post, w_in=w_in,
             lambda_q1=lambda_q1, lambda_k1=lambda_k1, lambda_q2=lambda_q2, lambda_k2=lambda_k2,
             subln_g=subln_g, conv_w=conv_w, conv_b=conv_b,
             filt_w1=filt_w1, filt_b1=filt_b1, filt_freq1=filt_freq1,
             filt_w2=filt_w2, filt_b2=filt_b2, filt_freq2=filt_freq2, filt_w3=filt_w3,
             hyena_d=hyena_d, w_out=w_out,
             ln_x_pre=ln_x_pre, ln_x_post=ln_x_post, ln_mem=ln_mem,
             wq_x=wq_x, wk_x=wk_x, wv_x=wv_x, wo_x=wo_x,
             ln_ffn_pre=ln_ffn_pre, ln_ffn_post=ln_ffn_post,
             w_gate=w_gate, w_up=w_up, w_down=w_down)
    y_prompt = trunk(x_prompt, mem_prompt, P)
    y_sample = trunk(x_sample, mem_sample, P)
    return (y_prompt, y_sample)
```

```python
import functools
import math

import numpy as np
import jax
import jax.numpy as jnp
from jax import lax
from jax.experimental import pallas as pl
from jax.experimental.pallas import tpu as pltpu

F32 = jnp.float32
BF16 = jnp.bfloat16

DA_HEADS = 4
DA_HEAD_DIM = 64
DA_V_DIM = 128
ROPE_DIM = 16
ROPE_THETA = 500000.0
SHORT_CONV = 3
FILTER_EMB = 33
FILTER_ORDER = 64
FAST_DECAY_PCT = 0.3
SLOW_DECAY_PCT = 1.5
DECAY_TARGET = 1e-2
X_HEADS = 4
EPS = 1e-6
LOG2E = 1.4426950408889634

LANES = 128
BF16_SUBLANES = 16
VMEM_LIMIT_CAP = 58 * 2**20


def _round_up(a, b):
    return -(-a // b) * b


def _params(semantics, vmem_bytes):
    limit = min(int(vmem_bytes * 1.25) + (4 << 20), VMEM_LIMIT_CAP)
    return pltpu.CompilerParams(dimension_semantics=semantics, vmem_limit_bytes=limit)


def _const_spec(shape):
    nd = len(shape)
    return pl.BlockSpec(shape, lambda *_: (0,) * nd, pipeline_mode=pl.Buffered(1))


def _rms(x, g):
    ms = jnp.mean(x * x, axis=-1, keepdims=True)
    return x * lax.rsqrt(ms + EPS) * g


def _dot(a, b):
    return jnp.dot(a, b, preferred_element_type=F32)


def _dot_nt(a, b):
    return lax.dot_general(a, b, (((1,), (1,)), ((), ())), preferred_element_type=F32)


def _rope_tables(L):
    inv = ROPE_THETA ** (-np.arange(0, ROPE_DIM, 2, dtype=np.float64) / ROPE_DIM)
    ang = np.arange(L, dtype=np.float64)[:, None] * inv[None, :]
    half = ROPE_DIM // 2
    cos = np.ones((L, LANES)); sa = np.zeros((L, LANES)); sb = np.zeros((L, LANES))
    for g in range(LANES // DA_HEAD_DIM):
        o = g * DA_HEAD_DIM
        cos[:, o:o + half] = np.cos(ang)
        cos[:, o + half:o + ROPE_DIM] = np.cos(ang)
        sb[:, o:o + half] = -np.sin(ang)
        sa[:, o + half:o + ROPE_DIM] = np.sin(ang)
    return (jnp.asarray(cos, F32), jnp.asarray(sa, F32), jnp.asarray(sb, F32))


def _inproj_kernel(x_ref, g_ref, w_ref, cos_ref, sa_ref, sb_ref,
                   q_ref, k_ref, v_ref, hy_ref, *, q_scale):
    xn = _rms(x_ref[...], g_ref[...]).astype(BF16)
    cos, sa, sb = cos_ref[...], sa_ref[...], sb_ref[...]
    half = ROPE_DIM // 2

    def rope(y):
        parts = []
        for s in range(y.shape[1] // LANES):
            ys = y[:, s * LANES:(s + 1) * LANES]
            parts.append(ys * cos + pltpu.roll(ys, half, 1) * sa
                         + pltpu.roll(ys, LANES - half, 1) * sb)
        return jnp.concatenate(parts, axis=1)

    a = q_ref.shape[1]
    q_ref[...] = (rope(_dot(xn, w_ref[:, 0:a])) * q_scale).astype(BF16)
    k_ref[...] = rope(_dot(xn, w_ref[:, a:2 * a])).astype(BF16)
    v_ref[...] = _dot(xn, w_ref[:, 2 * a:3 * a]).astype(BF16)
    hy_ref[...] = _dot(xn, w_ref[:, 3 * a:]).astype(BF16)


def _inproj(x, g, w, tables, L, tm):
    T, D = x.shape
    n_out = w.shape[1]
    a = DA_HEADS * 2 * DA_HEAD_DIM
    hyw = n_out - 3 * a
    blocks_per_seq = L // tm
    tab_spec = pl.BlockSpec((tm, LANES), lambda i: (i % blocks_per_seq, 0))
    row = lambda width: pl.BlockSpec((tm, width), lambda i: (i, 0))
    vmem = 2 * tm * D * 4 + D * n_out * 2 + 6 * tm * LANES * 4 + 2 * tm * n_out * 2 + 4 * tm * 1536 * 4
    q_scale = DA_HEAD_DIM ** -0.5 * LOG2E
    return pl.pallas_call(
        functools.partial(_inproj_kernel, q_scale=q_scale),
        grid=(T // tm,),
        in_specs=[row(D), _const_spec((1, D)), _const_spec((D, n_out)), tab_spec, tab_spec, tab_spec],
        out_specs=[row(a), row(a), row(a), row(hyw)],
        out_shape=[jax.ShapeDtypeStruct((T, a), BF16)] * 3 + [jax.ShapeDtypeStruct((T, hyw), BF16)],
        compiler_params=_params(("parallel",), vmem),
        name="inproj",
    )(x, g, w, *tables)


def _attn_kernel(lam_ref, q_ref, k_ref, v_ref, g_ref, o_ref, *, tk, lam_init):
    q = q_ref[0]
    tq = q.shape[0]
    L = k_ref.shape[1]
    lane = lax.broadcasted_iota(jnp.int32, q.shape, 1)
    zero = jnp.zeros_like(q)
    qz = (jnp.where(lane < DA_HEAD_DIM, q, zero), jnp.where(lane >= DA_HEAD_DIM, q, zero))

    def chunk(j, carry):
        start = pl.multiple_of(j * tk, tk)
        kc = k_ref[0, pl.ds(start, tk), :]
        vc = v_ref[0, pl.ds(start, tk), :]
        new = []
        for c in range(2):
            m, l, acc = carry[3 * c:3 * c + 3]
            s = _dot_nt(qz[c], kc)
            mn = jnp.maximum(m, jnp.max(s, axis=-1, keepdims=True))
            alpha = jnp.exp2(m - mn)
            p = jnp.exp2(s - mn)
            l = alpha * l + jnp.sum(p, axis=-1, keepdims=True)
            acc = alpha * acc + _dot(p.astype(BF16), vc)
            new += [mn, l, acc]
        return tuple(new)

    init = (jnp.full((tq, 1), -1e30, F32), jnp.zeros((tq, 1), F32), jnp.zeros((tq, DA_V_DIM), F32)) * 2
    m0, l0, a0, m1, l1, a1 = lax.fori_loop(0, L // tk, chunk, init)

    lp = lam_ref[...]
    lam = (jnp.exp(jnp.sum(lp[0:1] * lp[1:2], axis=-1, keepdims=True))
           - jnp.exp(jnp.sum(lp[2:3] * lp[3:4], axis=-1, keepdims=True)) + lam_init)
    o = a0 / l0 - lam * (a1 / l1)
    o_ref[0] = (_rms(o, g_ref[...]) * (1.0 - lam_init)).astype(BF16)


def _diff_attention(lam_params, q, k, v, g, lam_init, tq, tk):
    B, L, A = q.shape
    qspec = pl.BlockSpec((1, tq, LANES), lambda b, h, i: (b, i, h))
    kvspec = pl.BlockSpec((1, L, LANES), lambda b, h, i: (b, 0, h))
    vmem = 4 * L * LANES * 2 + 4 * tq * LANES * 2 + 2 * tq * tk * 10 + 4 * tq * LANES * 4
    return pl.pallas_call(
        functools.partial(_attn_kernel, tk=tk, lam_init=lam_init),
        grid=(B, DA_HEADS, L // tq),
        in_specs=[_const_spec(lam_params.shape), qspec, kvspec, kvspec, _const_spec((1, DA_V_DIM))],
        out_specs=qspec,
        out_shape=jax.ShapeDtypeStruct((B, L, A), BF16),
        compiler_params=_params(("parallel", "parallel", "parallel"), vmem),
        name="diff_attn",
    )(lam_params, q, k, v, g)


class _FftPlan:
    def __init__(self, L):
        n = 2 * L
        lg = int(math.log2(n))
        assert 2 ** lg == n
        self.L, self.n = L, n
        self.N1 = 2 ** (lg // 2)
        self.N2 = n // self.N1
        self.K1 = self.N1 // 2 + 1
        self.R = _round_up(2 * self.K1, BF16_SUBLANES)
        assert self.N2 % BF16_SUBLANES == 0 and (self.N1 // 2) % BF16_SUBLANES == 0


@functools.lru_cache(maxsize=None)
def _fft_consts(L):
    p = _FftPlan(L)
    N1, N2, K1, R, n = p.N1, p.N2, p.K1, p.R, p.n
    t2 = np.arange(N2, dtype=np.float64)[:, None, None]
    k1 = np.arange(K1, dtype=np.float64)[None, :, None]
    t1 = np.arange(N1, dtype=np.float64)[None, None, :]
    ang = -2.0 * np.pi * (t1 * k1 / N1 + t2 * k1 / n)
    f1 = np.zeros((N2, R, N1))
    f1[:, 0:2 * K1:2, :] = np.cos(ang)
    f1[:, 1:2 * K1:2, :] = np.sin(ang)
    c = np.where((np.arange(K1) == 0) | (np.arange(K1) == N1 // 2), 1.0, 2.0)[None, :, None]
    th = -ang[:, :, :N1 // 2]
    g1 = np.zeros((N2, N1 // 2, R))
    g1[:, :, 0:2 * K1:2] = np.transpose(c * np.cos(th), (0, 2, 1))
    g1[:, :, 1:2 * K1:2] = np.transpose(-c * np.sin(th), (0, 2, 1))
    a2 = 2.0 * np.pi * np.outer(np.arange(N2), np.arange(N2)) / N2
    C, S = np.cos(a2), np.sin(a2)
    f2 = np.block([[C, S], [-S, C]])
    return (jnp.asarray(f1, BF16), jnp.asarray(g1, BF16), jnp.asarray(f2, BF16), jnp.asarray(f2.T, BF16))


def _dft_stage1(src_ref, f1_ref, a_ref, n_rows, plan):
    N2, R = plan.N2, plan.R

    def body(t2, _):
        rows = src_ref[pl.ds(t2, n_rows, stride=N2), :].astype(BF16)
        a_ref[pl.ds(t2, R, stride=N2), :] = _dot(f1_ref[t2], rows)
        return 0

    lax.fori_loop(0, N2, body, 0)


@functools.lru_cache(maxsize=None)
def _filter_features(L):
    t = np.linspace(0.0, 1.0, L)[:, None]
    bands = (FILTER_EMB - 1) // 2
    w = 2.0 * np.pi * np.arange(L)[:, None] / L
    f = np.linspace(1e-4, bands - 1, bands)[None, :]
    z = np.concatenate([t, np.cos(f * w), -np.sin(f * w)], axis=-1)
    z_rev = np.concatenate([z[:1], z[:0:-1]], axis=0)
    zz = np.zeros((2 * L, LANES))
    zz[:L, :FILTER_EMB] = z
    zz[L:, :FILTER_EMB] = z_rev
    return jnp.asarray(zz, BF16)


def _decay_rates(width):
    max_decay = math.log(DECAY_TARGET) / FAST_DECAY_PCT
    min_decay = math.log(DECAY_TARGET) / SLOW_DECAY_PCT
    return jnp.asarray(np.abs(np.linspace(min_decay, max_decay, width))[None, :], F32)


def _filter_kernel(z_ref, w1_ref, b1_ref, fr1_ref, w2_ref, b2_ref, fr2_ref,
                   w3f_ref, w3b_ref, dl_ref, f1_ref, f2_ref, kf_ref,
                   h_ref, kern_ref, a_ref, *, plan, ch):
    L, N1, N2, K1 = plan.L, plan.N1, plan.N2, plan.K1
    n_ch = (2 * L) // ch

    @pl.when(pl.program_id(0) == 0)
    def _():
        def mlp(i, _):
            r0 = pl.multiple_of(i * ch, ch)
            h = jnp.sin(fr1_ref[...] * (_dot(z_ref[pl.ds(r0, ch), :], w1_ref[...]) + b1_ref[...]))
            h = jnp.sin(fr2_ref[...] * (_dot(h.astype(BF16), w2_ref[...]) + b2_ref[...]))
            h_ref[pl.ds(r0, ch), :] = h.astype(BF16)
            return 0
        lax.fori_loop(0, n_ch, mlp, 0)

    tc = kern_ref.shape[1]
    local = lax.broadcasted_iota(jnp.int32, (ch, tc), 0)

    def synth(i, asum):
        r0 = pl.multiple_of(i * ch, ch)
        row = local + r0
        h = h_ref[pl.ds(r0, ch), :]
        fwd = _dot(h, w3f_ref[...])
        bwd = _dot(h, w3b_ref[...])
        lag = jnp.where(row < L, row, 2 * L - row).astype(F32)
        win = jnp.exp(lag * (-1.0 / (L - 1)) * dl_ref[...])
        val = jnp.where(row < L, fwd + jnp.where(row == 0, bwd, 0.0), bwd) * win
        val = jnp.where(row == L, 0.0, val)
        kern_ref[pl.ds(r0, ch), :] = val
        return asum + jnp.sum(jnp.abs(val), axis=0, keepdims=True)

    asum = lax.fori_loop(0, n_ch, synth, jnp.zeros((1, tc), F32))
    scale = 1.0 / (asum * float(plan.n))

    _dft_stage1(kern_ref, f1_ref, a_ref, N1, plan)

    def stage2(k1, _):
        r0 = pl.multiple_of(k1 * 2 * N2, 2 * N2)
        slab = a_ref[pl.ds(r0, 2 * N2), :].astype(BF16)
        kf_ref[pl.ds(r0, 2 * N2), :] = (_dot(f2_ref[...], slab) * scale).astype(kf_ref.dtype)
        return 0

    lax.fori_loop(0, K1, stage2, 0)
    pad = plan.R - 2 * K1
    if pad:
        kf_ref[pl.ds(2 * K1 * N2, pad * N2), :] = jnp.zeros((pad * N2, tc), kf_ref.dtype)


def _hyena_filter_spectrum(L, w1, b1, fr1, w2, b2, fr2, w3, tc):
    plan = _FftPlan(L)
    C = w3.shape[1] // 2
    zz = _filter_features(L)
    f1, _, f2, _ = _fft_consts(L)
    w1p = jnp.zeros((LANES, FILTER_ORDER), BF16).at[:FILTER_EMB].set(w1.astype(BF16))
    ch = min(512, L)
    rows_a = plan.R * plan.N2
    cblock = lambda off: pl.BlockSpec((FILTER_ORDER, tc), lambda c: (0, off + c))
    vmem = (2 * L * LANES * 2 * 2 + 2 * L * tc * 4 + rows_a * tc * 4 + 2 * rows_a * tc * 2
            + f1.size * 2 * 2 + 8 * ch * tc * 4)
    return pl.pallas_call(
        functools.partial(_filter_kernel, plan=plan, ch=ch),
        grid=(C // tc,),
        in_specs=[_const_spec(zz.shape),
                  _const_spec(w1p.shape), _const_spec((1, FILTER_ORDER)), _const_spec((1, FILTER_ORDER)),
                  _const_spec((FILTER_ORDER, FILTER_ORDER)), _const_spec((1, FILTER_ORDER)),
                  _const_spec((1, FILTER_ORDER)),
                  cblock(0), cblock(C // tc),
                  pl.BlockSpec((1, tc), lambda c: (0, c)),
                  _const_spec(f1.shape), _const_spec(f2.shape)],
        out_specs=pl.BlockSpec((rows_a, tc), lambda c: (0, c)),
        out_shape=jax.ShapeDtypeStruct((rows_a, C), BF16),
        scratch_shapes=[pltpu.VMEM((2 * L, FILTER_ORDER), BF16),
                        pltpu.VMEM((2 * L, tc), F32),
                        pltpu.VMEM((rows_a, tc), F32)],
        compiler_params=_params(("arbitrary",), vmem),
        name="hyena_filter",
    )(zz, w1p, b1, fr1, w2.astype(BF16), b2, fr2, w3.astype(BF16), w3.astype(BF16),
      _decay_rates(C), f1, f2)


def _short_conv_chunk(raw_ref, w_ref, b_ref, r0, ch, L):
    halo = BF16_SUBLANES
    c = raw_ref[0, pl.ds(r0, ch), :].astype(F32)
    p0 = pl.multiple_of(jnp.maximum(r0 - halo, 0), halo)
    n0 = pl.multiple_of(jnp.minimum(r0 + ch, L - halo), halo)
    prev = raw_ref[0, pl.ds(p0, halo), :].astype(F32)[halo - 1:halo]
    nxt = raw_ref[0, pl.ds(n0, halo), :].astype(F32)[0:1]
    prev = jnp.where(r0 == 0, 0.0, prev)
    nxt = jnp.where(r0 + ch == L, 0.0, nxt)
    rows = lax.broadcasted_iota(jnp.int32, c.shape, 0)
    up = jnp.where(rows == 0, prev, pltpu.roll(c, 1, 0))
    un = jnp.where(rows == ch - 1, nxt, pltpu.roll(c, ch - 1, 0))
    w = w_ref[...]
    return w[0:1] * up + w[1:2] * c + w[2:3] * un + b_ref[...]


def _hyena_kernel(x0_ref, x1_ref, vh_ref, w0_ref, w1_ref, wv_ref, b0_ref, b1_ref, bv_ref,
                  d_ref, kf_ref, f1_ref, g1_ref, f2_ref, f2i_ref, o_ref,
                  u_ref, a_ref, y_ref, *, plan, ch):
    L, N1, N2, K1, R = plan.L, plan.N1, plan.N2, plan.K1, plan.R
    n_ch = L // ch

    def gate_in(i, _):
        r0 = pl.multiple_of(i * ch, ch)
        x1 = _short_conv_chunk(x1_ref, w1_ref, b1_ref, r0, ch, L)
        vh = _short_conv_chunk(vh_ref, wv_ref, bv_ref, r0, ch, L)
        u_ref[pl.ds(r0, ch), :] = x1 * vh
        return 0

    lax.fori_loop(0, n_ch, gate_in, 0)

    _dft_stage1(u_ref, f1_ref, a_ref, N1 // 2, plan)

    def spectral(k1, _):
        r0 = pl.multiple_of(k1 * 2 * N2, 2 * N2)
        x = _dot(f2_ref[...], a_ref[pl.ds(r0, 2 * N2), :].astype(BF16))
        xr, xi = x[:N2], x[N2:]
        kr = kf_ref[pl.ds(r0, N2), :].astype(F32)
        ki = kf_ref[pl.ds(r0 + N2, N2), :].astype(F32)
        y = jnp.concatenate([xr * kr - xi * ki, xr * ki + xi * kr], axis=0).astype(BF16)
        a_ref[pl.ds(r0, 2 * N2), :] = _dot(f2i_ref[...], y)
        return 0

    lax.fori_loop(0, K1, spectral, 0)

    def inverse1(t2, _):
        rows = a_ref[pl.ds(t2, R, stride=N2), :].astype(BF16)
        y_ref[pl.ds(t2, N1 // 2, stride=N2), :] = _dot(g1_ref[t2], rows)
        return 0

    lax.fori_loop(0, N2, inverse1, 0)

    def gate_out(i, _):
        r0 = pl.multiple_of(i * ch, ch)
        x0 = _short_conv_chunk(x0_ref, w0_ref, b0_ref, r0, ch, L)
        u = u_ref[pl.ds(r0, ch), :]
        o_ref[0, pl.ds(r0, ch), :] = ((y_ref[pl.ds(r0, ch), :] + d_ref[...] * u) * x0).astype(o_ref.dtype)
        return 0

    lax.fori_loop(0, n_ch, gate_out, 0)


def _hyena_operator(hy, conv_w, conv_b, d, kf, tc):
    B, L, W3 = hy.shape
    C = W3 // 3
    nb = C // tc
    plan = _FftPlan(L)
    f1full, g1, f2, f2i = _fft_consts(L)
    f1 = f1full[:, :, :plan.N1 // 2]
    rows_a = plan.R * plan.N2
    ch = min(512, L)
    seq = lambda part: pl.BlockSpec((1, L, tc), lambda c, b: (b, 0, part * nb + c))
    wspec = lambda part: pl.BlockSpec((SHORT_CONV, tc), lambda c, b: (0, part * nb + c))
    bspec = lambda part: pl.BlockSpec((1, tc), lambda c, b: (0, part * nb + c))
    pad_l = lambda m: _round_up(m, LANES)
    vmem = (6 * L * tc * 2 + rows_a * tc * 2 + 2 * L * tc * 4 + rows_a * tc * 4 + 2 * L * tc * 2
            + plan.N2 * plan.R * pad_l(plan.N1 // 2) * 2 + plan.N2 * (plan.N1 // 2) * pad_l(plan.R) * 2
            + 16 * ch * tc * 4 + 16 * plan.N2 * tc * 4)
    return pl.pallas_call(
        functools.partial(_hyena_kernel, plan=plan, ch=ch),
        grid=(nb, B),
        in_specs=[seq(0), seq(1), seq(2), wspec(0), wspec(1), wspec(2), bspec(0), bspec(1), bspec(2),
                  pl.BlockSpec((1, tc), lambda c, b: (0, c)),
                  pl.BlockSpec((rows_a, tc), lambda c, b: (0, c), pipeline_mode=pl.Buffered(1)),
                  _const_spec(f1.shape), _const_spec(g1.shape), _const_spec(f2.shape), _const_spec(f2i.shape)],
        out_specs=pl.BlockSpec((1, L, tc), lambda c, b: (b, 0, c)),
        out_shape=jax.ShapeDtypeStruct((B, L, C), BF16),
        scratch_shapes=[pltpu.VMEM((L, tc), F32), pltpu.VMEM((rows_a, tc), F32), pltpu.VMEM((L, tc), F32)],
        compiler_params=_params(("parallel", "parallel"), vmem),
        name="hyena_op",
    )(hy, hy, hy, conv_w, conv_w, conv_w, conv_b, conv_b, conv_b, d, kf, f1, g1, f2, f2i)


def _outproj_kernel(a_ref, y_ref, x_ref, wa_ref, wy_ref, g_ref, o_ref):
    z = _dot(a_ref[...], wa_ref[...]) + _dot(y_ref[...], wy_ref[...])
    o_ref[...] = x_ref[...] + _rms(z, g_ref[...])


def _outproj(attn, y, x, w, g, tm):
    T, D = x.shape
    a, c = attn.shape[1], y.shape[1]
    row = lambda width: pl.BlockSpec((tm, width), lambda i: (i, 0))
    vmem = 2 * tm * (a + c) * 2 + 4 * tm * D * 4 + (a + c) * D * 2 + 3 * tm * D * 4
    return pl.pallas_call(
        _outproj_kernel,
        grid=(T // tm,),
        in_specs=[row(a), row(c), row(D), _const_spec((a, D)), _const_spec((c, D)), _const_spec((1, D))],
        out_specs=row(D),
        out_shape=jax.ShapeDtypeStruct((T, D), F32),
        compiler_params=_params(("parallel",), vmem),
        name="outproj",
    )(attn, y, x, w[:a], w[a:], g)


def _memkv_kernel(m_ref, g_ref, wk_ref, wv_ref, k_ref, v_ref):
    mn = _rms(m_ref[...], g_ref[...]).astype(BF16)
    k_ref[...] = _dot(mn, wk_ref[...]).astype(BF16)
    v_ref[...] = _dot(mn, wv_ref[...]).astype(BF16)


def _memkv(mem, g, wk, wv, tm):
    T, D = mem.shape
    row = pl.BlockSpec((tm, D), lambda i: (i, 0))
    vmem = 2 * tm * D * 4 + 2 * D * D * 2 + 4 * tm * D * 2 + 3 * tm * D * 4
    return pl.pallas_call(
        _memkv_kernel,
        grid=(T // tm,),
        in_specs=[row, _const_spec((1, D)), _const_spec((D, D)), _const_spec((D, D))],
        out_specs=[row, row],
        out_shape=[jax.ShapeDtypeStruct((T, D), BF16)] * 2,
        compiler_params=_params(("parallel",), vmem),
        name="mem_kv",
    )(mem, g, wk, wv)


def _cross_kernel(x_ref, k_ref, v_ref, gpre_ref, wq_ref, wo_ref, gpost_ref, o_ref, *, q_scale):
    x = x_ref[...]
    xn = _rms(x, gpre_ref[...]).astype(BF16)
    q = (_dot(xn, wq_ref[...]) * q_scale).astype(BF16)
    hd = q.shape[1] // X_HEADS
    outs = []
    for h in range(X_HEADS):
        sl = slice(h * hd, (h + 1) * hd)
        s = _dot_nt(q[:, sl], k_ref[0, :, sl])
        p = jnp.exp2(s - jnp.max(s, axis=-1, keepdims=True))
        l = jnp.sum(p, axis=-1, keepdims=True)
        outs.append((_dot(p.astype(BF16), v_ref[0, :, sl]) / l).astype(BF16))
    z = _dot(jnp.concatenate(outs, axis=1), wo_ref[...])
    o_ref[...] = x + _rms(z, gpost_ref[...])


def _cross_block(x, k, v, gpre, wq, wo, gpost, L, tm):
    T, D = x.shape
    n_mem = k.shape[1]
    blocks_per_seq = L // tm
    row = pl.BlockSpec((tm, D), lambda i: (i, 0))
    kv = pl.BlockSpec((1, n_mem, D), lambda i: (i // blocks_per_seq, 0, 0))
    vmem = 4 * tm * D * 4 + 4 * n_mem * D * 2 + 2 * D * D * 2 + 6 * tm * D * 4
    q_scale = (D // X_HEADS) ** -0.5 * LOG2E
    return pl.pallas_call(
        functools.partial(_cross_kernel, q_scale=q_scale),
        grid=(T // tm,),
        in_specs=[row, kv, kv, _const_spec((1, D)), _const_spec((D, D)), _const_spec((D, D)), _const_spec((1, D))],
        out_specs=row,
        out_shape=jax.ShapeDtypeStruct((T, D), F32),
        compiler_params=_params(("parallel",), vmem),
        name="cross_attn",
    )(x, k, v, gpre, wq, wo, gpost)


def _swiglu_kernel(x_ref, gpre_ref, wg_ref, wu_ref, wd_ref, gpost_ref, o_ref, *, n_split):
    x = x_ref[...]
    xn = _rms(x, gpre_ref[...]).astype(BF16)
    ff = wg_ref.shape[1]
    cw = ff // n_split
    z = None
    for c in range(n_split):
        sl = slice(c * cw, (c + 1) * cw)
        gate = _dot(xn, wg_ref[:, sl])
        up = _dot(xn, wu_ref[:, sl])
        h = (gate * (1.0 / (1.0 + jnp.exp(-gate))) * up).astype(BF16)
        part = _dot(h, wd_ref[sl, :])
        z = part if z is None else z + part
    o_ref[...] = x + _rms(z, gpost_ref[...])


def _swiglu_block(x, gpre, wg, wu, wd, gpost, tm):
    T, D = x.shape
    ff = wg.shape[1]
    n_split = 2 if (ff // 2) % LANES == 0 else 1
    row = pl.BlockSpec((tm, D), lambda i: (i, 0))
    vmem = 4 * tm * D * 4 + 3 * D * ff * 2 + 4 * tm * (ff // n_split) * 4 + 4 * tm * D * 4
    return pl.pallas_call(
        functools.partial(_swiglu_kernel, n_split=n_split),
        grid=(T // tm,),
        in_specs=[row, _const_spec((1, D)), _const_spec((D, ff)), _const_spec((D, ff)),
                  _const_spec((ff, D)), _const_spec((1, D))],
        out_specs=row,
        out_shape=jax.ShapeDtypeStruct((T, D), F32),
        compiler_params=_params(("parallel",), vmem),
        name="swiglu",
    )(x, gpre, wg, wu, wd, gpost)


def _trunk(x, mem, P):
    B, L, D = x.shape
    depth = P['w_in'].shape[0]
    n_mem = mem.shape[1]
    tm = min(512, L)
    tq, tk = min(512, L), min(2048, L)
    tc = LANES
    a = DA_HEADS * 2 * DA_HEAD_DIM
    tables = _rope_tables(L)
    xf = x.reshape(B * L, D)
    memf = mem.reshape(B * n_mem, D)
    row = lambda v: v.reshape(1, -1)
    for l in range(depth):
        lam_init = 0.8 - 0.6 * math.exp(-0.3 * l)
        q, k, v, hy = _inproj(xf, row(P['ln_mix_pre'][l]), P['w_in'][l].astype(BF16), tables, L, tm)
        lam_params = jnp.stack([P['lambda_q1'][l], P['lambda_k1'][l], P['lambda_q2'][l], P['lambda_k2'][l]])
        attn = _diff_attention(lam_params, q.reshape(B, L, a), k.reshape(B, L, a), v.reshape(B, L, a),
                               row(P['subln_g'][l]), lam_init, tq, tk)
        kf = _hyena_filter_spectrum(L, P['filt_w1'][l], row(P['filt_b1'][l]), row(P['filt_freq1'][l]),
                                    P['filt_w2'][l], row(P['filt_b2'][l]), row(P['filt_freq2'][l]),
                                    P['filt_w3'][l], tc)
        y = _hyena_operator(hy.reshape(B, L, -1), P['conv_w'][l], row(P['conv_b'][l]),
                            row(P['hyena_d'][l]), kf, tc)
        xf = _outproj(attn.reshape(B * L, a), y.reshape(B * L, -1), xf, P['w_out'][l].astype(BF16),
                      row(P['ln_mix_post'][l]), tm)
        km, vm = _memkv(memf, row(P['ln_mem'][l]), P['wk_x'][l].astype(BF16), P['wv_x'][l].astype(BF16),
                        min(512, B * n_mem))
        xf = _cross_block(xf, km.reshape(B, n_mem, D), vm.reshape(B, n_mem, D), row(P['ln_x_pre'][l]),
                          P['wq_x'][l].astype(BF16), P['wo_x'][l].astype(BF16), row(P['ln_x_post'][l]), L, tm)
        xf = _swiglu_block(xf, row(P['ln_ffn_pre'][l]), P['w_gate'][l].astype(BF16), P['w_up'][l].astype(BF16),
                           P['w_down'][l].astype(BF16), row(P['ln_ffn_post'][l]), tm)
    return xf.reshape(B, L, D)


def kernel(x_prompt, x_sample, mem_prompt, mem_sample, ln_mix_pre, ln_mix_post, w_in, lambda_q1, lambda_k1, lambda_q2, lambda_k2, subln_g, conv_w, conv_b, filt_w1, filt_b1, filt_freq1, filt_w2, filt_b2, filt_freq2, filt_w3, hyena_d, w_out, ln_x_pre, ln_x_post, ln_mem, wq_x, wk_x, wv_x, wo_x, ln_ffn_pre, ln_ffn_post, w_gate, w_up, w_down):
    P = dict(ln_mix_pre=ln_mix_pre, ln_mix_post=ln_mix_post, w_in=w_in,
             lambda_q1=lambda_q1, lambda_k1=lambda_k1, lambda_q2=lambda_q2, lambda_k2=lambda_k2,
             subln_g=subln_g, conv_w=conv_w, conv_b=conv_b,
             filt_w1=filt_w1, filt_b1=filt_b1, filt_freq1=filt_freq1,
             filt_w2=filt_w2, filt_b2=filt_b2, filt_freq2=filt_freq2, filt_w3=filt_w3,
             hyena_d=hyena_d, w_out=w_out,
             ln_x_pre=ln_x_pre, ln_x_post=ln_x_post, ln_mem=ln_mem,
             wq_x=wq_x, wk_x=wk_x, wv_x=wv_x, wo_x=wo_x,
             ln_ffn_pre=ln_ffn_pre, ln_ffn_post=ln_ffn_post,
             w_gate=w_gate, w_up=w_up, w_down=w_down)
    return (_trunk(x_prompt, mem_prompt, P), _trunk(x_sample, mem_sample, P))
```

```python
import functools
import math

import numpy as np
import jax
import jax.numpy as jnp
from jax import lax
from jax.experimental import pallas as pl
from jax.experimental.pallas import tpu as pltpu

F32 = jnp.float32
BF16 = jnp.bfloat16

DA_HEADS = 4
DA_HEAD_DIM = 64
DA_V_DIM = 128
ROPE_DIM = 16
ROPE_THETA = 500000.0
SHORT_CONV = 3
FILTER_EMB = 33
FILTER_ORDER = 64
FAST_DECAY_PCT = 0.3
SLOW_DECAY_PCT = 1.5
DECAY_TARGET = 1e-2
X_HEADS = 4
EPS = 1e-6
LOG2E = 1.4426950408889634

LANES = 128
BF16_SUBLANES = 16
VMEM_LIMIT_CAP = 58 * 2**20

STAGE1_UNROLL = 8
STAGE2_UNROLL = 4


def _round_up(a, b):
    return -(-a // b) * b


def _params(semantics, vmem_bytes):
    limit = min(int(vmem_bytes * 1.25) + (4 << 20), VMEM_LIMIT_CAP)
    return pltpu.CompilerParams(dimension_semantics=semantics, vmem_limit_bytes=limit)


def _const_spec(shape):
    nd = len(shape)
    return pl.BlockSpec(shape, lambda *_: (0,) * nd, pipeline_mode=pl.Buffered(1))


def _rms(x, g):
    ms = jnp.mean(x * x, axis=-1, keepdims=True)
    return x * lax.rsqrt(ms + EPS) * g


def _dot(a, b):
    return jnp.dot(a, b, preferred_element_type=F32)


def _dot_nt(a, b):
    return lax.dot_general(a, b, (((1,), (1,)), ((), ())), preferred_element_type=F32)


def _rope_tables(L):
    inv = ROPE_THETA ** (-np.arange(0, ROPE_DIM, 2, dtype=np.float64) / ROPE_DIM)
    ang = np.arange(L, dtype=np.float64)[:, None] * inv[None, :]
    half = ROPE_DIM // 2
    cos = np.ones((L, LANES)); sa = np.zeros((L, LANES)); sb = np.zeros((L, LANES))
    for g in range(LANES // DA_HEAD_DIM):
        o = g * DA_HEAD_DIM
        cos[:, o:o + half] = np.cos(ang)
        cos[:, o + half:o + ROPE_DIM] = np.cos(ang)
        sb[:, o:o + half] = -np.sin(ang)
        sa[:, o + half:o + ROPE_DIM] = np.sin(ang)
    return (jnp.asarray(cos, F32), jnp.asarray(sa, F32), jnp.asarray(sb, F32))


def _inproj_kernel(x_ref, g_ref, w_ref, wvt_ref, cos_ref, sa_ref, sb_ref,
                   q_ref, k_ref, vt_ref, hy_ref, *, q_scale):
    xn = _rms(x_ref[...], g_ref[...]).astype(BF16)
    cos, sa, sb = cos_ref[...], sa_ref[...], sb_ref[...]
    half = ROPE_DIM // 2

    def rope(y):
        parts = []
        for s in range(y.shape[1] // LANES):
            ys = y[:, s * LANES:(s + 1) * LANES]
            parts.append(ys * cos + pltpu.roll(ys, half, 1) * sa
                         + pltpu.roll(ys, LANES - half, 1) * sb)
        return jnp.concatenate(parts, axis=1)

    a = q_ref.shape[1]
    q_ref[...] = (rope(_dot(xn, w_ref[:, 0:a])) * q_scale).astype(BF16)
    k_ref[...] = rope(_dot(xn, w_ref[:, a:2 * a])).astype(BF16)
    vt_ref[0, 0] = _dot_nt(wvt_ref[...], xn).astype(BF16)
    hy_ref[...] = _dot(xn, w_ref[:, 3 * a:]).astype(BF16)


def _inproj(x, g, w, tables, B, L, tm, tk):
    T, D = x.shape
    n_out = w.shape[1]
    a = DA_HEADS * 2 * DA_HEAD_DIM
    hyw = n_out - 3 * a
    blocks_per_seq = L // tm
    tiles_per_chunk = tk // tm
    tab_spec = pl.BlockSpec((tm, LANES), lambda i: (i % blocks_per_seq, 0))
    row = lambda width: pl.BlockSpec((tm, width), lambda i: (i, 0))
    vt_spec = pl.BlockSpec((1, 1, a, tm), lambda i: (i // blocks_per_seq, (i % blocks_per_seq) // tiles_per_chunk,
                                                     0, i % tiles_per_chunk))
    vmem = 2 * tm * D * 4 + D * (n_out + a) * 2 + 6 * tm * LANES * 4 + 2 * tm * n_out * 2 + 4 * tm * 1536 * 4
    q_scale = DA_HEAD_DIM ** -0.5 * LOG2E
    wvt = w[:, 2 * a:3 * a].T
    return pl.pallas_call(
        functools.partial(_inproj_kernel, q_scale=q_scale),
        grid=(T // tm,),
        in_specs=[row(D), _const_spec((1, D)), _const_spec((D, n_out)), _const_spec((a, D)),
                  tab_spec, tab_spec, tab_spec],
        out_specs=[row(a), row(a), vt_spec, row(hyw)],
        out_shape=[jax.ShapeDtypeStruct((T, a), BF16)] * 2
                  + [jax.ShapeDtypeStruct((B, L // tk, a, tk), BF16), jax.ShapeDtypeStruct((T, hyw), BF16)],
        compiler_params=_params(("parallel",), vmem),
        name="inproj",
    )(x, g, w, wvt, *tables)


def _attn_kernel(lam_ref, q_ref, k_ref, vt_ref, g_ref, o_ref, s_ref, *, tk, kb, lam_init):
    q = q_ref[0]
    tq = q.shape[0]
    L = k_ref.shape[1]
    lane = lax.broadcasted_iota(jnp.int32, q.shape, 1)
    zero = jnp.zeros_like(q)
    qz = (jnp.where(lane < DA_HEAD_DIM, q, zero), jnp.where(lane >= DA_HEAD_DIM, q, zero))

    nk, nb = L // tk, tk // kb
    neg = jnp.full((1, tq), -1e30, F32)

    def score_block(j, c, b, mx):
        start = pl.multiple_of(j * tk + b * kb, kb)
        s = _dot_nt(k_ref[0, pl.ds(start, kb), :], qz[c])
        s_ref[c, b * kb:(b + 1) * kb, :] = s
        return jnp.maximum(mx, jnp.max(s, axis=0, keepdims=True))

    def value_block(j, c, b, mn, lsum, pv):
        p = jnp.exp2(s_ref[c, b * kb:(b + 1) * kb, :] - mn)
        lsum = lsum + jnp.sum(p, axis=0, keepdims=True)
        pv = pv + _dot(vt_ref[0, j, :, b * kb:(b + 1) * kb], p.astype(BF16))
        return lsum, pv

    def step(score_of, value_of, mx_cur, state):
        m, l, acc = state
        mn = jnp.maximum(m, mx_cur)
        alpha = jnp.exp2(m - mn)
        lsum, pv, mx = jnp.zeros((1, tq), F32), jnp.zeros((DA_V_DIM, tq), F32), neg
        for b in range(nb):
            if score_of is not None:
                mx = score_block(*score_of, b, mx)
            lsum, pv = value_block(*value_of, b, mn, lsum, pv)
        return mx, (mn, alpha * l + lsum, alpha * acc + pv)

    mx0 = neg
    for b in range(nb):
        mx0 = score_block(0, 0, b, mx0)
    fresh = (neg, jnp.zeros((1, tq), F32), jnp.zeros((DA_V_DIM, tq), F32))

    def chunk(j, carry):
        mx0, st0, st1 = carry
        mx1, st0 = step((j, 1), (j, 0), mx0, st0)
        mx0, st1 = step((j + 1, 0), (j, 1), mx1, st1)
        return mx0, st0, st1

    mx0, st0, st1 = lax.fori_loop(0, nk - 1, chunk, (mx0, fresh, fresh))
    mx1, st0 = step((nk - 1, 1), (nk - 1, 0), mx0, st0)
    _, st1 = step(None, (nk - 1, 1), mx1, st1)
    (_, l0, a0), (_, l1, a1) = st0, st1

    lp = lam_ref[...]
    lam = (jnp.exp(jnp.sum(lp[0:1] * lp[1:2], axis=-1, keepdims=True))
           - jnp.exp(jnp.sum(lp[2:3] * lp[3:4], axis=-1, keepdims=True)) + lam_init)
    o = a0 / l0 - lam * (a1 / l1)
    ms = jnp.mean(o * o, axis=0, keepdims=True)
    g = jnp.concatenate([g_ref[...]] * (tq // LANES), axis=1)
    o = o * lax.rsqrt(ms + EPS) * g * (1.0 - lam_init)
    o_ref[0] = o.T.astype(BF16)


def _diff_attention(lam_params, q, k, vt, g, lam_init, tq, tk):
    B, L, A = q.shape
    qspec = pl.BlockSpec((1, tq, LANES), lambda b, h, i: (b, i, h))
    kspec = pl.BlockSpec((1, L, LANES), lambda b, h, i: (b, 0, h))
    vspec = pl.BlockSpec((1, L // tk, LANES, tk), lambda b, h, i: (b, 0, h, 0))
    g_cols = jnp.broadcast_to(g.reshape(DA_V_DIM, 1), (DA_V_DIM, LANES))
    kb = min(512, tk)
    vmem = 4 * L * LANES * 2 + 4 * tq * LANES * 2 + 2 * tq * tk * 4 + 4 * tq * kb * 6 + 16 * tq * LANES * 4
    return pl.pallas_call(
        functools.partial(_attn_kernel, tk=tk, kb=kb, lam_init=lam_init),
        grid=(B, DA_HEADS, L // tq),
        in_specs=[_const_spec(lam_params.shape), qspec, kspec, vspec, _const_spec((DA_V_DIM, LANES))],
        out_specs=qspec,
        out_shape=jax.ShapeDtypeStruct((B, L, A), BF16),
        scratch_shapes=[pltpu.VMEM((2, tk, tq), F32)],
        compiler_params=_params(("parallel", "parallel", "parallel"), vmem),
        name="diff_attn",
    )(lam_params, q, k, vt, g_cols)


class _FftPlan:
    def __init__(self, L):
        n = 2 * L
        lg = int(math.log2(n))
        assert 2 ** lg == n
        self.L, self.n = L, n
        self.N1 = 2 ** (lg // 2)
        self.N2 = n // self.N1
        self.K1 = self.N1 // 2 + 1
        self.R = _round_up(2 * self.K1, BF16_SUBLANES)
        assert self.N2 % BF16_SUBLANES == 0 and (self.N1 // 2) % BF16_SUBLANES == 0


@functools.lru_cache(maxsize=None)
def _fft_consts(L):
    p = _FftPlan(L)
    N1, N2, K1, R, n = p.N1, p.N2, p.K1, p.R, p.n
    t2 = np.arange(N2, dtype=np.float64)[:, None, None]
    k1 = np.arange(K1, dtype=np.float64)[None, :, None]
    t1 = np.arange(N1, dtype=np.float64)[None, None, :]
    ang = -2.0 * np.pi * (t1 * k1 / N1 + t2 * k1 / n)
    f1 = np.zeros((N2, R, N1))
    f1[:, 0:2 * K1:2, :] = np.cos(ang)
    f1[:, 1:2 * K1:2, :] = np.sin(ang)
    c = np.where((np.arange(K1) == 0) | (np.arange(K1) == N1 // 2), 1.0, 2.0)[None, :, None]
    th = -ang[:, :, :N1 // 2]
    g1 = np.zeros((N2, N1 // 2, R))
    g1[:, :, 0:2 * K1:2] = np.transpose(c * np.cos(th), (0, 2, 1))
    g1[:, :, 1:2 * K1:2] = np.transpose(-c * np.sin(th), (0, 2, 1))
    a2 = 2.0 * np.pi * np.outer(np.arange(N2), np.arange(N2)) / N2
    C, S = np.cos(a2), np.sin(a2)
    f2 = np.block([[C, S], [-S, C]])
    return (jnp.asarray(f1, BF16), jnp.asarray(g1, BF16), jnp.asarray(f2, BF16), jnp.asarray(f2.T, BF16))


def _load_rows(ref, idx):
    parts = [ref[g, idx, :] for g in range(ref.shape[0])]
    return parts[0] if len(parts) == 1 else jnp.concatenate(parts, axis=1)


def _store_rows(ref, idx, val):
    for g in range(ref.shape[0]):
        ref[g, idx, :] = val[:, g * LANES:(g + 1) * LANES]


def _dft_stage1(src_ref, f1_ref, a_ref, n_rows, plan):
    N2, R = plan.N2, plan.R

    def body(t2, _):
        rows = _load_rows(src_ref, pl.ds(t2, n_rows, stride=N2)).astype(BF16)
        _store_rows(a_ref, pl.ds(t2, R, stride=N2), _dot(f1_ref[t2], rows))
        return 0

    lax.fori_loop(0, N2, body, 0, unroll=STAGE1_UNROLL)


@functools.lru_cache(maxsize=None)
def _filter_features(L):
    t = np.linspace(0.0, 1.0, L)[:, None]
    bands = (FILTER_EMB - 1) // 2
    w = 2.0 * np.pi * np.arange(L)[:, None] / L
    f = np.linspace(1e-4, bands - 1, bands)[None, :]
    z = np.concatenate([t, np.cos(f * w), -np.sin(f * w)], axis=-1)
    z_rev = np.concatenate([z[:1], z[:0:-1]], axis=0)
    zz = np.zeros((2 * L, LANES))
    zz[:L, :FILTER_EMB] = z
    zz[L:, :FILTER_EMB] = z_rev
    return jnp.asarray(zz, BF16)


def _decay_rates(width):
    max_decay = math.log(DECAY_TARGET) / FAST_DECAY_PCT
    min_decay = math.log(DECAY_TARGET) / SLOW_DECAY_PCT
    return jnp.asarray(np.abs(np.linspace(min_decay, max_decay, width))[None, :], F32)


def _filter_kernel(z_ref, w1_ref, b1_ref, fr1_ref, w2_ref, b2_ref, fr2_ref,
                   w3f_ref, w3b_ref, dl_ref, f1_ref, f2_ref, kf_ref,
                   h_ref, kern_ref, a_ref, *, plan, ch):
    L, N1, N2, K1 = plan.L, plan.N1, plan.N2, plan.K1
    n_ch = (2 * L) // ch

    @pl.when(pl.program_id(0) == 0)
    def _():
        def mlp(i, _):
            r0 = pl.multiple_of(i * ch, ch)
            h = jnp.sin(fr1_ref[...] * (_dot(z_ref[pl.ds(r0, ch), :], w1_ref[...]) + b1_ref[...]))
            h = jnp.sin(fr2_ref[...] * (_dot(h.astype(BF16), w2_ref[...]) + b2_ref[...]))
            h_ref[pl.ds(r0, ch), :] = h.astype(BF16)
            return 0
        lax.fori_loop(0, n_ch, mlp, 0)

    tc = kf_ref.shape[1]
    local = lax.broadcasted_iota(jnp.int32, (ch, tc), 0)

    def synth(i, asum):
        r0 = pl.multiple_of(i * ch, ch)
        row = local + r0
        h = h_ref[pl.ds(r0, ch), :]
        fwd = _dot(h, w3f_ref[...])
        bwd = _dot(h, w3b_ref[...])
        lag = jnp.where(row < L, row, 2 * L - row).astype(F32)
        win = jnp.exp(lag * (-1.0 / (L - 1)) * dl_ref[...])
        val = jnp.where(row < L, fwd + jnp.where(row == 0, bwd, 0.0), bwd) * win
        val = jnp.where(row == L, 0.0, val)
        _store_rows(kern_ref, pl.ds(r0, ch), val)
        return asum + jnp.sum(jnp.abs(val), axis=0, keepdims=True)

    asum = lax.fori_loop(0, n_ch, synth, jnp.zeros((1, tc), F32))
    scale = 1.0 / (asum * float(plan.n))

    _dft_stage1(kern_ref, f1_ref, a_ref, N1, plan)

    def stage2(k1, _):
        r0 = pl.multiple_of(k1 * 2 * N2, 2 * N2)
        slab = _load_rows(a_ref, pl.ds(r0, 2 * N2)).astype(BF16)
        kf_ref[pl.ds(r0, 2 * N2), :] = (_dot(f2_ref[...], slab) * scale).astype(kf_ref.dtype)
        return 0

    lax.fori_loop(0, K1, stage2, 0, unroll=STAGE2_UNROLL)
    pad = plan.R - 2 * K1
    if pad:
        kf_ref[pl.ds(2 * K1 * N2, pad * N2), :] = jnp.zeros((pad * N2, tc), kf_ref.dtype)


def _hyena_filter_spectrum(L, w1, b1, fr1, w2, b2, fr2, w3, tc):
    plan = _FftPlan(L)
    C = w3.shape[1] // 2
    zz = _filter_features(L)
    f1, _, f2, _ = _fft_consts(L)
    w1p = jnp.zeros((LANES, FILTER_ORDER), BF16).at[:FILTER_EMB].set(w1.astype(BF16))
    ch = min(512, L)
    rows_a = plan.R * plan.N2
    cblock = lambda off: pl.BlockSpec((FILTER_ORDER, tc), lambda c: (0, off + c))
    vmem = (2 * L * LANES * 2 * 2 + 2 * L * tc * 4 + rows_a * tc * 4 + 2 * rows_a * tc * 2
            + f1.size * 2 * 2 + 8 * ch * tc * 4)
    return pl.pallas_call(
        functools.partial(_filter_kernel, plan=plan, ch=ch),
        grid=(C // tc,),
        in_specs=[_const_spec(zz.shape),
                  _const_spec(w1p.shape), _const_spec((1, FILTER_ORDER)), _const_spec((1, FILTER_ORDER)),
                  _const_spec((FILTER_ORDER, FILTER_ORDER)), _const_spec((1, FILTER_ORDER)),
                  _const_spec((1, FILTER_ORDER)),
                  cblock(0), cblock(C // tc),
                  pl.BlockSpec((1, tc), lambda c: (0, c)),
                  _const_spec(f1.shape), _const_spec(f2.shape)],
        out_specs=pl.BlockSpec((rows_a, tc), lambda c: (0, c)),
        out_shape=jax.ShapeDtypeStruct((rows_a, C), BF16),
        scratch_shapes=[pltpu.VMEM((2 * L, FILTER_ORDER), BF16),
                        pltpu.VMEM((tc // LANES, 2 * L, LANES), F32),
                        pltpu.VMEM((tc // LANES, rows_a, LANES), F32)],
        compiler_params=_params(("arbitrary",), vmem),
        name="hyena_filter",
    )(zz, w1p, b1, fr1, w2.astype(BF16), b2, fr2, w3.astype(BF16), w3.astype(BF16),
      _decay_rates(C), f1, f2)


def _short_conv_chunk(raw_ref, w_ref, b_ref, r0, ch, L):
    halo = BF16_SUBLANES
    c = raw_ref[0, pl.ds(r0, ch), :].astype(F32)
    p0 = pl.multiple_of(jnp.maximum(r0 - halo, 0), halo)
    n0 = pl.multiple_of(jnp.minimum(r0 + ch, L - halo), halo)
    prev = raw_ref[0, pl.ds(p0, halo), :].astype(F32)[halo - 1:halo]
    nxt = raw_ref[0, pl.ds(n0, halo), :].astype(F32)[0:1]
    prev = jnp.where(r0 == 0, 0.0, prev)
    nxt = jnp.where(r0 + ch == L, 0.0, nxt)
    rows = lax.broadcasted_iota(jnp.int32, c.shape, 0)
    up = jnp.where(rows == 0, prev, pltpu.roll(c, 1, 0))
    un = jnp.where(rows == ch - 1, nxt, pltpu.roll(c, ch - 1, 0))
    w = w_ref[...]
    return w[0:1] * up + w[1:2] * c + w[2:3] * un + b_ref[...]


def _hyena_kernel(x0_ref, x1_ref, vh_ref, w0_ref, w1_ref, wv_ref, b0_ref, b1_ref, bv_ref,
                  d_ref, kf_ref, f1_ref, g1_ref, f2_ref, f2i_ref, o_ref,
                  u_ref, a_ref, y_ref, *, plan, ch):
    L, N1, N2, K1, R = plan.L, plan.N1, plan.N2, plan.K1, plan.R
    n_ch = L // ch

    def gate_in(i, _):
        r0 = pl.multiple_of(i * ch, ch)
        x1 = _short_conv_chunk(x1_ref, w1_ref, b1_ref, r0, ch, L)
        vh = _short_conv_chunk(vh_ref, wv_ref, bv_ref, r0, ch, L)
        _store_rows(u_ref, pl.ds(r0, ch), x1 * vh)
        return 0

    lax.fori_loop(0, n_ch, gate_in, 0)

    _dft_stage1(u_ref, f1_ref, a_ref, N1 // 2, plan)

    def spectral(k1, _):
        r0 = pl.multiple_of(k1 * 2 * N2, 2 * N2)
        x = _dot(f2_ref[...], _load_rows(a_ref, pl.ds(r0, 2 * N2)).astype(BF16))
        xr, xi = x[:N2], x[N2:]
        kr = kf_ref[pl.ds(r0, N2), :].astype(F32)
        ki = kf_ref[pl.ds(r0 + N2, N2), :].astype(F32)
        y = jnp.concatenate([xr * kr - xi * ki, xr * ki + xi * kr], axis=0).astype(BF16)
        _store_rows(a_ref, pl.ds(r0, 2 * N2), _dot(f2i_ref[...], y))
        return 0

    lax.fori_loop(0, K1, spectral, 0, unroll=STAGE2_UNROLL)

    def inverse1(t2, _):
        rows = _load_rows(a_ref, pl.ds(t2, R, stride=N2)).astype(BF16)
        _store_rows(y_ref, pl.ds(t2, N1 // 2, stride=N2), _dot(g1_ref[t2], rows))
        return 0

    lax.fori_loop(0, N2, inverse1, 0, unroll=STAGE1_UNROLL)

    def gate_out(i, _):
        r0 = pl.multiple_of(i * ch, ch)
        x0 = _short_conv_chunk(x0_ref, w0_ref, b0_ref, r0, ch, L)
        u = _load_rows(u_ref, pl.ds(r0, ch))
        y = _load_rows(y_ref, pl.ds(r0, ch))
        o_ref[0, pl.ds(r0, ch), :] = ((y + d_ref[...] * u) * x0).astype(o_ref.dtype)
        return 0

    lax.fori_loop(0, n_ch, gate_out, 0)


def _hyena_vmem_bytes(plan, tc, ch):
    L, rows_a = plan.L, plan.R * plan.N2
    pad_l = lambda m: _round_up(m, LANES)
    return (6 * L * tc * 2 + rows_a * tc * 2 + 2 * L * tc * 4 + rows_a * tc * 4 + 2 * L * tc * 2
            + plan.N2 * plan.R * pad_l(plan.N1 // 2) * 2 + plan.N2 * (plan.N1 // 2) * pad_l(plan.R) * 2
            + 16 * ch * tc * 4 + 16 * plan.N2 * tc * 4)


def _hyena_operator(hy, conv_w, conv_b, d, kf):
    B, L, W3 = hy.shape
    C = W3 // 3
    plan = _FftPlan(L)
    ch = min(512, L)
    tc = 2 * LANES if int(_hyena_vmem_bytes(plan, 2 * LANES, ch) * 1.25) + (4 << 20) <= VMEM_LIMIT_CAP else LANES
    nb = C // tc
    f1full, g1, f2, f2i = _fft_consts(L)
    f1 = f1full[:, :, :plan.N1 // 2]
    rows_a = plan.R * plan.N2
    seq = lambda part: pl.BlockSpec((1, L, tc), lambda c, b: (b, 0, part * nb + c))
    wspec = lambda part: pl.BlockSpec((SHORT_CONV, tc), lambda c, b: (0, part * nb + c))
    bspec = lambda part: pl.BlockSpec((1, tc), lambda c, b: (0, part * nb + c))
    vmem = _hyena_vmem_bytes(plan, tc, ch)
    return pl.pallas_call(
        functools.partial(_hyena_kernel, plan=plan, ch=ch),
        grid=(nb, B),
        in_specs=[seq(0), seq(1), seq(2), wspec(0), wspec(1), wspec(2), bspec(0), bspec(1), bspec(2),
                  pl.BlockSpec((1, tc), lambda c, b: (0, c)),
                  pl.BlockSpec((rows_a, tc), lambda c, b: (0, c), pipeline_mode=pl.Buffered(1)),
                  _const_spec(f1.shape), _const_spec(g1.shape), _const_spec(f2.shape), _const_spec(f2i.shape)],
        out_specs=pl.BlockSpec((1, L, tc), lambda c, b: (b, 0, c)),
        out_shape=jax.ShapeDtypeStruct((B, L, C), BF16),
        scratch_shapes=[pltpu.VMEM((tc // LANES, L, LANES), F32),
                        pltpu.VMEM((tc // LANES, rows_a, LANES), F32),
                        pltpu.VMEM((tc // LANES, L, LANES), F32)],
        compiler_params=_params(("parallel", "parallel"), vmem),
        name="hyena_op",
    )(hy, hy, hy, conv_w, conv_w, conv_w, conv_b, conv_b, conv_b, d, kf, f1, g1, f2, f2i)


def _outproj_kernel(a_ref, y_ref, x_ref, wa_ref, wy_ref, g_ref, o_ref):
    z = _dot(a_ref[...], wa_ref[...]) + _dot(y_ref[...], wy_ref[...])
    o_ref[...] = x_ref[...] + _rms(z, g_ref[...])


def _outproj(attn, y, x, w, g, tm):
    T, D = x.shape
    a, c = attn.shape[1], y.shape[1]
    row = lambda width: pl.BlockSpec((tm, width), lambda i: (i, 0))
    vmem = 2 * tm * (a + c) * 2 + 4 * tm * D * 4 + (a + c) * D * 2 + 3 * tm * D * 4
    return pl.pallas_call(
        _outproj_kernel,
        grid=(T // tm,),
        in_specs=[row(a), row(c), row(D), _const_spec((a, D)), _const_spec((c, D)), _const_spec((1, D))],
        out_specs=row(D),
        out_shape=jax.ShapeDtypeStruct((T, D), F32),
        compiler_params=_params(("parallel",), vmem),
        name="outproj",
    )(attn, y, x, w[:a], w[a:], g)


def _memkv_kernel(m_ref, g_ref, wk_ref, wv_ref, k_ref, v_ref):
    mn = _rms(m_ref[...], g_ref[...]).astype(BF16)
    k_ref[...] = _dot(mn, wk_ref[...]).astype(BF16)
    v_ref[...] = _dot(mn, wv_ref[...]).astype(BF16)


def _memkv(mem, g, wk, wv, tm):
    T, D = mem.shape
    row = pl.BlockSpec((tm, D), lambda i: (i, 0))
    vmem = 2 * tm * D * 4 + 2 * D * D * 2 + 4 * tm * D * 2 + 3 * tm * D * 4
    return pl.pallas_call(
        _memkv_kernel,
        grid=(T // tm,),
        in_specs=[row, _const_spec((1, D)), _const_spec((D, D)), _const_spec((D, D))],
        out_specs=[row, row],
        out_shape=[jax.ShapeDtypeStruct((T, D), BF16)] * 2,
        compiler_params=_params(("parallel",), vmem),
        name="mem_kv",
    )(mem, g, wk, wv)


def _cross_kernel(x_ref, k_ref, v_ref, gpre_ref, wq_ref, wo_ref, gpost_ref, o_ref, *, q_scale):
    x = x_ref[...]
    xn = _rms(x, gpre_ref[...]).astype(BF16)
    q = (_dot(xn, wq_ref[...]) * q_scale).astype(BF16)
    hd = q.shape[1] // X_HEADS
    outs = []
    for h in range(X_HEADS):
        sl = slice(h * hd, (h + 1) * hd)
        s = _dot_nt(q[:, sl], k_ref[0, :, sl])
        p = jnp.exp2(s - jnp.max(s, axis=-1, keepdims=True))
        l = jnp.sum(p, axis=-1, keepdims=True)
        outs.append((_dot(p.astype(BF16), v_ref[0, :, sl]) / l).astype(BF16))
    z = _dot(jnp.concatenate(outs, axis=1), wo_ref[...])
    o_ref[...] = x + _rms(z, gpost_ref[...])


def _cross_block(x, k, v, gpre, wq, wo, gpost, L, tm):
    T, D = x.shape
    n_mem = k.shape[1]
    blocks_per_seq = L // tm
    row = pl.BlockSpec((tm, D), lambda i: (i, 0))
    kv = pl.BlockSpec((1, n_mem, D), lambda i: (i // blocks_per_seq, 0, 0))
    vmem = 4 * tm * D * 4 + 4 * n_mem * D * 2 + 2 * D * D * 2 + 6 * tm * D * 4
    q_scale = (D // X_HEADS) ** -0.5 * LOG2E
    return pl.pallas_call(
        functools.partial(_cross_kernel, q_scale=q_scale),
        grid=(T // tm,),
        in_specs=[row, kv, kv, _const_spec((1, D)), _const_spec((D, D)), _const_spec((D, D)), _const_spec((1, D))],
        out_specs=row,
        out_shape=jax.ShapeDtypeStruct((T, D), F32),
        compiler_params=_params(("parallel",), vmem),
        name="cross_attn",
    )(x, k, v, gpre, wq, wo, gpost)


def _swiglu_kernel(x_ref, gpre_ref, wg_ref, wu_ref, wd_ref, gpost_ref, o_ref, *, n_split):
    x = x_ref[...]
    xn = _rms(x, gpre_ref[...]).astype(BF16)
    ff = wg_ref.shape[1]
    cw = ff // n_split
    z = None
    for c in range(n_split):
        sl = slice(c * cw, (c + 1) * cw)
        gate = _dot(xn, wg_ref[:, sl])
        up = _dot(xn, wu_ref[:, sl])
        h = (gate * (1.0 / (1.0 + jnp.exp(-gate))) * up).astype(BF16)
        part = _dot(h, wd_ref[sl, :])
        z = part if z is None else z + part
    o_ref[...] = x + _rms(z, gpost_ref[...])


def _swiglu_block(x, gpre, wg, wu, wd, gpost, tm):
    T, D = x.shape
    ff = wg.shape[1]
    n_split = 2 if (ff // 2) % LANES == 0 else 1
    row = pl.BlockSpec((tm, D), lambda i: (i, 0))
    vmem = 4 * tm * D * 4 + 3 * D * ff * 2 + 4 * tm * (ff // n_split) * 4 + 4 * tm * D * 4
    return pl.pallas_call(
        functools.partial(_swiglu_kernel, n_split=n_split),
        grid=(T // tm,),
        in_specs=[row, _const_spec((1, D)), _const_spec((D, ff)), _const_spec((D, ff)),
                  _const_spec((ff, D)), _const_spec((1, D))],
        out_specs=row,
        out_shape=jax.ShapeDtypeStruct((T, D), F32),
        compiler_params=_params(("parallel",), vmem),
        name="swiglu",
    )(x, gpre, wg, wu, wd, gpost)


def _trunk(x, mem, P):
    B, L, D = x.shape
    depth = P['w_in'].shape[0]
    n_mem = mem.shape[1]
    tm = min(512, L)
    tq, tk = min(512, L), min(1024, L)
    tc = LANES
    a = DA_HEADS * 2 * DA_HEAD_DIM
    tables = _rope_tables(L)
    xf = x.reshape(B * L, D)
    memf = mem.reshape(B * n_mem, D)
    row = lambda v: v.reshape(1, -1)
    for l in range(depth):
        lam_init = 0.8 - 0.6 * math.exp(-0.3 * l)
        q, k, vt, hy = _inproj(xf, row(P['ln_mix_pre'][l]), P['w_in'][l].astype(BF16), tables, B, L, tm, tk)
        lam_params = jnp.stack([P['lambda_q1'][l], P['lambda_k1'][l], P['lambda_q2'][l], P['lambda_k2'][l]])
        attn = _diff_attention(lam_params, q.reshape(B, L, a), k.reshape(B, L, a), vt,
                               P['subln_g'][l], lam_init, tq, tk)
        kf = _hyena_filter_spectrum(L, P['filt_w1'][l], row(P['filt_b1'][l]), row(P['filt_freq1'][l]),
                                    P['filt_w2'][l], row(P['filt_b2'][l]), row(P['filt_freq2'][l]),
                                    P['filt_w3'][l], tc)
        y = _hyena_operator(hy.reshape(B, L, -1), P['conv_w'][l], row(P['conv_b'][l]),
                            row(P['hyena_d'][l]), kf)
        xf = _outproj(attn.reshape(B * L, a), y.reshape(B * L, -1), xf, P['w_out'][l].astype(BF16),
                      row(P['ln_mix_post'][l]), tm)
        km, vm = _memkv(memf, row(P['ln_mem'][l]), P['wk_x'][l].astype(BF16), P['wv_x'][l].astype(BF16),
                        min(512, B * n_mem))
        xf = _cross_block(xf, km.reshape(B, n_mem, D), vm.reshape(B, n_mem, D), row(P['ln_x_pre'][l]),
                          P['wq_x'][l].astype(BF16), P['wo_x'][l].astype(BF16), row(P['ln_x_post'][l]), L, tm)
        xf = _swiglu_block(xf, row(P['ln_ffn_pre'][l]), P['w_gate'][l].astype(BF16), P['w_up'][l].astype(BF16),
                           P['w_down'][l].astype(BF16), row(P['ln_ffn_post'][l]), tm)
    return xf.reshape(B, L, D)


def kernel(x_prompt, x_sample, mem_prompt, mem_sample, ln_mix_pre, ln_mix_post, w_in, lambda_q1, lambda_k1, lambda_q2, lambda_k2, subln_g, conv_w, conv_b, filt_w1, filt_b1, filt_freq1, filt_w2, filt_b2, filt_freq2, filt_w3, hyena_d, w_out, ln_x_pre, ln_x_post, ln_mem, wq_x, wk_x, wv_x, wo_x, ln_ffn_pre, ln_ffn_post, w_gate, w_up, w_down):
    P = dict(ln_mix_pre=ln_mix_pre, ln_mix_post=ln_mix_post, w_in=w_in,
             lambda_q1=lambda_q1, lambda_k1=lambda_k1, lambda_q2=lambda_q2, lambda_k2=lambda_k2,
             subln_g=subln_g, conv_w=conv_w, conv_b=conv_b,
             filt_w1=filt_w1, filt_b1=filt_b1, filt_freq1=filt_freq1,
             filt_w2=filt_w2, filt_b2=filt_b2, filt_freq2=filt_freq2, filt_w3=filt_w3,
             hyena_d=hyena_d, w_out=w_out,
             ln_x_pre=ln_x_pre, ln_x_post=ln_x_post, ln_mem=ln_mem,
             wq_x=wq_x, wk_x=wk_x, wv_x=wv_x, wo_x=wo_x,
             ln_ffn_pre=ln_ffn_pre, ln_ffn_post=ln_ffn_post,
             w_gate=w_gate, w_up=w_up, w_down=w_down)
    return (_trunk(x_prompt, mem_prompt, P), _trunk(x_sample, mem_sample, P))
```

```python
import functools
import math

import numpy as np
import jax
import jax.numpy as jnp
from jax import lax
from jax.experimental import pallas as pl
from jax.experimental.pallas import tpu as pltpu

F32 = jnp.float32
BF16 = jnp.bfloat16

DA_HEADS = 4
DA_HEAD_DIM = 64
DA_V_DIM = 128
ROPE_DIM = 16
ROPE_THETA = 500000.0
SHORT_CONV = 3
FILTER_EMB = 33
FILTER_ORDER = 64
FAST_DECAY_PCT = 0.3
SLOW_DECAY_PCT = 1.5
DECAY_TARGET = 1e-2
X_HEADS = 4
EPS = 1e-6
LOG2E = 1.4426950408889634

LANES = 128
SUBLANES = 8
BF16_SUBLANES = 16
VMEM_LIMIT_CAP = 58 * 2**20

STAGE1_UNROLL = 8
STAGE2_UNROLL = 4


def _round_up(a, b):
    return -(-a // b) * b


def _params(semantics, vmem_bytes):
    limit = min(int(vmem_bytes * 1.25) + (4 << 20), VMEM_LIMIT_CAP)
    return pltpu.CompilerParams(dimension_semantics=semantics, vmem_limit_bytes=limit)


def _const_spec(shape):
    nd = len(shape)
    return pl.BlockSpec(shape, lambda *_: (0,) * nd, pipeline_mode=pl.Buffered(1))


def _rms(x, g):
    ms = jnp.mean(x * x, axis=-1, keepdims=True)
    return x * lax.rsqrt(ms + EPS) * g


def _dot(a, b):
    return jnp.dot(a, b, preferred_element_type=F32)


def _dot_nt(a, b):
    return lax.dot_general(a, b, (((1,), (1,)), ((), ())), preferred_element_type=F32)


def _rope_tables(L):
    inv = ROPE_THETA ** (-np.arange(0, ROPE_DIM, 2, dtype=np.float64) / ROPE_DIM)
    ang = np.arange(L, dtype=np.float64)[:, None] * inv[None, :]
    half = ROPE_DIM // 2
    cos = np.ones((L, LANES)); sa = np.zeros((L, LANES)); sb = np.zeros((L, LANES))
    for g in range(LANES // DA_HEAD_DIM):
        o = g * DA_HEAD_DIM
        cos[:, o:o + half] = np.cos(ang)
        cos[:, o + half:o + ROPE_DIM] = np.cos(ang)
        sb[:, o:o + half] = -np.sin(ang)
        sa[:, o + half:o + ROPE_DIM] = np.sin(ang)
    return (jnp.asarray(cos, F32), jnp.asarray(sa, F32), jnp.asarray(sb, F32))


def _inproj_kernel(x_ref, g_ref, w_ref, wvt_ref, cos_ref, sa_ref, sb_ref,
                   q_ref, k_ref, vt_ref, hy_ref, *, q_scale):
    xn = _rms(x_ref[...], g_ref[...]).astype(BF16)
    cos, sa, sb = cos_ref[...], sa_ref[...], sb_ref[...]
    half = ROPE_DIM // 2

    def rope(y):
        parts = []
        for s in range(y.shape[1] // LANES):
            ys = y[:, s * LANES:(s + 1) * LANES]
            parts.append(ys * cos + pltpu.roll(ys, half, 1) * sa
                         + pltpu.roll(ys, LANES - half, 1) * sb)
        return jnp.concatenate(parts, axis=1)

    a = q_ref.shape[1]
    q_ref[...] = (rope(_dot(xn, w_ref[:, 0:a])) * q_scale).astype(BF16)
    k_ref[...] = rope(_dot(xn, w_ref[:, a:2 * a])).astype(BF16)
    vt_ref[0, 0] = _dot_nt(wvt_ref[...], xn).astype(BF16)
    hy_ref[...] = _dot(xn, w_ref[:, 3 * a:]).astype(BF16)


def _inproj(x, g, w, tables, B, L, tm, tk):
    T, D = x.shape
    n_out = w.shape[1]
    a = DA_HEADS * 2 * DA_HEAD_DIM
    hyw = n_out - 3 * a
    blocks_per_seq = L // tm
    tiles_per_chunk = tk // tm
    tab_spec = pl.BlockSpec((tm, LANES), lambda i: (i % blocks_per_seq, 0))
    row = lambda width: pl.BlockSpec((tm, width), lambda i: (i, 0))
    vt_spec = pl.BlockSpec((1, 1, a, tm), lambda i: (i // blocks_per_seq, (i % blocks_per_seq) // tiles_per_chunk,
                                                     0, i % tiles_per_chunk))
    vmem = 2 * tm * D * 4 + D * (n_out + a) * 2 + 6 * tm * LANES * 4 + 2 * tm * n_out * 2 + 4 * tm * 1536 * 4
    q_scale = DA_HEAD_DIM ** -0.5 * LOG2E
    wvt = w[:, 2 * a:3 * a].T
    return pl.pallas_call(
        functools.partial(_inproj_kernel, q_scale=q_scale),
        grid=(T // tm,),
        in_specs=[row(D), _const_spec((1, D)), _const_spec((D, n_out)), _const_spec((a, D)),
                  tab_spec, tab_spec, tab_spec],
        out_specs=[row(a), row(a), vt_spec, row(hyw)],
        out_shape=[jax.ShapeDtypeStruct((T, a), BF16)] * 2
                  + [jax.ShapeDtypeStruct((B, L // tk, a, tk), BF16), jax.ShapeDtypeStruct((T, hyw), BF16)],
        compiler_params=_params(("parallel",), vmem),
        name="inproj",
    )(x, g, w, wvt, *tables)


def _attn_kernel(lam_ref, q_ref, k_ref, vt_ref, g_ref, o_ref, s_ref, *, tk, kb, lam_init):
    q = q_ref[0]
    tq = q.shape[0]
    L = k_ref.shape[1]
    lane = lax.broadcasted_iota(jnp.int32, q.shape, 1)
    zero = jnp.zeros_like(q)
    qz = (jnp.where(lane < DA_HEAD_DIM, q, zero), jnp.where(lane >= DA_HEAD_DIM, q, zero))

    nk, nb = L // tk, tk // kb
    neg = jnp.full((1, tq), -1e30, F32)

    def score_block(j, c, b, mx):
        start = pl.multiple_of(j * tk + b * kb, kb)
        s = _dot_nt(k_ref[0, pl.ds(start, kb), :], qz[c])
        s_ref[c, b * kb:(b + 1) * kb, :] = s
        return jnp.maximum(mx, jnp.max(s, axis=0, keepdims=True))

    def value_block(j, c, b, mn, lsum, pv):
        p = jnp.exp2(s_ref[c, b * kb:(b + 1) * kb, :] - mn)
        lsum = lsum + jnp.sum(p, axis=0, keepdims=True)
        pv = pv + _dot(vt_ref[0, j, :, b * kb:(b + 1) * kb], p.astype(BF16))
        return lsum, pv

    def step(score_of, value_of, mx_cur, state):
        m, l, acc = state
        mn = jnp.maximum(m, mx_cur)
        alpha = jnp.exp2(m - mn)
        lsum, pv, mx = jnp.zeros((1, tq), F32), jnp.zeros((DA_V_DIM, tq), F32), neg
        for b in range(nb):
            if score_of is not None:
                mx = score_block(*score_of, b, mx)
            lsum, pv = value_block(*value_of, b, mn, lsum, pv)
        return mx, (mn, alpha * l + lsum, alpha * acc + pv)

    mx0 = neg
    for b in range(nb):
        mx0 = score_block(0, 0, b, mx0)
    fresh = (neg, jnp.zeros((1, tq), F32), jnp.zeros((DA_V_DIM, tq), F32))

    def chunk(j, carry):
        mx0, st0, st1 = carry
        mx1, st0 = step((j, 1), (j, 0), mx0, st0)
        mx0, st1 = step((j + 1, 0), (j, 1), mx1, st1)
        return mx0, st0, st1

    mx0, st0, st1 = lax.fori_loop(0, nk - 1, chunk, (mx0, fresh, fresh), unroll=True)
    mx1, st0 = step((nk - 1, 1), (nk - 1, 0), mx0, st0)
    _, st1 = step(None, (nk - 1, 1), mx1, st1)
    (_, l0, a0), (_, l1, a1) = st0, st1

    lp = lam_ref[...]
    lam = (jnp.exp(jnp.sum(lp[0:1] * lp[1:2], axis=-1, keepdims=True))
           - jnp.exp(jnp.sum(lp[2:3] * lp[3:4], axis=-1, keepdims=True)) + lam_init)
    o = a0 / l0 - lam * (a1 / l1)
    ms = jnp.mean(o * o, axis=0, keepdims=True)
    g = jnp.concatenate([g_ref[...]] * (tq // LANES), axis=1)
    o = o * lax.rsqrt(ms + EPS) * g * (1.0 - lam_init)
    o_ref[0] = o.T.astype(BF16)


def _diff_attention(lam_params, q, k, vt, g, lam_init, tq, tk):
    B, L, A = q.shape
    qspec = pl.BlockSpec((1, tq, LANES), lambda b, h, i: (b, i, h))
    kspec = pl.BlockSpec((1, L, LANES), lambda b, h, i: (b, 0, h))
    vspec = pl.BlockSpec((1, L // tk, LANES, tk), lambda b, h, i: (b, 0, h, 0))
    g_cols = jnp.broadcast_to(g.reshape(DA_V_DIM, 1), (DA_V_DIM, LANES))
    kb = min(512, tk)
    vmem = 4 * L * LANES * 2 + 4 * tq * LANES * 2 + 2 * tq * tk * 4 + 4 * tq * kb * 6 + 16 * tq * LANES * 4
    return pl.pallas_call(
        functools.partial(_attn_kernel, tk=tk, kb=kb, lam_init=lam_init),
        grid=(B, DA_HEADS, L // tq),
        in_specs=[_const_spec(lam_params.shape), qspec, kspec, vspec, _const_spec((DA_V_DIM, LANES))],
        out_specs=qspec,
        out_shape=jax.ShapeDtypeStruct((B, L, A), BF16),
        scratch_shapes=[pltpu.VMEM((2, tk, tq), F32)],
        compiler_params=_params(("parallel", "parallel", "parallel"), vmem),
        name="diff_attn",
    )(lam_params, q, k, vt, g_cols)


class _FftPlan:
    def __init__(self, L):
        n = 2 * L
        lg = int(math.log2(n))
        assert 2 ** lg == n
        self.L, self.n = L, n
        self.N1 = 2 ** (lg // 2)
        self.N2 = n // self.N1
        self.K1 = self.N1 // 2 + 1
        self.R = _round_up(2 * self.K1, BF16_SUBLANES)
        self.P = self.N2 + SUBLANES
        assert (self.P // SUBLANES) % 2 == 1
        assert self.N2 % BF16_SUBLANES == 0 and (self.N1 // 2) % BF16_SUBLANES == 0


@functools.lru_cache(maxsize=None)
def _fft_consts(L):
    p = _FftPlan(L)
    N1, N2, K1, R, n = p.N1, p.N2, p.K1, p.R, p.n
    t2 = np.arange(N2, dtype=np.float64)[:, None, None]
    k1 = np.arange(K1, dtype=np.float64)[None, :, None]
    t1 = np.arange(N1, dtype=np.float64)[None, None, :]
    ang = -2.0 * np.pi * (t1 * k1 / N1 + t2 * k1 / n)
    f1 = np.zeros((N2, R, N1))
    f1[:, 0:2 * K1:2, :] = np.cos(ang)
    f1[:, 1:2 * K1:2, :] = np.sin(ang)
    c = np.where((np.arange(K1) == 0) | (np.arange(K1) == N1 // 2), 1.0, 2.0)[None, :, None]
    th = -ang[:, :, :N1 // 2]
    g1 = np.zeros((N2, N1 // 2, R))
    g1[:, :, 0:2 * K1:2] = np.transpose(c * np.cos(th), (0, 2, 1))
    g1[:, :, 1:2 * K1:2] = np.transpose(-c * np.sin(th), (0, 2, 1))
    a2 = 2.0 * np.pi * np.outer(np.arange(N2), np.arange(N2)) / N2
    C, S = np.cos(a2), np.sin(a2)
    f2 = np.block([[C, S], [-S, C]])
    return (jnp.asarray(f1, BF16), jnp.asarray(g1, BF16), jnp.asarray(f2, BF16), jnp.asarray(f2.T, BF16))


def _load_rows(ref, idx):
    parts = [ref[g, idx, :] for g in range(ref.shape[0])]
    return parts[0] if len(parts) == 1 else jnp.concatenate(parts, axis=1)


def _store_rows(ref, idx, val):
    for g in range(ref.shape[0]):
        ref[g, idx, :] = val[:, g * LANES:(g + 1) * LANES]


def _store_slabs(ref, first_slab, val, plan):
    for s in range(val.shape[0] // plan.N2):
        start = pl.multiple_of((first_slab + s) * plan.P, SUBLANES)
        _store_rows(ref, pl.ds(start, plan.N2), val[s * plan.N2:(s + 1) * plan.N2])


def _load_slabs(ref, first_slab, count, plan):
    parts = [_load_rows(ref, pl.ds(pl.multiple_of((first_slab + s) * plan.P, SUBLANES), plan.N2))
             for s in range(count)]
    return parts[0] if count == 1 else jnp.concatenate(parts, axis=0)


def _dft_stage1(src_ref, f1_ref, a_ref, n_slabs, plan):
    def body(t2, _):
        rows = _load_rows(src_ref, pl.ds(t2, n_slabs, stride=plan.P)).astype(BF16)
        _store_rows(a_ref, pl.ds(t2, plan.R, stride=plan.P), _dot(f1_ref[t2], rows))
        return 0

    lax.fori_loop(0, plan.N2, body, 0, unroll=STAGE1_UNROLL)


@functools.lru_cache(maxsize=None)
def _filter_features(L):
    t = np.linspace(0.0, 1.0, L)[:, None]
    bands = (FILTER_EMB - 1) // 2
    w = 2.0 * np.pi * np.arange(L)[:, None] / L
    f = np.linspace(1e-4, bands - 1, bands)[None, :]
    z = np.concatenate([t, np.cos(f * w), -np.sin(f * w)], axis=-1)
    z_rev = np.concatenate([z[:1], z[:0:-1]], axis=0)
    zz = np.zeros((2 * L, LANES))
    zz[:L, :FILTER_EMB] = z
    zz[L:, :FILTER_EMB] = z_rev
    return jnp.asarray(zz, BF16)


def _decay_rates(width):
    max_decay = math.log(DECAY_TARGET) / FAST_DECAY_PCT
    min_decay = math.log(DECAY_TARGET) / SLOW_DECAY_PCT
    return jnp.asarray(np.abs(np.linspace(min_decay, max_decay, width))[None, :], F32)


def _filter_kernel(z_ref, w1_ref, b1_ref, fr1_ref, w2_ref, b2_ref, fr2_ref,
                   w3f_ref, w3b_ref, dl_ref, f1_ref, f2_ref, kf_ref,
                   h_ref, kern_ref, a_ref, *, plan, ch):
    L, N1, N2, K1 = plan.L, plan.N1, plan.N2, plan.K1
    n_ch = (2 * L) // ch

    @pl.when(pl.program_id(0) == 0)
    def _():
        def mlp(i, _):
            r0 = pl.multiple_of(i * ch, ch)
            h = jnp.sin(fr1_ref[...] * (_dot(z_ref[pl.ds(r0, ch), :], w1_ref[...]) + b1_ref[...]))
            h = jnp.sin(fr2_ref[...] * (_dot(h.astype(BF16), w2_ref[...]) + b2_ref[...]))
            h_ref[pl.ds(r0, ch), :] = h.astype(BF16)
            return 0
        lax.fori_loop(0, n_ch, mlp, 0)

    tc = kf_ref.shape[1]
    local = lax.broadcasted_iota(jnp.int32, (ch, tc), 0)

    def synth(i, asum):
        r0 = pl.multiple_of(i * ch, ch)
        row = local + r0
        h = h_ref[pl.ds(r0, ch), :]
        fwd = _dot(h, w3f_ref[...])
        bwd = _dot(h, w3b_ref[...])
        lag = jnp.where(row < L, row, 2 * L - row).astype(F32)
        win = jnp.exp(lag * (-1.0 / (L - 1)) * dl_ref[...])
        val = jnp.where(row < L, fwd + jnp.where(row == 0, bwd, 0.0), bwd) * win
        val = jnp.where(row == L, 0.0, val)
        _store_slabs(kern_ref, i * (ch // N2), val, plan)
        return asum + jnp.sum(jnp.abs(val), axis=0, keepdims=True)

    asum = lax.fori_loop(0, n_ch, synth, jnp.zeros((1, tc), F32))
    scale = 1.0 / (asum * float(plan.n))

    _dft_stage1(kern_ref, f1_ref, a_ref, N1, plan)

    def stage2(k1, _):
        r0 = pl.multiple_of(k1 * 2 * N2, 2 * N2)
        slab = _load_slabs(a_ref, 2 * k1, 2, plan).astype(BF16)
        kf_ref[pl.ds(r0, 2 * N2), :] = (_dot(f2_ref[...], slab) * scale).astype(kf_ref.dtype)
        return 0

    lax.fori_loop(0, K1, stage2, 0, unroll=STAGE2_UNROLL)
    pad = plan.R - 2 * K1
    if pad:
        kf_ref[pl.ds(2 * K1 * N2, pad * N2), :] = jnp.zeros((pad * N2, tc), kf_ref.dtype)


def _hyena_filter_spectrum(L, w1, b1, fr1, w2, b2, fr2, w3, tc):
    plan = _FftPlan(L)
    C = w3.shape[1] // 2
    zz = _filter_features(L)
    f1, _, f2, _ = _fft_consts(L)
    w1p = jnp.zeros((LANES, FILTER_ORDER), BF16).at[:FILTER_EMB].set(w1.astype(BF16))
    ch = min(512, L)
    rows_a = plan.R * plan.N2
    cblock = lambda off: pl.BlockSpec((FILTER_ORDER, tc), lambda c: (0, off + c))
    vmem = (2 * L * LANES * 2 * 2 + 2 * L * tc * 4 + rows_a * tc * 4 + 2 * rows_a * tc * 2
            + f1.size * 2 * 2 + 8 * ch * tc * 4)
    return pl.pallas_call(
        functools.partial(_filter_kernel, plan=plan, ch=ch),
        grid=(C // tc,),
        in_specs=[_const_spec(zz.shape),
                  _const_spec(w1p.shape), _const_spec((1, FILTER_ORDER)), _const_spec((1, FILTER_ORDER)),
                  _const_spec((FILTER_ORDER, FILTER_ORDER)), _const_spec((1, FILTER_ORDER)),
                  _const_spec((1, FILTER_ORDER)),
                  cblock(0), cblock(C // tc),
                  pl.BlockSpec((1, tc), lambda c: (0, c)),
                  _const_spec(f1.shape), _const_spec(f2.shape)],
        out_specs=pl.BlockSpec((rows_a, tc), lambda c: (0, c)),
        out_shape=jax.ShapeDtypeStruct((rows_a, C), BF16),
        scratch_shapes=[pltpu.VMEM((2 * L, FILTER_ORDER), BF16),
                        pltpu.VMEM((tc // LANES, plan.N1 * plan.P, LANES), F32),
                        pltpu.VMEM((tc // LANES, plan.R * plan.P, LANES), F32)],
        compiler_params=_params(("arbitrary",), vmem),
        name="hyena_filter",
    )(zz, w1p, b1, fr1, w2.astype(BF16), b2, fr2, w3.astype(BF16), w3.astype(BF16),
      _decay_rates(C), f1, f2)


def _short_conv_chunk(raw_ref, w_ref, b_ref, r0, ch, L):
    halo = BF16_SUBLANES
    c = raw_ref[0, pl.ds(r0, ch), :].astype(F32)
    p0 = pl.multiple_of(jnp.maximum(r0 - halo, 0), halo)
    n0 = pl.multiple_of(jnp.minimum(r0 + ch, L - halo), halo)
    prev = raw_ref[0, pl.ds(p0, halo), :].astype(F32)[halo - 1:halo]
    nxt = raw_ref[0, pl.ds(n0, halo), :].astype(F32)[0:1]
    prev = jnp.where(r0 == 0, 0.0, prev)
    nxt = jnp.where(r0 + ch == L, 0.0, nxt)
    rows = lax.broadcasted_iota(jnp.int32, c.shape, 0)
    up = jnp.where(rows == 0, prev, pltpu.roll(c, 1, 0))
    un = jnp.where(rows == ch - 1, nxt, pltpu.roll(c, ch - 1, 0))
    w = w_ref[...]
    return w[0:1] * up + w[1:2] * c + w[2:3] * un + b_ref[...]


def _hyena_kernel(x0_ref, x1_ref, vh_ref, w0_ref, w1_ref, wv_ref, b0_ref, b1_ref, bv_ref,
                  d_ref, kf_ref, f1_ref, g1_ref, f2_ref, f2i_ref, o_ref,
                  u_ref, a_ref, y_ref, *, plan, ch):
    L, N1, N2, K1, R = plan.L, plan.N1, plan.N2, plan.K1, plan.R
    n_ch = L // ch

    def gate_in(i, _):
        r0 = pl.multiple_of(i * ch, ch)
        x1 = _short_conv_chunk(x1_ref, w1_ref, b1_ref, r0, ch, L)
        vh = _short_conv_chunk(vh_ref, wv_ref, bv_ref, r0, ch, L)
        _store_slabs(u_ref, i * (ch // N2), x1 * vh, plan)
        return 0

    lax.fori_loop(0, n_ch, gate_in, 0)

    _dft_stage1(u_ref, f1_ref, a_ref, N1 // 2, plan)

    def spectral(k1, _):
        r0 = pl.multiple_of(k1 * 2 * N2, 2 * N2)
        x = _dot(f2_ref[...], _load_slabs(a_ref, 2 * k1, 2, plan).astype(BF16))
        xr, xi = x[:N2], x[N2:]
        kr = kf_ref[pl.ds(r0, N2), :].astype(F32)
        ki = kf_ref[pl.ds(r0 + N2, N2), :].astype(F32)
        y = jnp.concatenate([xr * kr - xi * ki, xr * ki + xi * kr], axis=0).astype(BF16)
        _store_slabs(a_ref, 2 * k1, _dot(f2i_ref[...], y), plan)
        return 0

    lax.fori_loop(0, K1, spectral, 0, unroll=STAGE2_UNROLL)

    def inverse1(t2, _):
        rows = _load_rows(a_ref, pl.ds(t2, R, stride=plan.P)).astype(BF16)
        _store_rows(y_ref, pl.ds(t2, N1 // 2, stride=plan.P), _dot(g1_ref[t2], rows))
        return 0

    lax.fori_loop(0, N2, inverse1, 0, unroll=STAGE1_UNROLL)

    def gate_out(i, _):
        r0 = pl.multiple_of(i * ch, ch)
        x0 = _short_conv_chunk(x0_ref, w0_ref, b0_ref, r0, ch, L)
        u = _load_slabs(u_ref, i * (ch // N2), ch // N2, plan)
        y = _load_slabs(y_ref, i * (ch // N2), ch // N2, plan)
        o_ref[0, pl.ds(r0, ch), :] = ((y + d_ref[...] * u) * x0).astype(o_ref.dtype)
        return 0

    lax.fori_loop(0, n_ch, gate_out, 0)


def _hyena_vmem_bytes(plan, tc, ch):
    L, rows_a = plan.L, plan.R * plan.N2
    pad_l = lambda m: _round_up(m, LANES)
    return (6 * L * tc * 2 + rows_a * tc * 2 + 2 * L * tc * 4 + rows_a * tc * 4 + 2 * L * tc * 2
            + plan.N2 * plan.R * pad_l(plan.N1 // 2) * 2 + plan.N2 * (plan.N1 // 2) * pad_l(plan.R) * 2
            + 16 * ch * tc * 4 + 16 * plan.N2 * tc * 4)


def _hyena_operator(hy, conv_w, conv_b, d, kf):
    B, L, W3 = hy.shape
    C = W3 // 3
    plan = _FftPlan(L)
    ch = min(512, L)
    tc = 2 * LANES if int(_hyena_vmem_bytes(plan, 2 * LANES, ch) * 1.25) + (4 << 20) <= VMEM_LIMIT_CAP else LANES
    nb = C // tc
    f1full, g1, f2, f2i = _fft_consts(L)
    f1 = f1full[:, :, :plan.N1 // 2]
    rows_a = plan.R * plan.N2
    seq = lambda part: pl.BlockSpec((1, L, tc), lambda c, b: (b, 0, part * nb + c))
    wspec = lambda part: pl.BlockSpec((SHORT_CONV, tc), lambda c, b: (0, part * nb + c))
    bspec = lambda part: pl.BlockSpec((1, tc), lambda c, b: (0, part * nb + c))
    vmem = _hyena_vmem_bytes(plan, tc, ch)
    return pl.pallas_call(
        functools.partial(_hyena_kernel, plan=plan, ch=ch),
        grid=(nb, B),
        in_specs=[seq(0), seq(1), seq(2), wspec(0), wspec(1), wspec(2), bspec(0), bspec(1), bspec(2),
                  pl.BlockSpec((1, tc), lambda c, b: (0, c)),
                  pl.BlockSpec((rows_a, tc), lambda c, b: (0, c), pipeline_mode=pl.Buffered(1)),
                  _const_spec(f1.shape), _const_spec(g1.shape), _const_spec(f2.shape), _const_spec(f2i.shape)],
        out_specs=pl.BlockSpec((1, L, tc), lambda c, b: (b, 0, c)),
        out_shape=jax.ShapeDtypeStruct((B, L, C), BF16),
        scratch_shapes=[pltpu.VMEM((tc // LANES, (plan.N1 // 2) * plan.P, LANES), F32),
                        pltpu.VMEM((tc // LANES, plan.R * plan.P, LANES), F32),
                        pltpu.VMEM((tc // LANES, (plan.N1 // 2) * plan.P, LANES), F32)],
        compiler_params=_params(("parallel", "parallel"), vmem),
        name="hyena_op",
    )(hy, hy, hy, conv_w, conv_w, conv_w, conv_b, conv_b, conv_b, d, kf, f1, g1, f2, f2i)


def _outproj_kernel(a_ref, y_ref, x_ref, wa_ref, wy_ref, g_ref, o_ref):
    z = _dot(a_ref[...], wa_ref[...]) + _dot(y_ref[...], wy_ref[...])
    o_ref[...] = x_ref[...] + _rms(z, g_ref[...])


def _outproj(attn, y, x, w, g, tm):
    T, D = x.shape
    a, c = attn.shape[1], y.shape[1]
    row = lambda width: pl.BlockSpec((tm, width), lambda i: (i, 0))
    vmem = 2 * tm * (a + c) * 2 + 4 * tm * D * 4 + (a + c) * D * 2 + 3 * tm * D * 4
    return pl.pallas_call(
        _outproj_kernel,
        grid=(T // tm,),
        in_specs=[row(a), row(c), row(D), _const_spec((a, D)), _const_spec((c, D)), _const_spec((1, D))],
        out_specs=row(D),
        out_shape=jax.ShapeDtypeStruct((T, D), F32),
        compiler_params=_params(("parallel",), vmem),
        name="outproj",
    )(attn, y, x, w[:a], w[a:], g)


def _memkv_kernel(m_ref, g_ref, wk_ref, wv_ref, k_ref, v_ref):
    mn = _rms(m_ref[...], g_ref[...]).astype(BF16)
    k_ref[...] = _dot(mn, wk_ref[...]).astype(BF16)
    v_ref[...] = _dot(mn, wv_ref[...]).astype(BF16)


def _memkv(mem, g, wk, wv, tm):
    T, D = mem.shape
    row = pl.BlockSpec((tm, D), lambda i: (i, 0))
    vmem = 2 * tm * D * 4 + 2 * D * D * 2 + 4 * tm * D * 2 + 3 * tm * D * 4
    return pl.pallas_call(
        _memkv_kernel,
        grid=(T // tm,),
        in_specs=[row, _const_spec((1, D)), _const_spec((D, D)), _const_spec((D, D))],
        out_specs=[row, row],
        out_shape=[jax.ShapeDtypeStruct((T, D), BF16)] * 2,
        compiler_params=_params(("parallel",), vmem),
        name="mem_kv",
    )(mem, g, wk, wv)


def _cross_kernel(x_ref, k_ref, v_ref, gpre_ref, wq_ref, wo_ref, gpost_ref, o_ref, *, q_scale):
    x = x_ref[...]
    xn = _rms(x, gpre_ref[...]).astype(BF16)
    q = (_dot(xn, wq_ref[...]) * q_scale).astype(BF16)
    hd = q.shape[1] // X_HEADS
    outs = []
    for h in range(X_HEADS):
        sl = slice(h * hd, (h + 1) * hd)
        s = _dot_nt(q[:, sl], k_ref[0, :, sl])
        p = jnp.exp2(s - jnp.max(s, axis=-1, keepdims=True))
        l = jnp.sum(p, axis=-1, keepdims=True)
        outs.append((_dot(p.astype(BF16), v_ref[0, :, sl]) / l).astype(BF16))
    z = _dot(jnp.concatenate(outs, axis=1), wo_ref[...])
    o_ref[...] = x + _rms(z, gpost_ref[...])


def _cross_block(x, k, v, gpre, wq, wo, gpost, L, tm):
    T, D = x.shape
    n_mem = k.shape[1]
    blocks_per_seq = L // tm
    row = pl.BlockSpec((tm, D), lambda i: (i, 0))
    kv = pl.BlockSpec((1, n_mem, D), lambda i: (i // blocks_per_seq, 0, 0))
    vmem = 4 * tm * D * 4 + 4 * n_mem * D * 2 + 2 * D * D * 2 + 6 * tm * D * 4
    q_scale = (D // X_HEADS) ** -0.5 * LOG2E
    return pl.pallas_call(
        functools.partial(_cross_kernel, q_scale=q_scale),
        grid=(T // tm,),
        in_specs=[row, kv, kv, _const_spec((1, D)), _const_spec((D, D)), _const_spec((D, D)), _const_spec((1, D))],
        out_specs=row,
        out_shape=jax.ShapeDtypeStruct((T, D), F32),
        compiler_params=_params(("parallel",), vmem),
        name="cross_attn",
    )(x, k, v, gpre, wq, wo, gpost)


def _swiglu_kernel(x_ref, gpre_ref, wg_ref, wu_ref, wd_ref, gpost_ref, o_ref, *, n_split):
    x = x_ref[...]
    xn = _rms(x, gpre_ref[...]).astype(BF16)
    ff = wg_ref.shape[1]
    cw = ff // n_split
    z = None
    for c in range(n_split):
        sl = slice(c * cw, (c + 1) * cw)
        gate = _dot(xn, wg_ref[:, sl])
        up = _dot(xn, wu_ref[:, sl])
        h = (gate * (1.0 / (1.0 + jnp.exp(-gate))) * up).astype(BF16)
        part = _dot(h, wd_ref[sl, :])
        z = part if z is None else z + part
    o_ref[...] = x + _rms(z, gpost_ref[...])


def _swiglu_block(x, gpre, wg, wu, wd, gpost, tm):
    T, D = x.shape
    ff = wg.shape[1]
    n_split = 2 if (ff // 2) % LANES == 0 else 1
    row = pl.BlockSpec((tm, D), lambda i: (i, 0))
    vmem = 4 * tm * D * 4 + 3 * D * ff * 2 + 4 * tm * (ff // n_split) * 4 + 4 * tm * D * 4
    return pl.pallas_call(
        functools.partial(_swiglu_kernel, n_split=n_split),
        grid=(T // tm,),
        in_specs=[row, _const_spec((1, D)), _const_spec((D, ff)), _const_spec((D, ff)),
                  _const_spec((ff, D)), _const_spec((1, D))],
        out_specs=row,
        out_shape=jax.ShapeDtypeStruct((T, D), F32),
        compiler_params=_params(("parallel",), vmem),
        name="swiglu",
    )(x, gpre, wg, wu, wd, gpost)


def _trunk(x, mem, P):
    B, L, D = x.shape
    depth = P['w_in'].shape[0]
    n_mem = mem.shape[1]
    tm = min(512, L)
    tq, tk = min(512, L), min(1024, L)
    tc = LANES
    a = DA_HEADS * 2 * DA_HEAD_DIM
    tables = _rope_tables(L)
    xf = x.reshape(B * L, D)
    memf = mem.reshape(B * n_mem, D)
    row = lambda v: v.reshape(1, -1)
    for l in range(depth):
        lam_init = 0.8 - 0.6 * math.exp(-0.3 * l)
        q, k, vt, hy = _inproj(xf, row(P['ln_mix_pre'][l]), P['w_in'][l].astype(BF16), tables, B, L, tm, tk)
        lam_params = jnp.stack([P['lambda_q1'][l], P['lambda_k1'][l], P['lambda_q2'][l], P['lambda_k2'][l]])
        attn = _diff_attention(lam_params, q.reshape(B, L, a), k.reshape(B, L, a), vt,
                               P['subln_g'][l], lam_init, tq, tk)
        kf = _hyena_filter_spectrum(L, P['filt_w1'][l], row(P['filt_b1'][l]), row(P['filt_freq1'][l]),
                                    P['filt_w2'][l], row(P['filt_b2'][l]), row(P['filt_freq2'][l]),
                                    P['filt_w3'][l], tc)
        y = _hyena_operator(hy.reshape(B, L, -1), P['conv_w'][l], row(P['conv_b'][l]),
                            row(P['hyena_d'][l]), kf)
        xf = _outproj(attn.reshape(B * L, a), y.reshape(B * L, -1), xf, P['w_out'][l].astype(BF16),
                      row(P['ln_mix_post'][l]), tm)
        km, vm = _memkv(memf, row(P['ln_mem'][l]), P['wk_x'][l].astype(BF16), P['wv_x'][l].astype(BF16),
                        min(512, B * n_mem))
        xf = _cross_block(xf, km.reshape(B, n_mem, D), vm.reshape(B, n_mem, D), row(P['ln_x_pre'][l]),
                          P['wq_x'][l].astype(BF16), P['wo_x'][l].astype(BF16), row(P['ln_x_post'][l]), L, tm)
        xf = _swiglu_block(xf, row(P['ln_ffn_pre'][l]), P['w_gate'][l].astype(BF16), P['w_up'][l].astype(BF16),
                           P['w_down'][l].astype(BF16), row(P['ln_ffn_post'][l]), tm)
    return xf.reshape(B, L, D)


def kernel(x_prompt, x_sample, mem_prompt, mem_sample, ln_mix_pre, ln_mix_post, w_in, lambda_q1, lambda_k1, lambda_q2, lambda_k2, subln_g, conv_w, conv_b, filt_w1, filt_b1, filt_freq1, filt_w2, filt_b2, filt_freq2, filt_w3, hyena_d, w_out, ln_x_pre, ln_x_post, ln_mem, wq_x, wk_x, wv_x, wo_x, ln_ffn_pre, ln_ffn_post, w_gate, w_up, w_down):
    P = dict(ln_mix_pre=ln_mix_pre, ln_mix_post=ln_mix_post, w_in=w_in,
             lambda_q1=lambda_q1, lambda_k1=lambda_k1, lambda_q2=lambda_q2, lambda_k2=lambda_k2,
             subln_g=subln_g, conv_w=conv_w, conv_b=conv_b,
             filt_w1=filt_w1, filt_b1=filt_b1, filt_freq1=filt_freq1,
             filt_w2=filt_w2, filt_b2=filt_b2, filt_freq2=filt_freq2, filt_w3=filt_w3,
             hyena_d=hyena_d, w_out=w_out,
             ln_x_pre=ln_x_pre, ln_x_post=ln_x_post, ln_mem=ln_mem,
             wq_x=wq_x, wk_x=wk_x, wv_x=wv_x, wo_x=wo_x,
             ln_ffn_pre=ln_ffn_pre, ln_ffn_post=ln_ffn_post,
             w_gate=w_gate, w_up=w_up, w_down=w_down)
    return (_trunk(x_prompt, mem_prompt, P), _trunk(x_sample, mem_sample, P))
```

```python
import functools
import math

import numpy as np
import jax
import jax.numpy as jnp
from jax import lax
from jax.experimental import pallas as pl
from jax.experimental.pallas import tpu as pltpu

F32 = jnp.float32
BF16 = jnp.bfloat16

DA_HEADS = 4
DA_HEAD_DIM = 64
DA_V_DIM = 128
ROPE_DIM = 16
ROPE_THETA = 500000.0
SHORT_CONV = 3
FILTER_EMB = 33
FILTER_ORDER = 64
FAST_DECAY_PCT = 0.3
SLOW_DECAY_PCT = 1.5
DECAY_TARGET = 1e-2
X_HEADS = 4
EPS = 1e-6
LOG2E = 1.4426950408889634

LANES = 128
SUBLANES = 8
BF16_SUBLANES = 16
VMEM_LIMIT_CAP = 58 * 2**20

STAGE1_UNROLL = 8
STAGE2_UNROLL = 4


def _round_up(a, b):
    return -(-a // b) * b


def _params(semantics, vmem_bytes):
    limit = min(int(vmem_bytes * 1.25) + (4 << 20), VMEM_LIMIT_CAP)
    return pltpu.CompilerParams(dimension_semantics=semantics, vmem_limit_bytes=limit)


def _const_spec(shape):
    nd = len(shape)
    return pl.BlockSpec(shape, lambda *_: (0,) * nd, pipeline_mode=pl.Buffered(1))


def _rms(x, g):
    ms = jnp.mean(x * x, axis=-1, keepdims=True)
    return x * lax.rsqrt(ms + EPS) * g


def _dot(a, b):
    return jnp.dot(a, b, preferred_element_type=F32)


def _dot_nt(a, b):
    return lax.dot_general(a, b, (((1,), (1,)), ((), ())), preferred_element_type=F32)


def _rope_tables(L):
    inv = ROPE_THETA ** (-np.arange(0, ROPE_DIM, 2, dtype=np.float64) / ROPE_DIM)
    ang = np.arange(L, dtype=np.float64)[:, None] * inv[None, :]
    half = ROPE_DIM // 2
    cos = np.ones((L, LANES)); sa = np.zeros((L, LANES)); sb = np.zeros((L, LANES))
    for g in range(LANES // DA_HEAD_DIM):
        o = g * DA_HEAD_DIM
        cos[:, o:o + half] = np.cos(ang)
        cos[:, o + half:o + ROPE_DIM] = np.cos(ang)
        sb[:, o:o + half] = -np.sin(ang)
        sa[:, o + half:o + ROPE_DIM] = np.sin(ang)
    return (jnp.asarray(cos, F32), jnp.asarray(sa, F32), jnp.asarray(sb, F32),
            jnp.asarray(np.cos(ang).T, F32), jnp.asarray(np.sin(ang).T, F32))


def _inproj_kernel(x_ref, g_ref, w_ref, wqt_ref, wvt_ref, cos_ref, sa_ref, sb_ref, cost_ref, sint_ref,
                   qt_ref, k_ref, vt_ref, hy_ref, *, q_scale):
    xn = _rms(x_ref[...], g_ref[...]).astype(BF16)
    cos, sa, sb = cos_ref[...], sa_ref[...], sb_ref[...]
    half = ROPE_DIM // 2

    def rope(y):
        parts = []
        for s in range(y.shape[1] // LANES):
            ys = y[:, s * LANES:(s + 1) * LANES]
            parts.append(ys * cos + pltpu.roll(ys, half, 1) * sa
                         + pltpu.roll(ys, LANES - half, 1) * sb)
        return jnp.concatenate(parts, axis=1)

    def rope_t(y):
        ct, st = cost_ref[...], sint_ref[...]
        parts = []
        for o in range(0, y.shape[0], DA_HEAD_DIM):
            x1, x2 = y[o:o + half], y[o + half:o + ROPE_DIM]
            parts += [x1 * ct - x2 * st, x2 * ct + x1 * st, y[o + ROPE_DIM:o + DA_HEAD_DIM]]
        return jnp.concatenate(parts, axis=0)

    a = k_ref.shape[1]
    qt_ref[0, 0] = (rope_t(_dot_nt(wqt_ref[...], xn)) * q_scale).astype(BF16)
    k_ref[...] = rope(_dot(xn, w_ref[:, a:2 * a])).astype(BF16)
    vt_ref[0, 0] = _dot_nt(wvt_ref[...], xn).astype(BF16)
    hy_ref[...] = _dot(xn, w_ref[:, 3 * a:]).astype(BF16)


def _inproj(x, g, w, tables, B, L, tm, tk):
    T, D = x.shape
    n_out = w.shape[1]
    a = DA_HEADS * 2 * DA_HEAD_DIM
    hyw = n_out - 3 * a
    blocks_per_seq = L // tm
    tiles_per_chunk = tk // tm
    tab_spec = pl.BlockSpec((tm, LANES), lambda i: (i % blocks_per_seq, 0))
    tabt_spec = pl.BlockSpec((ROPE_DIM // 2, tm), lambda i: (0, i % blocks_per_seq))
    row = lambda width: pl.BlockSpec((tm, width), lambda i: (i, 0))
    qt_spec = pl.BlockSpec((1, 1, a, tm), lambda i: (i // blocks_per_seq, i % blocks_per_seq, 0, 0))
    vt_spec = pl.BlockSpec((1, 1, a, tm), lambda i: (i // blocks_per_seq, (i % blocks_per_seq) // tiles_per_chunk,
                                                     0, i % tiles_per_chunk))
    vmem = 2 * tm * D * 4 + D * (n_out + 2 * a) * 2 + 8 * tm * LANES * 4 + 2 * tm * n_out * 2 + 4 * tm * 1536 * 4
    q_scale = DA_HEAD_DIM ** -0.5 * LOG2E
    wqt, wvt = w[:, 0:a].T, w[:, 2 * a:3 * a].T
    return pl.pallas_call(
        functools.partial(_inproj_kernel, q_scale=q_scale),
        grid=(T // tm,),
        in_specs=[row(D), _const_spec((1, D)), _const_spec((D, n_out)), _const_spec((a, D)), _const_spec((a, D)),
                  tab_spec, tab_spec, tab_spec, tabt_spec, tabt_spec],
        out_specs=[qt_spec, row(a), vt_spec, row(hyw)],
        out_shape=[jax.ShapeDtypeStruct((B, L // tm, a, tm), BF16), jax.ShapeDtypeStruct((T, a), BF16),
                   jax.ShapeDtypeStruct((B, L // tk, a, tk), BF16), jax.ShapeDtypeStruct((T, hyw), BF16)],
        compiler_params=_params(("parallel",), vmem),
        name="inproj",
    )(x, g, w, wqt, wvt, *tables)


def _attn_kernel(lam_ref, qt_ref, k_ref, vt_ref, g_ref, o_ref, s_ref, *, tk, kb, lam_init):
    qt = qt_ref[0, 0]
    tq = qt.shape[1]
    L = k_ref.shape[1]
    feat = lax.broadcasted_iota(jnp.int32, qt.shape, 0)
    zero = jnp.zeros_like(qt)
    qz = (jnp.where(feat < DA_HEAD_DIM, qt, zero), jnp.where(feat >= DA_HEAD_DIM, qt, zero))

    nk, nb = L // tk, tk // kb
    neg = jnp.full((1, tq), -1e30, F32)

    def score_block(j, c, b, mx):
        start = pl.multiple_of(j * tk + b * kb, kb)
        s = _dot(k_ref[0, pl.ds(start, kb), :], qz[c])
        s_ref[c, b * kb:(b + 1) * kb, :] = s
        return jnp.maximum(mx, jnp.max(s, axis=0, keepdims=True))

    def value_block(j, c, b, mn, lsum, pv):
        p = jnp.exp2(s_ref[c, b * kb:(b + 1) * kb, :] - mn)
        lsum = lsum + jnp.sum(p, axis=0, keepdims=True)
        pv = pv + _dot(vt_ref[0, j, :, b * kb:(b + 1) * kb], p.astype(BF16))
        return lsum, pv

    def step(score_of, value_of, mx_cur, state):
        m, l, acc = state
        mn = jnp.maximum(m, mx_cur)
        alpha = jnp.exp2(m - mn)
        lsum, pv, mx = jnp.zeros((1, tq), F32), jnp.zeros((DA_V_DIM, tq), F32), neg
        for b in range(nb):
            if score_of is not None:
                mx = score_block(*score_of, b, mx)
            lsum, pv = value_block(*value_of, b, mn, lsum, pv)
        return mx, (mn, alpha * l + lsum, alpha * acc + pv)

    mx0 = neg
    for b in range(nb):
        mx0 = score_block(0, 0, b, mx0)
    fresh = (neg, jnp.zeros((1, tq), F32), jnp.zeros((DA_V_DIM, tq), F32))

    def chunk(j, carry):
        mx0, st0, st1 = carry
        mx1, st0 = step((j, 1), (j, 0), mx0, st0)
        mx0, st1 = step((j + 1, 0), (j, 1), mx1, st1)
        return mx0, st0, st1

    mx0, st0, st1 = lax.fori_loop(0, nk - 1, chunk, (mx0, fresh, fresh), unroll=True)
    mx1, st0 = step((nk - 1, 1), (nk - 1, 0), mx0, st0)
    _, st1 = step(None, (nk - 1, 1), mx1, st1)
    (_, l0, a0), (_, l1, a1) = st0, st1

    lp = lam_ref[...]
    lam = (jnp.exp(jnp.sum(lp[0:1] * lp[1:2], axis=-1, keepdims=True))
           - jnp.exp(jnp.sum(lp[2:3] * lp[3:4], axis=-1, keepdims=True)) + lam_init)
    o = a0 / l0 - lam * (a1 / l1)
    ms = jnp.mean(o * o, axis=0, keepdims=True)
    g = jnp.concatenate([g_ref[...]] * (tq // LANES), axis=1)
    o = o * lax.rsqrt(ms + EPS) * g * (1.0 - lam_init)
    o_ref[0] = o.T.astype(BF16)


def _diff_attention(lam_params, qt, k, vt, g, lam_init, tq, tk):
    B, L, A = k.shape
    qspec = pl.BlockSpec((1, 1, LANES, tq), lambda b, h, i: (b, i, h, 0))
    ospec = pl.BlockSpec((1, tq, LANES), lambda b, h, i: (b, i, h))
    kspec = pl.BlockSpec((1, L, LANES), lambda b, h, i: (b, 0, h))
    vspec = pl.BlockSpec((1, L // tk, LANES, tk), lambda b, h, i: (b, 0, h, 0))
    g_cols = jnp.broadcast_to(g.reshape(DA_V_DIM, 1), (DA_V_DIM, LANES))
    kb = min(512, tk)
    vmem = 4 * L * LANES * 2 + 4 * tq * LANES * 2 + 2 * tq * tk * 4 + 4 * tq * kb * 6 + 16 * tq * LANES * 4
    return pl.pallas_call(
        functools.partial(_attn_kernel, tk=tk, kb=kb, lam_init=lam_init),
        grid=(B, DA_HEADS, L // tq),
        in_specs=[_const_spec(lam_params.shape), qspec, kspec, vspec, _const_spec((DA_V_DIM, LANES))],
        out_specs=ospec,
        out_shape=jax.ShapeDtypeStruct((B, L, A), BF16),
        scratch_shapes=[pltpu.VMEM((2, tk, tq), F32)],
        compiler_params=_params(("parallel", "parallel", "parallel"), vmem),
        name="diff_attn",
    )(lam_params, qt, k, vt, g_cols)


class _FftPlan:
    def __init__(self, L):
        n = 2 * L
        lg = int(math.log2(n))
        assert 2 ** lg == n
        self.L, self.n = L, n
        self.N1 = 2 ** (lg // 2)
        self.N2 = n // self.N1
        self.K1 = self.N1 // 2 + 1
        self.R = _round_up(2 * self.K1, BF16_SUBLANES)
        self.P = self.N2 + SUBLANES
        assert (self.P // SUBLANES) % 2 == 1
        assert self.N2 % BF16_SUBLANES == 0 and (self.N1 // 2) % BF16_SUBLANES == 0


@functools.lru_cache(maxsize=None)
def _fft_consts(L):
    p = _FftPlan(L)
    N1, N2, K1, R, n = p.N1, p.N2, p.K1, p.R, p.n
    t2 = np.arange(N2, dtype=np.float64)[:, None, None]
    k1 = np.arange(K1, dtype=np.float64)[None, :, None]
    t1 = np.arange(N1, dtype=np.float64)[None, None, :]
    ang = -2.0 * np.pi * (t1 * k1 / N1 + t2 * k1 / n)
    f1 = np.zeros((N2, R, N1))
    f1[:, 0:2 * K1:2, :] = np.cos(ang)
    f1[:, 1:2 * K1:2, :] = np.sin(ang)
    c = np.where((np.arange(K1) == 0) | (np.arange(K1) == N1 // 2), 1.0, 2.0)[None, :, None]
    th = -ang[:, :, :N1 // 2]
    g1 = np.zeros((N2, N1 // 2, R))
    g1[:, :, 0:2 * K1:2] = np.transpose(c * np.cos(th), (0, 2, 1))
    g1[:, :, 1:2 * K1:2] = np.transpose(-c * np.sin(th), (0, 2, 1))
    a2 = 2.0 * np.pi * np.outer(np.arange(N2), np.arange(N2)) / N2
    C, S = np.cos(a2), np.sin(a2)
    f2 = np.block([[C, S], [-S, C]])
    return (jnp.asarray(f1, BF16), jnp.asarray(g1, BF16), jnp.asarray(f2, BF16), jnp.asarray(f2.T, BF16))


def _load_rows(ref, idx):
    parts = [ref[g, idx, :] for g in range(ref.shape[0])]
    return parts[0] if len(parts) == 1 else jnp.concatenate(parts, axis=1)


def _store_rows(ref, idx, val):
    for g in range(ref.shape[0]):
        ref[g, idx, :] = val[:, g * LANES:(g + 1) * LANES]


def _store_slabs(ref, first_slab, val, plan):
    for s in range(val.shape[0] // plan.N2):
        start = pl.multiple_of((first_slab + s) * plan.P, SUBLANES)
        _store_rows(ref, pl.ds(start, plan.N2), val[s * plan.N2:(s + 1) * plan.N2])


def _load_slabs(ref, first_slab, count, plan):
    parts = [_load_rows(ref, pl.ds(pl.multiple_of((first_slab + s) * plan.P, SUBLANES), plan.N2))
             for s in range(count)]
    return parts[0] if count == 1 else jnp.concatenate(parts, axis=0)


def _dft_stage1(src_ref, f1_ref, a_ref, n_slabs, plan):
    def body(t2, _):
        rows = _load_rows(src_ref, pl.ds(t2, n_slabs, stride=plan.P)).astype(BF16)
        _store_rows(a_ref, pl.ds(t2, plan.R, stride=plan.P), _dot(f1_ref[t2], rows))
        return 0

    lax.fori_loop(0, plan.N2, body, 0, unroll=STAGE1_UNROLL)


@functools.lru_cache(maxsize=None)
def _filter_features(L):
    t = np.linspace(0.0, 1.0, L)[:, None]
    bands = (FILTER_EMB - 1) // 2
    w = 2.0 * np.pi * np.arange(L)[:, None] / L
    f = np.linspace(1e-4, bands - 1, bands)[None, :]
    z = np.concatenate([t, np.cos(f * w), -np.sin(f * w)], axis=-1)
    z_rev = np.concatenate([z[:1], z[:0:-1]], axis=0)
    zz = np.zeros((2 * L, LANES))
    zz[:L, :FILTER_EMB] = z
    zz[L:, :FILTER_EMB] = z_rev
    return jnp.asarray(zz, BF16)


def _decay_rates(width):
    max_decay = math.log(DECAY_TARGET) / FAST_DECAY_PCT
    min_decay = math.log(DECAY_TARGET) / SLOW_DECAY_PCT
    return jnp.asarray(np.abs(np.linspace(min_decay, max_decay, width))[None, :], F32)


def _filter_kernel(z_ref, w1_ref, b1_ref, fr1_ref, w2_ref, b2_ref, fr2_ref,
                   w3f_ref, w3b_ref, dl_ref, f1_ref, f2_ref, kf_ref,
                   h_ref, kern_ref, a_ref, *, plan, ch):
    L, N1, N2, K1 = plan.L, plan.N1, plan.N2, plan.K1
    n_ch = (2 * L) // ch

    @pl.when(pl.program_id(0) == 0)
    def _():
        def mlp(i, _):
            r0 = pl.multiple_of(i * ch, ch)
            h = jnp.sin(fr1_ref[...] * (_dot(z_ref[pl.ds(r0, ch), :], w1_ref[...]) + b1_ref[...]))
            h = jnp.sin(fr2_ref[...] * (_dot(h.astype(BF16), w2_ref[...]) + b2_ref[...]))
            h_ref[pl.ds(r0, ch), :] = h.astype(BF16)
            return 0
        lax.fori_loop(0, n_ch, mlp, 0)

    tc = kf_ref.shape[1]
    local = lax.broadcasted_iota(jnp.int32, (ch, tc), 0)

    def synth(i, asum):
        r0 = pl.multiple_of(i * ch, ch)
        row = local + r0
        h = h_ref[pl.ds(r0, ch), :]
        fwd = _dot(h, w3f_ref[...])
        bwd = _dot(h, w3b_ref[...])
        lag = jnp.where(row < L, row, 2 * L - row).astype(F32)
        win = jnp.exp(lag * (-1.0 / (L - 1)) * dl_ref[...])
        val = jnp.where(row < L, fwd + jnp.where(row == 0, bwd, 0.0), bwd) * win
        val = jnp.where(row == L, 0.0, val)
        _store_slabs(kern_ref, i * (ch // N2), val, plan)
        return asum + jnp.sum(jnp.abs(val), axis=0, keepdims=True)

    asum = lax.fori_loop(0, n_ch, synth, jnp.zeros((1, tc), F32))
    scale = 1.0 / (asum * float(plan.n))

    _dft_stage1(kern_ref, f1_ref, a_ref, N1, plan)

    def stage2(k1, _):
        r0 = pl.multiple_of(k1 * 2 * N2, 2 * N2)
        slab = _load_slabs(a_ref, 2 * k1, 2, plan).astype(BF16)
        kf_ref[pl.ds(r0, 2 * N2), :] = (_dot(f2_ref[...], slab) * scale).astype(kf_ref.dtype)
        return 0

    lax.fori_loop(0, K1, stage2, 0, unroll=STAGE2_UNROLL)
    pad = plan.R - 2 * K1
    if pad:
        kf_ref[pl.ds(2 * K1 * N2, pad * N2), :] = jnp.zeros((pad * N2, tc), kf_ref.dtype)


def _hyena_filter_spectrum(L, w1, b1, fr1, w2, b2, fr2, w3, tc):
    plan = _FftPlan(L)
    C = w3.shape[1] // 2
    zz = _filter_features(L)
    f1, _, f2, _ = _fft_consts(L)
    w1p = jnp.zeros((LANES, FILTER_ORDER), BF16).at[:FILTER_EMB].set(w1.astype(BF16))
    ch = min(512, L)
    rows_a = plan.R * plan.N2
    cblock = lambda off: pl.BlockSpec((FILTER_ORDER, tc), lambda c: (0, off + c))
    vmem = (2 * L * LANES * 2 * 2 + 2 * L * tc * 4 + rows_a * tc * 4 + 2 * rows_a * tc * 2
            + f1.size * 2 * 2 + 8 * ch * tc * 4)
    return pl.pallas_call(
        functools.partial(_filter_kernel, plan=plan, ch=ch),
        grid=(C // tc,),
        in_specs=[_const_spec(zz.shape),
                  _const_spec(w1p.shape), _const_spec((1, FILTER_ORDER)), _const_spec((1, FILTER_ORDER)),
                  _const_spec((FILTER_ORDER, FILTER_ORDER)), _const_spec((1, FILTER_ORDER)),
                  _const_spec((1, FILTER_ORDER)),
                  cblock(0), cblock(C // tc),
                  pl.BlockSpec((1, tc), lambda c: (0, c)),
                  _const_spec(f1.shape), _const_spec(f2.shape)],
        out_specs=pl.BlockSpec((rows_a, tc), lambda c: (0, c)),
        out_shape=jax.ShapeDtypeStruct((rows_a, C), BF16),
        scratch_shapes=[pltpu.VMEM((2 * L, FILTER_ORDER), BF16),
                        pltpu.VMEM((tc // LANES, plan.N1 * plan.P, LANES), F32),
                        pltpu.VMEM((tc // LANES, plan.R * plan.P, LANES), F32)],
        compiler_params=_params(("arbitrary",), vmem),
        name="hyena_filter",
    )(zz, w1p, b1, fr1, w2.astype(BF16), b2, fr2, w3.astype(BF16), w3.astype(BF16),
      _decay_rates(C), f1, f2)


def _short_conv_chunk(raw_ref, w_ref, b_ref, r0, ch, L):
    halo = BF16_SUBLANES
    c = raw_ref[0, pl.ds(r0, ch), :].astype(F32)
    p0 = pl.multiple_of(jnp.maximum(r0 - halo, 0), halo)
    n0 = pl.multiple_of(jnp.minimum(r0 + ch, L - halo), halo)
    prev = raw_ref[0, pl.ds(p0, halo), :].astype(F32)[halo - 1:halo]
    nxt = raw_ref[0, pl.ds(n0, halo), :].astype(F32)[0:1]
    prev = jnp.where(r0 == 0, 0.0, prev)
    nxt = jnp.where(r0 + ch == L, 0.0, nxt)
    rows = lax.broadcasted_iota(jnp.int32, c.shape, 0)
    up = jnp.where(rows == 0, prev, pltpu.roll(c, 1, 0))
    un = jnp.where(rows == ch - 1, nxt, pltpu.roll(c, ch - 1, 0))
    w = w_ref[...]
    return w[0:1] * up + w[1:2] * c + w[2:3] * un + b_ref[...]


def _hyena_kernel(x0_ref, x1_ref, vh_ref, w0_ref, w1_ref, wv_ref, b0_ref, b1_ref, bv_ref,
                  d_ref, kf_ref, f1_ref, g1_ref, f2_ref, f2i_ref, o_ref,
                  u_ref, a_ref, y_ref, *, plan, ch):
    L, N1, N2, K1, R = plan.L, plan.N1, plan.N2, plan.K1, plan.R
    n_ch = L // ch

    def gate_in(i, _):
        r0 = pl.multiple_of(i * ch, ch)
        x1 = _short_conv_chunk(x1_ref, w1_ref, b1_ref, r0, ch, L)
        vh = _short_conv_chunk(vh_ref, wv_ref, bv_ref, r0, ch, L)
        _store_slabs(u_ref, i * (ch // N2), x1 * vh, plan)
        return 0

    lax.fori_loop(0, n_ch, gate_in, 0)

    _dft_stage1(u_ref, f1_ref, a_ref, N1 // 2, plan)

    def spectral(k1, _):
        r0 = pl.multiple_of(k1 * 2 * N2, 2 * N2)
        x = _dot(f2_ref[...], _load_slabs(a_ref, 2 * k1, 2, plan).astype(BF16))
        xr, xi = x[:N2], x[N2:]
        kr = kf_ref[pl.ds(r0, N2), :].astype(F32)
        ki = kf_ref[pl.ds(r0 + N2, N2), :].astype(F32)
        y = jnp.concatenate([xr * kr - xi * ki, xr * ki + xi * kr], axis=0).astype(BF16)
        _store_slabs(a_ref, 2 * k1, _dot(f2i_ref[...], y), plan)
        return 0

    lax.fori_loop(0, K1, spectral, 0, unroll=STAGE2_UNROLL)

    def inverse1(t2, _):
        rows = _load_rows(a_ref, pl.ds(t2, R, stride=plan.P)).astype(BF16)
        _store_rows(y_ref, pl.ds(t2, N1 // 2, stride=plan.P), _dot(g1_ref[t2], rows))
        return 0

    lax.fori_loop(0, N2, inverse1, 0, unroll=STAGE1_UNROLL)

    def gate_out(i, _):
        r0 = pl.multiple_of(i * ch, ch)
        x0 = _short_conv_chunk(x0_ref, w0_ref, b0_ref, r0, ch, L)
        u = _load_slabs(u_ref, i * (ch // N2), ch // N2, plan)
        y = _load_slabs(y_ref, i * (ch // N2), ch // N2, plan)
        o_ref[0, pl.ds(r0, ch), :] = ((y + d_ref[...] * u) * x0).astype(o_ref.dtype)
        return 0

    lax.fori_loop(0, n_ch, gate_out, 0)


def _hyena_vmem_bytes(plan, tc, ch):
    L, rows_a = plan.L, plan.R * plan.N2
    pad_l = lambda m: _round_up(m, LANES)
    return (6 * L * tc * 2 + rows_a * tc * 2 + 2 * L * tc * 4 + rows_a * tc * 4 + 2 * L * tc * 2
            + plan.N2 * plan.R * pad_l(plan.N1 // 2) * 2 + plan.N2 * (plan.N1 // 2) * pad_l(plan.R) * 2
            + 16 * ch * tc * 4 + 16 * plan.N2 * tc * 4)


def _hyena_operator(hy, conv_w, conv_b, d, kf):
    B, L, W3 = hy.shape
    C = W3 // 3
    plan = _FftPlan(L)
    ch = min(512, L)
    tc = 2 * LANES if int(_hyena_vmem_bytes(plan, 2 * LANES, ch) * 1.25) + (4 << 20) <= VMEM_LIMIT_CAP else LANES
    nb = C // tc
    f1full, g1, f2, f2i = _fft_consts(L)
    f1 = f1full[:, :, :plan.N1 // 2]
    rows_a = plan.R * plan.N2
    seq = lambda part: pl.BlockSpec((1, L, tc), lambda c, b: (b, 0, part * nb + c))
    wspec = lambda part: pl.BlockSpec((SHORT_CONV, tc), lambda c, b: (0, part * nb + c))
    bspec = lambda part: pl.BlockSpec((1, tc), lambda c, b: (0, part * nb + c))
    vmem = _hyena_vmem_bytes(plan, tc, ch)
    return pl.pallas_call(
        functools.partial(_hyena_kernel, plan=plan, ch=ch),
        grid=(nb, B),
        in_specs=[seq(0), seq(1), seq(2), wspec(0), wspec(1), wspec(2), bspec(0), bspec(1), bspec(2),
                  pl.BlockSpec((1, tc), lambda c, b: (0, c)),
                  pl.BlockSpec((rows_a, tc), lambda c, b: (0, c), pipeline_mode=pl.Buffered(1)),
                  _const_spec(f1.shape), _const_spec(g1.shape), _const_spec(f2.shape), _const_spec(f2i.shape)],
        out_specs=pl.BlockSpec((1, L, tc), lambda c, b: (b, 0, c)),
        out_shape=jax.ShapeDtypeStruct((B, L, C), BF16),
        scratch_shapes=[pltpu.VMEM((tc // LANES, (plan.N1 // 2) * plan.P, LANES), F32),
                        pltpu.VMEM((tc // LANES, plan.R * plan.P, LANES), F32),
                        pltpu.VMEM((tc // LANES, (plan.N1 // 2) * plan.P, LANES), F32)],
        compiler_params=_params(("parallel", "parallel"), vmem),
        name="hyena_op",
    )(hy, hy, hy, conv_w, conv_w, conv_w, conv_b, conv_b, conv_b, d, kf, f1, g1, f2, f2i)


def _memkv_kernel(m_ref, g_ref, wk_ref, wv_ref, k_ref, v_ref):
    mn = _rms(m_ref[...], g_ref[...]).astype(BF16)
    k_ref[...] = _dot(mn, wk_ref[...]).astype(BF16)
    v_ref[...] = _dot(mn, wv_ref[...]).astype(BF16)


def _memkv(mem, g, wk, wv, tm):
    T, D = mem.shape
    row = pl.BlockSpec((tm, D), lambda i: (i, 0))
    vmem = 2 * tm * D * 4 + 2 * D * D * 2 + 4 * tm * D * 2 + 3 * tm * D * 4
    return pl.pallas_call(
        _memkv_kernel,
        grid=(T // tm,),
        in_specs=[row, _const_spec((1, D)), _const_spec((D, D)), _const_spec((D, D))],
        out_specs=[row, row],
        out_shape=[jax.ShapeDtypeStruct((T, D), BF16)] * 2,
        compiler_params=_params(("parallel",), vmem),
        name="mem_kv",
    )(mem, g, wk, wv)


def _mixout_cross_kernel(a_ref, y_ref, x_ref, wa_ref, wy_ref, gmix_ref, k_ref, v_ref,
                         gpre_ref, wq_ref, wo_ref, gpost_ref, o_ref, *, q_scale):
    z = _dot(a_ref[...], wa_ref[...]) + _dot(y_ref[...], wy_ref[...])
    x = x_ref[...] + _rms(z, gmix_ref[...])
    xn = _rms(x, gpre_ref[...]).astype(BF16)
    q = (_dot(xn, wq_ref[...]) * q_scale).astype(BF16)
    hd = q.shape[1] // X_HEADS
    outs = []
    for h in range(X_HEADS):
        sl = slice(h * hd, (h + 1) * hd)
        s = _dot_nt(q[:, sl], k_ref[0, :, sl])
        p = jnp.exp2(s - jnp.max(s, axis=-1, keepdims=True))
        l = jnp.sum(p, axis=-1, keepdims=True)
        outs.append((_dot(p.astype(BF16), v_ref[0, :, sl]) / l).astype(BF16))
    z = _dot(jnp.concatenate(outs, axis=1), wo_ref[...])
    o_ref[...] = x + _rms(z, gpost_ref[...])


def _mixout_cross_block(attn, y, x, w_out, gmix, k, v, gpre, wq, wo, gpost, L, tm):
    T, D = x.shape
    a, c = attn.shape[1], y.shape[1]
    n_mem = k.shape[1]
    blocks_per_seq = L // tm
    row = lambda width: pl.BlockSpec((tm, width), lambda i: (i, 0))
    kv = pl.BlockSpec((1, n_mem, D), lambda i: (i // blocks_per_seq, 0, 0))
    vmem = (2 * tm * (a + c) * 2 + 4 * tm * D * 4 + 4 * n_mem * D * 2 + (a + c + 2 * D) * D * 2
            + 8 * tm * D * 4)
    q_scale = (D // X_HEADS) ** -0.5 * LOG2E
    return pl.pallas_call(
        functools.partial(_mixout_cross_kernel, q_scale=q_scale),
        grid=(T // tm,),
        in_specs=[row(a), row(c), row(D), _const_spec((a, D)), _const_spec((c, D)), _const_spec((1, D)),
                  kv, kv, _const_spec((1, D)), _const_spec((D, D)), _const_spec((D, D)), _const_spec((1, D))],
        out_specs=row(D),
        out_shape=jax.ShapeDtypeStruct((T, D), F32),
        compiler_params=_params(("parallel",), vmem),
        name="mixout_cross",
    )(attn, y, x, w_out[:a], w_out[a:], gmix, k, v, gpre, wq, wo, gpost)


def _swiglu_kernel(x_ref, gpre_ref, wg_ref, wu_ref, wd_ref, gpost_ref, o_ref, *, n_split):
    x = x_ref[...]
    xn = _rms(x, gpre_ref[...]).astype(BF16)
    ff = wg_ref.shape[1]
    cw = ff // n_split
    z = None
    for c in range(n_split):
        sl = slice(c * cw, (c + 1) * cw)
        gate = _dot(xn, wg_ref[:, sl])
        up = _dot(xn, wu_ref[:, sl])
        h = (gate * (1.0 / (1.0 + jnp.exp(-gate))) * up).astype(BF16)
        part = _dot(h, wd_ref[sl, :])
        z = part if z is None else z + part
    o_ref[...] = x + _rms(z, gpost_ref[...])


def _swiglu_block(x, gpre, wg, wu, wd, gpost, tm):
    T, D = x.shape
    ff = wg.shape[1]
    n_split = 2 if (ff // 2) % LANES == 0 else 1
    row = pl.BlockSpec((tm, D), lambda i: (i, 0))
    vmem = 4 * tm * D * 4 + 3 * D * ff * 2 + 4 * tm * (ff // n_split) * 4 + 4 * tm * D * 4
    return pl.pallas_call(
        functools.partial(_swiglu_kernel, n_split=n_split),
        grid=(T // tm,),
        in_specs=[row, _const_spec((1, D)), _const_spec((D, ff)), _const_spec((D, ff)),
                  _const_spec((ff, D)), _const_spec((1, D))],
        out_specs=row,
        out_shape=jax.ShapeDtypeStruct((T, D), F32),
        compiler_params=_params(("parallel",), vmem),
        name="swiglu",
    )(x, gpre, wg, wu, wd, gpost)


def _trunk(x, mem, P):
    B, L, D = x.shape
    depth = P['w_in'].shape[0]
    n_mem = mem.shape[1]
    tm = min(512, L)
    tq, tk = tm, min(1024, L)
    tc = LANES
    a = DA_HEADS * 2 * DA_HEAD_DIM
    tables = _rope_tables(L)
    xf = x.reshape(B * L, D)
    memf = mem.reshape(B * n_mem, D)
    row = lambda v: v.reshape(1, -1)
    for l in range(depth):
        lam_init = 0.8 - 0.6 * math.exp(-0.3 * l)
        qt, k, vt, hy = _inproj(xf, row(P['ln_mix_pre'][l]), P['w_in'][l].astype(BF16), tables, B, L, tm, tk)
        lam_params = jnp.stack([P['lambda_q1'][l], P['lambda_k1'][l], P['lambda_q2'][l], P['lambda_k2'][l]])
        attn = _diff_attention(lam_params, qt, k.reshape(B, L, a), vt, P['subln_g'][l], lam_init, tq, tk)
        kf = _hyena_filter_spectrum(L, P['filt_w1'][l], row(P['filt_b1'][l]), row(P['filt_freq1'][l]),
                                    P['filt_w2'][l], row(P['filt_b2'][l]), row(P['filt_freq2'][l]),
                                    P['filt_w3'][l], tc)
        y = _hyena_operator(hy.reshape(B, L, -1), P['conv_w'][l], row(P['conv_b'][l]),
                            row(P['hyena_d'][l]), kf)
        km, vm = _memkv(memf, row(P['ln_mem'][l]), P['wk_x'][l].astype(BF16), P['wv_x'][l].astype(BF16),
                        min(512, B * n_mem))
        xf = _mixout_cross_block(attn.reshape(B * L, a), y.reshape(B * L, -1), xf, P['w_out'][l].astype(BF16),
                                 row(P['ln_mix_post'][l]), km.reshape(B, n_mem, D), vm.reshape(B, n_mem, D),
                                 row(P['ln_x_pre'][l]), P['wq_x'][l].astype(BF16), P['wo_x'][l].astype(BF16),
                                 row(P['ln_x_post'][l]), L, tm)
        xf = _swiglu_block(xf, row(P['ln_ffn_pre'][l]), P['w_gate'][l].astype(BF16), P['w_up'][l].astype(BF16),
                           P['w_down'][l].astype(BF16), row(P['ln_ffn_post'][l]), tm)
    return xf.reshape(B, L, D)


def kernel(x_prompt, x_sample, mem_prompt, mem_sample, ln_mix_pre, ln_mix_post, w_in, lambda_q1, lambda_k1, lambda_q2, lambda_k2, subln_g, conv_w, conv_b, filt_w1, filt_b1, filt_freq1, filt_w2, filt_b2, filt_freq2, filt_w3, hyena_d, w_out, ln_x_pre, ln_x_post, ln_mem, wq_x, wk_x, wv_x, wo_x, ln_ffn_pre, ln_ffn_post, w_gate, w_up, w_down):
    P = dict(ln_mix_pre=ln_mix_pre, ln_mix_post=ln_mix_post, w_in=w_in,
             lambda_q1=lambda_q1, lambda_k1=lambda_k1, lambda_q2=lambda_q2, lambda_k2=lambda_k2,
             subln_g=subln_g, conv_w=conv_w, conv_b=conv_b,
             filt_w1=filt_w1, filt_b1=filt_b1, filt_freq1=filt_freq1,
             filt_w2=filt_w2, filt_b2=filt_b2, filt_freq2=filt_freq2, filt_w3=filt_w3,
             hyena_d=hyena_d, w_out=w_out,
             ln_x_pre=ln_x_pre, ln_x_post=ln_x_post, ln_mem=ln_mem,
             wq_x=wq_x, wk_x=wk_x, wv_x=wv_x, wo_x=wo_x,
             ln_ffn_pre=ln_ffn_pre, ln_ffn_post=ln_ffn_post,
             w_gate=w_gate, w_up=w_up, w_down=w_down)
    return (_trunk(x_prompt, mem_prompt, P), _trunk(x_sample, mem_sample, P))
```

```python
import functools
import math

import numpy as np
import jax
import jax.numpy as jnp
from jax import lax
from jax.experimental import pallas as pl
from jax.experimental.pallas import tpu as pltpu

F32 = jnp.float32
BF16 = jnp.bfloat16

DA_HEADS = 4
DA_HEAD_DIM = 64
DA_V_DIM = 128
ROPE_DIM = 16
ROPE_THETA = 500000.0
SHORT_CONV = 3
FILTER_EMB = 33
FILTER_ORDER = 64
FAST_DECAY_PCT = 0.3
SLOW_DECAY_PCT = 1.5
DECAY_TARGET = 1e-2
X_HEADS = 4
EPS = 1e-6
LOG2E = 1.4426950408889634

LANES = 128
SUBLANES = 8
BF16_SUBLANES = 16
VMEM_LIMIT_CAP = 58 * 2**20

STAGE1_UNROLL = 32
STAGE2_UNROLL = 16

SUM_GROUPS = 4


def _round_up(a, b):
    return -(-a // b) * b


def _params(semantics, vmem_bytes):
    limit = min(int(vmem_bytes * 1.25) + (4 << 20), VMEM_LIMIT_CAP)
    return pltpu.CompilerParams(dimension_semantics=semantics, vmem_limit_bytes=limit)


def _const_spec(shape):
    nd = len(shape)
    return pl.BlockSpec(shape, lambda *_: (0,) * nd, pipeline_mode=pl.Buffered(1))


def _rms(x, g):
    ms = jnp.mean(x * x, axis=-1, keepdims=True)
    return x * lax.rsqrt(ms + EPS) * g


def _dot(a, b):
    return jnp.dot(a, b, preferred_element_type=F32)


def _dot_nt(a, b):
    return lax.dot_general(a, b, (((1,), (1,)), ((), ())), preferred_element_type=F32)


def _rope_tables(L):
    inv = ROPE_THETA ** (-np.arange(0, ROPE_DIM, 2, dtype=np.float64) / ROPE_DIM)
    ang = np.arange(L, dtype=np.float64)[:, None] * inv[None, :]
    half = ROPE_DIM // 2
    cos = np.ones((L, LANES)); sa = np.zeros((L, LANES)); sb = np.zeros((L, LANES))
    for g in range(LANES // DA_HEAD_DIM):
        o = g * DA_HEAD_DIM
        cos[:, o:o + half] = np.cos(ang)
        cos[:, o + half:o + ROPE_DIM] = np.cos(ang)
        sb[:, o:o + half] = -np.sin(ang)
        sa[:, o + half:o + ROPE_DIM] = np.sin(ang)
    return (jnp.asarray(cos, F32), jnp.asarray(sa, F32), jnp.asarray(sb, F32),
            jnp.asarray(np.cos(ang).T, F32), jnp.asarray(np.sin(ang).T, F32))


def _inproj_kernel(x_ref, g_ref, w_ref, wqt_ref, wvt_ref, cos_ref, sa_ref, sb_ref, cost_ref, sint_ref,
                   qt_ref, k_ref, vt_ref, hy_ref, *, q_scale):
    xn = _rms(x_ref[...], g_ref[...]).astype(BF16)
    cos, sa, sb = cos_ref[...], sa_ref[...], sb_ref[...]
    half = ROPE_DIM // 2

    def rope(y):
        parts = []
        for s in range(y.shape[1] // LANES):
            ys = y[:, s * LANES:(s + 1) * LANES]
            parts.append(ys * cos + pltpu.roll(ys, half, 1) * sa
                         + pltpu.roll(ys, LANES - half, 1) * sb)
        return jnp.concatenate(parts, axis=1)

    def rope_t(y):
        ct, st = cost_ref[...], sint_ref[...]
        parts = []
        for o in range(0, y.shape[0], DA_HEAD_DIM):
            x1, x2 = y[o:o + half], y[o + half:o + ROPE_DIM]
            parts += [x1 * ct - x2 * st, x2 * ct + x1 * st, y[o + ROPE_DIM:o + DA_HEAD_DIM]]
        return jnp.concatenate(parts, axis=0)

    a = k_ref.shape[1]
    qt_ref[0, 0] = (rope_t(_dot_nt(wqt_ref[...], xn)) * q_scale).astype(BF16)
    k_ref[...] = rope(_dot(xn, w_ref[:, a:2 * a])).astype(BF16)
    vt_ref[0, 0] = _dot_nt(wvt_ref[...], xn).astype(BF16)
    hy_ref[...] = _dot(xn, w_ref[:, 3 * a:]).astype(BF16)


def _inproj(x, g, w, tables, B, L, tm, tk):
    T, D = x.shape
    n_out = w.shape[1]
    a = DA_HEADS * 2 * DA_HEAD_DIM
    hyw = n_out - 3 * a
    blocks_per_seq = L // tm
    tiles_per_chunk = tk // tm
    tab_spec = pl.BlockSpec((tm, LANES), lambda i: (i % blocks_per_seq, 0))
    tabt_spec = pl.BlockSpec((ROPE_DIM // 2, tm), lambda i: (0, i % blocks_per_seq))
    row = lambda width: pl.BlockSpec((tm, width), lambda i: (i, 0))
    qt_spec = pl.BlockSpec((1, 1, a, tm), lambda i: (i // blocks_per_seq, i % blocks_per_seq, 0, 0))
    vt_spec = pl.BlockSpec((1, 1, a, tm), lambda i: (i // blocks_per_seq, (i % blocks_per_seq) // tiles_per_chunk,
                                                     0, i % tiles_per_chunk))
    vmem = 2 * tm * D * 4 + D * (n_out + 2 * a) * 2 + 8 * tm * LANES * 4 + 2 * tm * n_out * 2 + 4 * tm * 1536 * 4
    q_scale = DA_HEAD_DIM ** -0.5 * LOG2E
    wqt, wvt = w[:, 0:a].T, w[:, 2 * a:3 * a].T
    return pl.pallas_call(
        functools.partial(_inproj_kernel, q_scale=q_scale),
        grid=(T // tm,),
        in_specs=[row(D), _const_spec((1, D)), _const_spec((D, n_out)), _const_spec((a, D)), _const_spec((a, D)),
                  tab_spec, tab_spec, tab_spec, tabt_spec, tabt_spec],
        out_specs=[qt_spec, row(a), vt_spec, row(hyw)],
        out_shape=[jax.ShapeDtypeStruct((B, L // tm, a, tm), BF16), jax.ShapeDtypeStruct((T, a), BF16),
                   jax.ShapeDtypeStruct((B, L // tk, a, tk), BF16), jax.ShapeDtypeStruct((T, hyw), BF16)],
        compiler_params=_params(("parallel",), vmem),
        name="inproj",
    )(x, g, w, wqt, wvt, *tables)


def _attn_kernel(lam_ref, qt_ref, k_ref, vt_ref, g_ref, o_ref, s_ref, *, tk, kb, lam_init):
    qt = qt_ref[0, 0]
    tq = qt.shape[1]
    L = k_ref.shape[1]
    feat = lax.broadcasted_iota(jnp.int32, qt.shape, 0)
    zero = jnp.zeros_like(qt)
    qz = (jnp.where(feat < DA_HEAD_DIM, qt, zero), jnp.where(feat >= DA_HEAD_DIM, qt, zero))

    nk, nb = L // tk, tk // kb
    neg = jnp.full((1, tq), -1e30, F32)

    def score_block(j, c, b, mx):
        start = pl.multiple_of(j * tk + b * kb, kb)
        s = _dot(k_ref[0, pl.ds(start, kb), :], qz[c])
        s_ref[c, b * kb:(b + 1) * kb, :] = s.astype(BF16)
        return jnp.maximum(mx, jnp.max(s, axis=0, keepdims=True))

    def value_block(j, c, b, mn, lsum, pv):
        p = jnp.exp2(s_ref[c, b * kb:(b + 1) * kb, :] - mn.astype(BF16))
        g = kb // SUM_GROUPS
        part = p[0:g]
        for i in range(1, SUM_GROUPS):
            part = part + p[i * g:(i + 1) * g]
        lsum = lsum + jnp.sum(part.astype(F32), axis=0, keepdims=True)
        pv = pv + _dot(vt_ref[0, j, :, b * kb:(b + 1) * kb], p)
        return lsum, pv

    def step(score_of, value_of, mx_cur, state):
        m, l, acc = state
        mn = jnp.maximum(m, mx_cur).astype(BF16).astype(F32)
        alpha = jnp.exp2(m - mn)
        lsum, pv, mx = jnp.zeros((1, tq), F32), jnp.zeros((DA_V_DIM, tq), F32), neg
        for b in range(nb):
            if score_of is not None:
                mx = score_block(*score_of, b, mx)
            lsum, pv = value_block(*value_of, b, mn, lsum, pv)
        return mx, (mn, alpha * l + lsum, alpha * acc + pv)

    mx0 = neg
    for b in range(nb):
        mx0 = score_block(0, 0, b, mx0)
    fresh = (neg, jnp.zeros((1, tq), F32), jnp.zeros((DA_V_DIM, tq), F32))

    def chunk(j, carry):
        mx0, st0, st1 = carry
        mx1, st0 = step((j, 1), (j, 0), mx0, st0)
        mx0, st1 = step((j + 1, 0), (j, 1), mx1, st1)
        return mx0, st0, st1

    mx0, st0, st1 = lax.fori_loop(0, nk - 1, chunk, (mx0, fresh, fresh), unroll=True)
    mx1, st0 = step((nk - 1, 1), (nk - 1, 0), mx0, st0)
    _, st1 = step(None, (nk - 1, 1), mx1, st1)
    (_, l0, a0), (_, l1, a1) = st0, st1

    lp = lam_ref[...]
    lam = (jnp.exp(jnp.sum(lp[0:1] * lp[1:2], axis=-1, keepdims=True))
           - jnp.exp(jnp.sum(lp[2:3] * lp[3:4], axis=-1, keepdims=True)) + lam_init)
    o = a0 / l0 - lam * (a1 / l1)
    ms = jnp.mean(o * o, axis=0, keepdims=True)
    g = jnp.concatenate([g_ref[...]] * (tq // LANES), axis=1)
    o = o * lax.rsqrt(ms + EPS) * g * (1.0 - lam_init)
    o_ref[0] = o.T.astype(BF16)


def _diff_attention(lam_params, qt, k, vt, g, lam_init, tq, tk):
    B, L, A = k.shape
    qspec = pl.BlockSpec((1, 1, LANES, tq), lambda b, h, i: (b, i, h, 0))
    ospec = pl.BlockSpec((1, tq, LANES), lambda b, h, i: (b, i, h))
    kspec = pl.BlockSpec((1, L, LANES), lambda b, h, i: (b, 0, h))
    vspec = pl.BlockSpec((1, L // tk, LANES, tk), lambda b, h, i: (b, 0, h, 0))
    g_cols = jnp.broadcast_to(g.reshape(DA_V_DIM, 1), (DA_V_DIM, LANES))
    kb = min(512, tk)
    vmem = 4 * L * LANES * 2 + 4 * tq * LANES * 2 + 2 * tq * tk * 4 + 4 * tq * kb * 6 + 16 * tq * LANES * 4
    return pl.pallas_call(
        functools.partial(_attn_kernel, tk=tk, kb=kb, lam_init=lam_init),
        grid=(B, DA_HEADS, L // tq),
        in_specs=[_const_spec(lam_params.shape), qspec, kspec, vspec, _const_spec((DA_V_DIM, LANES))],
        out_specs=ospec,
        out_shape=jax.ShapeDtypeStruct((B, L, A), BF16),
        scratch_shapes=[pltpu.VMEM((2, tk, tq), BF16)],
        compiler_params=_params(("parallel", "parallel", "parallel"), vmem),
        name="diff_attn",
    )(lam_params, qt, k, vt, g_cols)


class _FftPlan:
    def __init__(self, L):
        n = 2 * L
        lg = int(math.log2(n))
        assert 2 ** lg == n
        self.L, self.n = L, n
        self.N1 = 2 ** (lg // 2)
        self.N2 = n // self.N1
        self.K1 = self.N1 // 2 + 1
        self.R = _round_up(2 * self.K1, BF16_SUBLANES)
        self.P = self.N2 + SUBLANES
        assert (self.P // SUBLANES) % 2 == 1
        assert self.N2 % BF16_SUBLANES == 0 and (self.N1 // 2) % BF16_SUBLANES == 0


@functools.lru_cache(maxsize=None)
def _fft_consts(L):
    p = _FftPlan(L)
    N1, N2, K1, R, n = p.N1, p.N2, p.K1, p.R, p.n
    t2 = np.arange(N2, dtype=np.float64)[:, None, None]
    k1 = np.arange(K1, dtype=np.float64)[None, :, None]
    t1 = np.arange(N1, dtype=np.float64)[None, None, :]
    ang = -2.0 * np.pi * (t1 * k1 / N1 + t2 * k1 / n)
    f1 = np.zeros((N2, R, N1))
    f1[:, 0:2 * K1:2, :] = np.cos(ang)
    f1[:, 1:2 * K1:2, :] = np.sin(ang)
    c = np.where((np.arange(K1) == 0) | (np.arange(K1) == N1 // 2), 1.0, 2.0)[None, :, None]
    th = -ang[:, :, :N1 // 2]
    g1 = np.zeros((N2, N1 // 2, R))
    g1[:, :, 0:2 * K1:2] = np.transpose(c * np.cos(th), (0, 2, 1))
    g1[:, :, 1:2 * K1:2] = np.transpose(-c * np.sin(th), (0, 2, 1))
    a2 = 2.0 * np.pi * np.outer(np.arange(N2), np.arange(N2)) / N2
    C, S = np.cos(a2), np.sin(a2)
    f2 = np.block([[C, S], [-S, C]])
    return (jnp.asarray(f1, BF16), jnp.asarray(g1, BF16), jnp.asarray(f2, BF16), jnp.asarray(f2.T, BF16))


def _load_rows(ref, idx):
    parts = [ref[g, idx, :] for g in range(ref.shape[0])]
    return parts[0] if len(parts) == 1 else jnp.concatenate(parts, axis=1)


def _store_rows(ref, idx, val):
    for g in range(ref.shape[0]):
        ref[g, idx, :] = val[:, g * LANES:(g + 1) * LANES]


def _store_slabs(ref, first_slab, val, plan):
    for s in range(val.shape[0] // plan.N2):
        start = pl.multiple_of((first_slab + s) * plan.P, SUBLANES)
        _store_rows(ref, pl.ds(start, plan.N2), val[s * plan.N2:(s + 1) * plan.N2])


def _load_slabs(ref, first_slab, count, plan):
    parts = [_load_rows(ref, pl.ds(pl.multiple_of((first_slab + s) * plan.P, SUBLANES), plan.N2))
             for s in range(count)]
    return parts[0] if count == 1 else jnp.concatenate(parts, axis=0)


def _dft_stage1(src_ref, f1_ref, a_ref, n_slabs, plan):
    def body(t2, _):
        rows = _load_rows(src_ref, pl.ds(t2, n_slabs, stride=plan.P)).astype(BF16)
        _store_rows(a_ref, pl.ds(t2, plan.R, stride=plan.P), _dot(f1_ref[t2], rows))
        return 0

    lax.fori_loop(0, plan.N2, body, 0, unroll=STAGE1_UNROLL)


@functools.lru_cache(maxsize=None)
def _filter_features(L):
    t = np.linspace(0.0, 1.0, L)[:, None]
    bands = (FILTER_EMB - 1) // 2
    w = 2.0 * np.pi * np.arange(L)[:, None] / L
    f = np.linspace(1e-4, bands - 1, bands)[None, :]
    z = np.concatenate([t, np.cos(f * w), -np.sin(f * w)], axis=-1)
    z_rev = np.concatenate([z[:1], z[:0:-1]], axis=0)
    zz = np.zeros((2 * L, LANES))
    zz[:L, :FILTER_EMB] = z
    zz[L:, :FILTER_EMB] = z_rev
    return jnp.asarray(zz, BF16)


def _decay_rates(width):
    max_decay = math.log(DECAY_TARGET) / FAST_DECAY_PCT
    min_decay = math.log(DECAY_TARGET) / SLOW_DECAY_PCT
    return jnp.asarray(np.abs(np.linspace(min_decay, max_decay, width))[None, :], F32)


def _filter_kernel(z_ref, w1_ref, b1_ref, fr1_ref, w2_ref, b2_ref, fr2_ref,
                   w3f_ref, w3b_ref, dl_ref, f1_ref, f2_ref, kf_ref,
                   h_ref, kern_ref, a_ref, *, plan, ch):
    L, N1, N2, K1 = plan.L, plan.N1, plan.N2, plan.K1
    n_ch = (2 * L) // ch

    @pl.when(pl.program_id(0) == 0)
    def _():
        def mlp(i, _):
            r0 = pl.multiple_of(i * ch, ch)
            h = jnp.sin(fr1_ref[...] * (_dot(z_ref[pl.ds(r0, ch), :], w1_ref[...]) + b1_ref[...]))
            h = jnp.sin(fr2_ref[...] * (_dot(h.astype(BF16), w2_ref[...]) + b2_ref[...]))
            h_ref[pl.ds(r0, ch), :] = h.astype(BF16)
            return 0
        lax.fori_loop(0, n_ch, mlp, 0)

    tc = kf_ref.shape[1]
    local = lax.broadcasted_iota(jnp.int32, (ch, tc), 0)

    def synth(i, asum):
        r0 = pl.multiple_of(i * ch, ch)
        row = local + r0
        h = h_ref[pl.ds(r0, ch), :]
        fwd = _dot(h, w3f_ref[...])
        bwd = _dot(h, w3b_ref[...])
        lag = jnp.where(row < L, row, 2 * L - row).astype(F32)
        win = jnp.exp(lag * (-1.0 / (L - 1)) * dl_ref[...])
        val = jnp.where(row < L, fwd + jnp.where(row == 0, bwd, 0.0), bwd) * win
        val = jnp.where(row == L, 0.0, val)
        _store_slabs(kern_ref, i * (ch // N2), val, plan)
        return asum + jnp.sum(jnp.abs(val), axis=0, keepdims=True)

    asum = lax.fori_loop(0, n_ch, synth, jnp.zeros((1, tc), F32))
    scale = 1.0 / (asum * float(plan.n))

    _dft_stage1(kern_ref, f1_ref, a_ref, N1, plan)

    def stage2(k1, _):
        r0 = pl.multiple_of(k1 * 2 * N2, 2 * N2)
        slab = _load_slabs(a_ref, 2 * k1, 2, plan).astype(BF16)
        kf_ref[pl.ds(r0, 2 * N2), :] = (_dot(f2_ref[...], slab) * scale).astype(kf_ref.dtype)
        return 0

    lax.fori_loop(0, K1, stage2, 0, unroll=STAGE2_UNROLL)
    pad = plan.R - 2 * K1
    if pad:
        kf_ref[pl.ds(2 * K1 * N2, pad * N2), :] = jnp.zeros((pad * N2, tc), kf_ref.dtype)


def _hyena_filter_spectrum(L, w1, b1, fr1, w2, b2, fr2, w3, tc):
    plan = _FftPlan(L)
    C = w3.shape[1] // 2
    zz = _filter_features(L)
    f1, _, f2, _ = _fft_consts(L)
    w1p = jnp.zeros((LANES, FILTER_ORDER), BF16).at[:FILTER_EMB].set(w1.astype(BF16))
    ch = min(512, L)
    rows_a = plan.R * plan.N2
    cblock = lambda off: pl.BlockSpec((FILTER_ORDER, tc), lambda c: (0, off + c))
    vmem = (2 * L * LANES * 2 * 2 + 2 * L * tc * 4 + rows_a * tc * 4 + 2 * rows_a * tc * 2
            + f1.size * 2 * 2 + 8 * ch * tc * 4)
    return pl.pallas_call(
        functools.partial(_filter_kernel, plan=plan, ch=ch),
        grid=(C // tc,),
        in_specs=[_const_spec(zz.shape),
                  _const_spec(w1p.shape), _const_spec((1, FILTER_ORDER)), _const_spec((1, FILTER_ORDER)),
                  _const_spec((FILTER_ORDER, FILTER_ORDER)), _const_spec((1, FILTER_ORDER)),
                  _const_spec((1, FILTER_ORDER)),
                  cblock(0), cblock(C // tc),
                  pl.BlockSpec((1, tc), lambda c: (0, c)),
                  _const_spec(f1.shape), _const_spec(f2.shape)],
        out_specs=pl.BlockSpec((rows_a, tc), lambda c: (0, c)),
        out_shape=jax.ShapeDtypeStruct((rows_a, C), BF16),
        scratch_shapes=[pltpu.VMEM((2 * L, FILTER_ORDER), BF16),
                        pltpu.VMEM((tc // LANES, plan.N1 * plan.P, LANES), F32),
                        pltpu.VMEM((tc // LANES, plan.R * plan.P, LANES), F32)],
        compiler_params=_params(("arbitrary",), vmem),
        name="hyena_filter",
    )(zz, w1p, b1, fr1, w2.astype(BF16), b2, fr2, w3.astype(BF16), w3.astype(BF16),
      _decay_rates(C), f1, f2)


def _short_conv_chunk(raw_ref, w_ref, b_ref, r0, ch, L):
    halo = BF16_SUBLANES
    c = raw_ref[0, pl.ds(r0, ch), :].astype(F32)
    p0 = pl.multiple_of(jnp.maximum(r0 - halo, 0), halo)
    n0 = pl.multiple_of(jnp.minimum(r0 + ch, L - halo), halo)
    prev = raw_ref[0, pl.ds(p0, halo), :].astype(F32)[halo - 1:halo]
    nxt = raw_ref[0, pl.ds(n0, halo), :].astype(F32)[0:1]
    prev = jnp.where(r0 == 0, 0.0, prev)
    nxt = jnp.where(r0 + ch == L, 0.0, nxt)
    rows = lax.broadcasted_iota(jnp.int32, c.shape, 0)
    up = jnp.where(rows == 0, prev, pltpu.roll(c, 1, 0))
    un = jnp.where(rows == ch - 1, nxt, pltpu.roll(c, ch - 1, 0))
    w = w_ref[...]
    return w[0:1] * up + w[1:2] * c + w[2:3] * un + b_ref[...]


def _hyena_kernel(x0_ref, x1_ref, vh_ref, w0_ref, w1_ref, wv_ref, b0_ref, b1_ref, bv_ref,
                  d_ref, kf_ref, f1_ref, g1_ref, f2_ref, f2i_ref, o_ref,
                  u_ref, a_ref, y_ref, *, plan, ch):
    L, N1, N2, K1, R = plan.L, plan.N1, plan.N2, plan.K1, plan.R
    n_ch = L // ch

    def gate_in(i, _):
        r0 = pl.multiple_of(i * ch, ch)
        x1 = _short_conv_chunk(x1_ref, w1_ref, b1_ref, r0, ch, L)
        vh = _short_conv_chunk(vh_ref, wv_ref, bv_ref, r0, ch, L)
        _store_slabs(u_ref, i * (ch // N2), x1 * vh, plan)
        return 0

    lax.fori_loop(0, n_ch, gate_in, 0)

    _dft_stage1(u_ref, f1_ref, a_ref, N1 // 2, plan)

    def spectral(k1, _):
        r0 = pl.multiple_of(k1 * 2 * N2, 2 * N2)
        x = _dot(f2_ref[...], _load_slabs(a_ref, 2 * k1, 2, plan).astype(BF16))
        xr, xi = x[:N2], x[N2:]
        kr = kf_ref[pl.ds(r0, N2), :].astype(F32)
        ki = kf_ref[pl.ds(r0 + N2, N2), :].astype(F32)
        y = jnp.concatenate([xr * kr - xi * ki, xr * ki + xi * kr], axis=0).astype(BF16)
        _store_slabs(a_ref, 2 * k1, _dot(f2i_ref[...], y), plan)
        return 0

    lax.fori_loop(0, K1, spectral, 0, unroll=STAGE2_UNROLL)

    def inverse1(t2, _):
        rows = _load_rows(a_ref, pl.ds(t2, R, stride=plan.P)).astype(BF16)
        _store_rows(y_ref, pl.ds(t2, N1 // 2, stride=plan.P), _dot(g1_ref[t2], rows))
        return 0

    lax.fori_loop(0, N2, inverse1, 0, unroll=STAGE1_UNROLL)

    def gate_out(i, _):
        r0 = pl.multiple_of(i * ch, ch)
        x0 = _short_conv_chunk(x0_ref, w0_ref, b0_ref, r0, ch, L)
        u = _load_slabs(u_ref, i * (ch // N2), ch // N2, plan)
        y = _load_slabs(y_ref, i * (ch // N2), ch // N2, plan)
        o_ref[0, pl.ds(r0, ch), :] = ((y + d_ref[...] * u) * x0).astype(o_ref.dtype)
        return 0

    lax.fori_loop(0, n_ch, gate_out, 0)


def _hyena_vmem_bytes(plan, tc, ch):
    L, rows_a = plan.L, plan.R * plan.N2
    pad_l = lambda m: _round_up(m, LANES)
    return (6 * L * tc * 2 + rows_a * tc * 2 + 2 * L * tc * 4 + rows_a * tc * 4 + 2 * L * tc * 2
            + plan.N2 * plan.R * pad_l(plan.N1 // 2) * 2 + plan.N2 * (plan.N1 // 2) * pad_l(plan.R) * 2
            + 16 * ch * tc * 4 + 16 * plan.N2 * tc * 4)


def _hyena_operator(hy, conv_w, conv_b, d, kf):
    B, L, W3 = hy.shape
    C = W3 // 3
    plan = _FftPlan(L)
    ch = min(512, L)
    tc = 2 * LANES if int(_hyena_vmem_bytes(plan, 2 * LANES, ch) * 1.25) + (4 << 20) <= VMEM_LIMIT_CAP else LANES
    nb = C // tc
    f1full, g1, f2, f2i = _fft_consts(L)
    f1 = f1full[:, :, :plan.N1 // 2]
    rows_a = plan.R * plan.N2
    seq = lambda part: pl.BlockSpec((1, L, tc), lambda c, b: (b, 0, part * nb + c))
    wspec = lambda part: pl.BlockSpec((SHORT_CONV, tc), lambda c, b: (0, part * nb + c))
    bspec = lambda part: pl.BlockSpec((1, tc), lambda c, b: (0, part * nb + c))
    vmem = _hyena_vmem_bytes(plan, tc, ch)
    return pl.pallas_call(
        functools.partial(_hyena_kernel, plan=plan, ch=ch),
        grid=(nb, B),
        in_specs=[seq(0), seq(1), seq(2), wspec(0), wspec(1), wspec(2), bspec(0), bspec(1), bspec(2),
                  pl.BlockSpec((1, tc), lambda c, b: (0, c)),
                  pl.BlockSpec((rows_a, tc), lambda c, b: (0, c), pipeline_mode=pl.Buffered(1)),
                  _const_spec(f1.shape), _const_spec(g1.shape), _const_spec(f2.shape), _const_spec(f2i.shape)],
        out_specs=pl.BlockSpec((1, L, tc), lambda c, b: (b, 0, c)),
        out_shape=jax.ShapeDtypeStruct((B, L, C), BF16),
        scratch_shapes=[pltpu.VMEM((tc // LANES, (plan.N1 // 2) * plan.P, LANES), F32),
                        pltpu.VMEM((tc // LANES, plan.R * plan.P, LANES), F32),
                        pltpu.VMEM((tc // LANES, (plan.N1 // 2) * plan.P, LANES), F32)],
        compiler_params=_params(("parallel", "parallel"), vmem),
        name="hyena_op",
    )(hy, hy, hy, conv_w, conv_w, conv_w, conv_b, conv_b, conv_b, d, kf, f1, g1, f2, f2i)


def _memkv_kernel(m_ref, g_ref, wk_ref, wv_ref, k_ref, v_ref):
    mn = _rms(m_ref[...], g_ref[...]).astype(BF16)
    k_ref[...] = _dot(mn, wk_ref[...]).astype(BF16)
    v_ref[...] = _dot(mn, wv_ref[...]).astype(BF16)


def _memkv(mem, g, wk, wv, tm):
    T, D = mem.shape
    row = pl.BlockSpec((tm, D), lambda i: (i, 0))
    vmem = 2 * tm * D * 4 + 2 * D * D * 2 + 4 * tm * D * 2 + 3 * tm * D * 4
    return pl.pallas_call(
        _memkv_kernel,
        grid=(T // tm,),
        in_specs=[row, _const_spec((1, D)), _const_spec((D, D)), _const_spec((D, D))],
        out_specs=[row, row],
        out_shape=[jax.ShapeDtypeStruct((T, D), BF16)] * 2,
        compiler_params=_params(("parallel",), vmem),
        name="mem_kv",
    )(mem, g, wk, wv)


def _mixout_cross_kernel(a_ref, y_ref, x_ref, wa_ref, wy_ref, gmix_ref, k_ref, v_ref,
                         gpre_ref, wq_ref, wo_ref, gpost_ref, o_ref, *, q_scale):
    z = _dot(a_ref[...], wa_ref[...]) + _dot(y_ref[...], wy_ref[...])
    x = x_ref[...] + _rms(z, gmix_ref[...])
    xn = _rms(x, gpre_ref[...]).astype(BF16)
    q = (_dot(xn, wq_ref[...]) * q_scale).astype(BF16)
    hd = q.shape[1] // X_HEADS
    outs = []
    for h in range(X_HEADS):
        sl = slice(h * hd, (h + 1) * hd)
        s = _dot_nt(q[:, sl], k_ref[0, :, sl])
        p = jnp.exp2(s - jnp.max(s, axis=-1, keepdims=True))
        l = jnp.sum(p, axis=-1, keepdims=True)
        outs.append((_dot(p.astype(BF16), v_ref[0, :, sl]) / l).astype(BF16))
    z = _dot(jnp.concatenate(outs, axis=1), wo_ref[...])
    o_ref[...] = x + _rms(z, gpost_ref[...])


def _mixout_cross_block(attn, y, x, w_out, gmix, k, v, gpre, wq, wo, gpost, L, tm):
    T, D = x.shape
    a, c = attn.shape[1], y.shape[1]
    n_mem = k.shape[1]
    blocks_per_seq = L // tm
    row = lambda width: pl.BlockSpec((tm, width), lambda i: (i, 0))
    kv = pl.BlockSpec((1, n_mem, D), lambda i: (i // blocks_per_seq, 0, 0))
    vmem = (2 * tm * (a + c) * 2 + 4 * tm * D * 4 + 4 * n_mem * D * 2 + (a + c + 2 * D) * D * 2
            + 8 * tm * D * 4)
    q_scale = (D // X_HEADS) ** -0.5 * LOG2E
    return pl.pallas_call(
        functools.partial(_mixout_cross_kernel, q_scale=q_scale),
        grid=(T // tm,),
        in_specs=[row(a), row(c), row(D), _const_spec((a, D)), _const_spec((c, D)), _const_spec((1, D)),
                  kv, kv, _const_spec((1, D)), _const_spec((D, D)), _const_spec((D, D)), _const_spec((1, D))],
        out_specs=row(D),
        out_shape=jax.ShapeDtypeStruct((T, D), F32),
        compiler_params=_params(("parallel",), vmem),
        name="mixout_cross",
    )(attn, y, x, w_out[:a], w_out[a:], gmix, k, v, gpre, wq, wo, gpost)


def _swiglu_kernel(x_ref, gpre_ref, wg_ref, wu_ref, wd_ref, gpost_ref, o_ref, *, n_split):
    x = x_ref[...]
    xn = _rms(x, gpre_ref[...]).astype(BF16)
    ff = wg_ref.shape[1]
    cw = ff // n_split
    z = None
    for c in range(n_split):
        sl = slice(c * cw, (c + 1) * cw)
        gate = _dot(xn, wg_ref[:, sl])
        up = _dot(xn, wu_ref[:, sl])
        h = (gate * (1.0 / (1.0 + jnp.exp(-gate))) * up).astype(BF16)
        part = _dot(h, wd_ref[sl, :])
        z = part if z is None else z + part
    o_ref[...] = x + _rms(z, gpost_ref[...])


def _swiglu_block(x, gpre, wg, wu, wd, gpost, tm):
    T, D = x.shape
    ff = wg.shape[1]
    n_split = 2 if (ff // 2) % LANES == 0 else 1
    row = pl.BlockSpec((tm, D), lambda i: (i, 0))
    vmem = 4 * tm * D * 4 + 3 * D * ff * 2 + 4 * tm * (ff // n_split) * 4 + 4 * tm * D * 4
    return pl.pallas_call(
        functools.partial(_swiglu_kernel, n_split=n_split),
        grid=(T // tm,),
        in_specs=[row, _const_spec((1, D)), _const_spec((D, ff)), _const_spec((D, ff)),
                  _const_spec((ff, D)), _const_spec((1, D))],
        out_specs=row,
        out_shape=jax.ShapeDtypeStruct((T, D), F32),
        compiler_params=_params(("parallel",), vmem),
        name="swiglu",
    )(x, gpre, wg, wu, wd, gpost)


def _trunk(x, mem, P):
    B, L, D = x.shape
    depth = P['w_in'].shape[0]
    n_mem = mem.shape[1]
    tm = min(512, L)
    tq, tk = tm, min(1024, L)
    tc = LANES
    a = DA_HEADS * 2 * DA_HEAD_DIM
    tables = _rope_tables(L)
    xf = x.reshape(B * L, D)
    memf = mem.reshape(B * n_mem, D)
    row = lambda v: v.reshape(1, -1)
    for l in range(depth):
        lam_init = 0.8 - 0.6 * math.exp(-0.3 * l)
        qt, k, vt, hy = _inproj(xf, row(P['ln_mix_pre'][l]), P['w_in'][l].astype(BF16), tables, B, L, tm, tk)
        lam_params = jnp.stack([P['lambda_q1'][l], P['lambda_k1'][l], P['lambda_q2'][l], P['lambda_k2'][l]])
        attn = _diff_attention(lam_params, qt, k.reshape(B, L, a), vt, P['subln_g'][l], lam_init, tq, tk)
        kf = _hyena_filter_spectrum(L, P['filt_w1'][l], row(P['filt_b1'][l]), row(P['filt_freq1'][l]),
                                    P['filt_w2'][l], row(P['filt_b2'][l]), row(P['filt_freq2'][l]),
                                    P['filt_w3'][l], tc)
        y = _hyena_operator(hy.reshape(B, L, -1), P['conv_w'][l], row(P['conv_b'][l]),
                            row(P['hyena_d'][l]), kf)
        km, vm = _memkv(memf, row(P['ln_mem'][l]), P['wk_x'][l].astype(BF16), P['wv_x'][l].astype(BF16),
                        min(512, B * n_mem))
        xf = _mixout_cross_block(attn.reshape(B * L, a), y.reshape(B * L, -1), xf, P['w_out'][l].astype(BF16),
                                 row(P['ln_mix_post'][l]), km.reshape(B, n_mem, D), vm.reshape(B, n_mem, D),
                                 row(P['ln_x_pre'][l]), P['wq_x'][l].astype(BF16), P['wo_x'][l].astype(BF16),
                                 row(P['ln_x_post'][l]), L, tm)
        xf = _swiglu_block(xf, row(P['ln_ffn_pre'][l]), P['w_gate'][l].astype(BF16), P['w_up'][l].astype(BF16),
                           P['w_down'][l].astype(BF16), row(P['ln_ffn_post'][l]), tm)
    return xf.reshape(B, L, D)


def kernel(x_prompt, x_sample, mem_prompt, mem_sample, ln_mix_pre, ln_mix_post, w_in, lambda_q1, lambda_k1, lambda_q2, lambda_k2, subln_g, conv_w, conv_b, filt_w1, filt_b1, filt_freq1, filt_w2, filt_b2, filt_freq2, filt_w3, hyena_d, w_out, ln_x_pre, ln_x_post, ln_mem, wq_x, wk_x, wv_x, wo_x, ln_ffn_pre, ln_ffn_post, w_gate, w_up, w_down):
    P = dict(ln_mix_pre=ln_mix_pre, ln_mix_post=ln_mix_post, w_in=w_in,
             lambda_q1=lambda_q1, lambda_k1=lambda_k1, lambda_q2=lambda_q2, lambda_k2=lambda_k2,
             subln_g=subln_g, conv_w=conv_w, conv_b=conv_b,
             filt_w1=filt_w1, filt_b1=filt_b1, filt_freq1=filt_freq1,
             filt_w2=filt_w2, filt_b2=filt_b2, filt_freq2=filt_freq2, filt_w3=filt_w3,
             hyena_d=hyena_d, w_out=w_out,
             ln_x_pre=ln_x_pre, ln_x_post=ln_x_post, ln_mem=ln_mem,
             wq_x=wq_x, wk_x=wk_x, wv_x=wv_x, wo_x=wo_x,
             ln_ffn_pre=ln_ffn_pre, ln_ffn_post=ln_ffn_post,
             w_gate=w_gate, w_up=w_up, w_down=w_down)
    return (_trunk(x_prompt, mem_prompt, P), _trunk(x_sample, mem_sample, P))
```

```python
import functools
import math

import numpy as np
import jax
import jax.numpy as jnp
from jax import lax
from jax.experimental import pallas as pl
from jax.experimental.pallas import tpu as pltpu

F32 = jnp.float32
BF16 = jnp.bfloat16

DA_HEADS = 4
DA_HEAD_DIM = 64
DA_V_DIM = 128
ROPE_DIM = 16
ROPE_THETA = 500000.0
SHORT_CONV = 3
FILTER_EMB = 33
FILTER_ORDER = 64
FAST_DECAY_PCT = 0.3
SLOW_DECAY_PCT = 1.5
DECAY_TARGET = 1e-2
X_HEADS = 4
EPS = 1e-6
LOG2E = 1.4426950408889634

LANES = 128
SUBLANES = 8
BF16_SUBLANES = 16
MXU_WIDTH = 256
VMEM_LIMIT_CAP = 58 * 2**20

STAGE1_UNROLL = 32
STAGE2_UNROLL = 16


def _round_up(a, b):
    return -(-a // b) * b


def _params(semantics, vmem_bytes):
    limit = min(int(vmem_bytes * 1.25) + (4 << 20), VMEM_LIMIT_CAP)
    return pltpu.CompilerParams(dimension_semantics=semantics, vmem_limit_bytes=limit)


def _const_spec(shape):
    nd = len(shape)
    return pl.BlockSpec(shape, lambda *_: (0,) * nd, pipeline_mode=pl.Buffered(1))


def _rms(x, g):
    ms = jnp.mean(x * x, axis=-1, keepdims=True)
    return x * lax.rsqrt(ms + EPS) * g


def _dot(a, b):
    return jnp.dot(a, b, preferred_element_type=F32)


def _dot_nt(a, b):
    return lax.dot_general(a, b, (((1,), (1,)), ((), ())), preferred_element_type=F32)


def _rope_tables(L):
    inv = ROPE_THETA ** (-np.arange(0, ROPE_DIM, 2, dtype=np.float64) / ROPE_DIM)
    ang = np.arange(L, dtype=np.float64)[:, None] * inv[None, :]
    half = ROPE_DIM // 2
    cos = np.ones((L, LANES)); sa = np.zeros((L, LANES)); sb = np.zeros((L, LANES))
    for g in range(LANES // DA_HEAD_DIM):
        o = g * DA_HEAD_DIM
        cos[:, o:o + half] = np.cos(ang)
        cos[:, o + half:o + ROPE_DIM] = np.cos(ang)
        sb[:, o:o + half] = -np.sin(ang)
        sa[:, o + half:o + ROPE_DIM] = np.sin(ang)
    return (jnp.asarray(cos, F32), jnp.asarray(sa, F32), jnp.asarray(sb, F32),
            jnp.asarray(np.cos(ang).T, F32), jnp.asarray(np.sin(ang).T, F32))


def _inproj_kernel(x_ref, g_ref, w_ref, wqt_ref, wvt_ref, cos_ref, sa_ref, sb_ref, cost_ref, sint_ref,
                   qt_ref, k_ref, vt_ref, hy_ref, *, q_scale):
    xn = _rms(x_ref[...], g_ref[...]).astype(BF16)
    cos, sa, sb = cos_ref[...], sa_ref[...], sb_ref[...]
    half = ROPE_DIM // 2

    def rope(y):
        parts = []
        for s in range(y.shape[1] // LANES):
            ys = y[:, s * LANES:(s + 1) * LANES]
            parts.append(ys * cos + pltpu.roll(ys, half, 1) * sa
                         + pltpu.roll(ys, LANES - half, 1) * sb)
        return jnp.concatenate(parts, axis=1)

    def rope_t(y):
        ct, st = cost_ref[...], sint_ref[...]
        parts = []
        for o in range(0, y.shape[0], DA_HEAD_DIM):
            x1, x2 = y[o:o + half], y[o + half:o + ROPE_DIM]
            parts += [x1 * ct - x2 * st, x2 * ct + x1 * st, y[o + ROPE_DIM:o + DA_HEAD_DIM]]
        return jnp.concatenate(parts, axis=0)

    a = k_ref.shape[1]
    qt_ref[0, 0] = (rope_t(_dot_nt(wqt_ref[...], xn)) * q_scale).astype(BF16)
    k_ref[...] = rope(_dot(xn, w_ref[:, a:2 * a])).astype(BF16)
    vt_ref[0, 0] = _dot_nt(wvt_ref[...], xn).astype(BF16)
    hy_ref[...] = _dot(xn, w_ref[:, 3 * a:]).astype(BF16)


def _inproj(x, g, w, tables, B, L, tm, tk):
    T, D = x.shape
    n_out = w.shape[1]
    a = DA_HEADS * 2 * DA_HEAD_DIM
    hyw = n_out - 3 * a
    blocks_per_seq = L // tm
    tiles_per_chunk = tk // tm
    tab_spec = pl.BlockSpec((tm, LANES), lambda i: (i % blocks_per_seq, 0))
    tabt_spec = pl.BlockSpec((ROPE_DIM // 2, tm), lambda i: (0, i % blocks_per_seq))
    row = lambda width: pl.BlockSpec((tm, width), lambda i: (i, 0))
    qt_spec = pl.BlockSpec((1, 1, a, tm), lambda i: (i // blocks_per_seq, i % blocks_per_seq, 0, 0))
    vt_spec = pl.BlockSpec((1, 1, a, tm), lambda i: (i // blocks_per_seq, (i % blocks_per_seq) // tiles_per_chunk,
                                                     0, i % tiles_per_chunk))
    vmem = 2 * tm * D * 4 + D * (n_out + 2 * a) * 2 + 8 * tm * LANES * 4 + 2 * tm * n_out * 2 + 4 * tm * 1536 * 4
    q_scale = DA_HEAD_DIM ** -0.5 * LOG2E
    wqt, wvt = w[:, 0:a].T, w[:, 2 * a:3 * a].T
    return pl.pallas_call(
        functools.partial(_inproj_kernel, q_scale=q_scale),
        grid=(T // tm,),
        in_specs=[row(D), _const_spec((1, D)), _const_spec((D, n_out)), _const_spec((a, D)), _const_spec((a, D)),
                  tab_spec, tab_spec, tab_spec, tabt_spec, tabt_spec],
        out_specs=[qt_spec, row(a), vt_spec, row(hyw)],
        out_shape=[jax.ShapeDtypeStruct((B, L // tm, a, tm), BF16), jax.ShapeDtypeStruct((T, a), BF16),
                   jax.ShapeDtypeStruct((B, L // tk, a, tk), BF16), jax.ShapeDtypeStruct((T, hyw), BF16)],
        compiler_params=_params(("parallel",), vmem),
        name="inproj",
    )(x, g, w, wqt, wvt, *tables)


def _attn_kernel(lam_ref, qt_ref, k_ref, vt_ref, g_ref, o_ref, s_ref, *, tk, kb, lam_init):
    qt = qt_ref[0, 0]
    tq = qt.shape[1]
    L = k_ref.shape[1]
    feat = lax.broadcasted_iota(jnp.int32, qt.shape, 0)
    zero = jnp.zeros_like(qt)
    qz = (jnp.where(feat < DA_HEAD_DIM, qt, zero), jnp.where(feat >= DA_HEAD_DIM, qt, zero))

    nk, nb = L // tk, tk // kb
    neg = jnp.full((1, tq), -1e30, F32)

    def score_block(j, c, b, mx):
        start = pl.multiple_of(j * tk + b * kb, kb)
        s = _dot(k_ref[0, pl.ds(start, kb), :], qz[c])
        s_ref[c, b * kb:(b + 1) * kb, :] = s
        return jnp.maximum(mx, jnp.max(s, axis=0, keepdims=True))

    def value_block(j, c, b, mn, lsum, pv):
        p = jnp.exp2(s_ref[c, b * kb:(b + 1) * kb, :] - mn)
        lsum = lsum + jnp.sum(p, axis=0, keepdims=True)
        pv = pv + _dot(vt_ref[0, j, :, b * kb:(b + 1) * kb], p.astype(BF16))
        return lsum, pv

    def step(score_of, value_of, mx_cur, state):
        m, l, acc = state
        mn = jnp.maximum(m, mx_cur)
        alpha = jnp.exp2(m - mn)
        lsum, pv, mx = jnp.zeros((1, tq), F32), jnp.zeros((DA_V_DIM, tq), F32), neg
        for b in range(nb):
            if score_of is not None:
                mx = score_block(*score_of, b, mx)
            lsum, pv = value_block(*value_of, b, mn, lsum, pv)
        return mx, (mn, alpha * l + lsum, alpha * acc + pv)

    mx0 = neg
    for b in range(nb):
        mx0 = score_block(0, 0, b, mx0)
    fresh = (neg, jnp.zeros((1, tq), F32), jnp.zeros((DA_V_DIM, tq), F32))

    def chunk(j, carry):
        mx0, st0, st1 = carry
        mx1, st0 = step((j, 1), (j, 0), mx0, st0)
        mx0, st1 = step((j + 1, 0), (j, 1), mx1, st1)
        return mx0, st0, st1

    mx0, st0, st1 = lax.fori_loop(0, nk - 1, chunk, (mx0, fresh, fresh), unroll=True)
    mx1, st0 = step((nk - 1, 1), (nk - 1, 0), mx0, st0)
    _, st1 = step(None, (nk - 1, 1), mx1, st1)
    (_, l0, a0), (_, l1, a1) = st0, st1

    lp = lam_ref[...]
    lam = (jnp.exp(jnp.sum(lp[0:1] * lp[1:2], axis=-1, keepdims=True))
           - jnp.exp(jnp.sum(lp[2:3] * lp[3:4], axis=-1, keepdims=True)) + lam_init)
    o = a0 / l0 - lam * (a1 / l1)
    ms = jnp.mean(o * o, axis=0, keepdims=True)
    g = jnp.concatenate([g_ref[...]] * (tq // LANES), axis=1)
    o = o * lax.rsqrt(ms + EPS) * g * (1.0 - lam_init)
    o_ref[0] = o.T.astype(BF16)


def _diff_attention(lam_params, qt, k, vt, g, lam_init, tq, tk):
    B, L, A = k.shape
    qspec = pl.BlockSpec((1, 1, LANES, tq), lambda b, h, i: (b, i, h, 0))
    ospec = pl.BlockSpec((1, tq, LANES), lambda b, h, i: (b, i, h))
    kspec = pl.BlockSpec((1, L, LANES), lambda b, h, i: (b, 0, h))
    vspec = pl.BlockSpec((1, L // tk, LANES, tk), lambda b, h, i: (b, 0, h, 0))
    g_cols = jnp.broadcast_to(g.reshape(DA_V_DIM, 1), (DA_V_DIM, LANES))
    kb = min(512, tk)
    vmem = 4 * L * LANES * 2 + 4 * tq * LANES * 2 + 2 * tq * tk * 4 + 4 * tq * kb * 6 + 16 * tq * LANES * 4
    return pl.pallas_call(
        functools.partial(_attn_kernel, tk=tk, kb=kb, lam_init=lam_init),
        grid=(B, DA_HEADS, L // tq),
        in_specs=[_const_spec(lam_params.shape), qspec, kspec, vspec, _const_spec((DA_V_DIM, LANES))],
        out_specs=ospec,
        out_shape=jax.ShapeDtypeStruct((B, L, A), BF16),
        scratch_shapes=[pltpu.VMEM((2, tk, tq), F32)],
        compiler_params=_params(("parallel", "parallel", "parallel"), vmem),
        name="diff_attn",
    )(lam_params, qt, k, vt, g_cols)


class _FftPlan:
    def __init__(self, L):
        n = 2 * L
        lg = int(math.log2(n))
        assert 2 ** lg == n
        self.L, self.n = L, n
        self.N1 = 2 ** (lg // 2)
        self.N2 = n // self.N1
        self.K1 = self.N1 // 2 + 1
        self.R = _round_up(2 * self.K1, BF16_SUBLANES)
        self.P = self.N2 + SUBLANES
        assert (self.P // SUBLANES) % 2 == 1
        assert self.N2 % BF16_SUBLANES == 0 and (self.N1 // 2) % BF16_SUBLANES == 0


@functools.lru_cache(maxsize=None)
def _fft_consts(L):
    p = _FftPlan(L)
    N1, N2, K1, R, n = p.N1, p.N2, p.K1, p.R, p.n
    t2 = np.arange(N2, dtype=np.float64)[:, None, None]
    k1 = np.arange(K1, dtype=np.float64)[None, :, None]
    t1 = np.arange(N1, dtype=np.float64)[None, None, :]
    ang = -2.0 * np.pi * (t1 * k1 / N1 + t2 * k1 / n)
    f1 = np.zeros((N2, R, N1))
    f1[:, 0:2 * K1:2, :] = np.cos(ang)
    f1[:, 1:2 * K1:2, :] = np.sin(ang)
    c = np.where((np.arange(K1) == 0) | (np.arange(K1) == N1 // 2), 1.0, 2.0)[None, :, None]
    th = -ang[:, :, :N1 // 2]
    g1 = np.zeros((N2, N1 // 2, R))
    g1[:, :, 0:2 * K1:2] = np.transpose(c * np.cos(th), (0, 2, 1))
    g1[:, :, 1:2 * K1:2] = np.transpose(-c * np.sin(th), (0, 2, 1))
    a2 = 2.0 * np.pi * np.outer(np.arange(N2), np.arange(N2)) / N2
    C, S = np.cos(a2), np.sin(a2)
    f2 = np.block([[C, S], [-S, C]])
    return (jnp.asarray(f1, BF16), jnp.asarray(g1, BF16), jnp.asarray(f2, BF16), jnp.asarray(f2.T, BF16))


def _load_rows(ref, idx):
    parts = [ref[g, idx, :] for g in range(ref.shape[0])]
    return parts[0] if len(parts) == 1 else jnp.concatenate(parts, axis=1)


def _store_rows(ref, idx, val):
    for g in range(ref.shape[0]):
        ref[g, idx, :] = val[:, g * LANES:(g + 1) * LANES]


def _store_slabs(ref, first_slab, val, plan):
    for s in range(val.shape[0] // plan.N2):
        start = pl.multiple_of((first_slab + s) * plan.P, SUBLANES)
        _store_rows(ref, pl.ds(start, plan.N2), val[s * plan.N2:(s + 1) * plan.N2])


def _load_slabs(ref, first_slab, count, plan):
    parts = [_load_rows(ref, pl.ds(pl.multiple_of((first_slab + s) * plan.P, SUBLANES), plan.N2))
             for s in range(count)]
    return parts[0] if count == 1 else jnp.concatenate(parts, axis=0)


def _dft_stage1(src_ref, f1_ref, a_ref, n_slabs, plan):
    def body(t2, _):
        rows = _load_rows(src_ref, pl.ds(t2, n_slabs, stride=plan.P)).astype(BF16)
        _store_rows(a_ref, pl.ds(t2, plan.R, stride=plan.P), _dot(f1_ref[t2], rows))
        return 0

    lax.fori_loop(0, plan.N2, body, 0, unroll=STAGE1_UNROLL)


@functools.lru_cache(maxsize=None)
def _filter_features(L):
    t = np.linspace(0.0, 1.0, L)[:, None]
    bands = (FILTER_EMB - 1) // 2
    w = 2.0 * np.pi * np.arange(L)[:, None] / L
    f = np.linspace(1e-4, bands - 1, bands)[None, :]
    z = np.concatenate([t, np.cos(f * w), -np.sin(f * w)], axis=-1)
    z_rev = np.concatenate([z[:1], z[:0:-1]], axis=0)
    zz = np.zeros((2 * L, LANES))
    zz[:L, :FILTER_EMB] = z
    zz[L:, :FILTER_EMB] = z_rev
    return jnp.asarray(zz, BF16)


def _decay_rates(width):
    max_decay = math.log(DECAY_TARGET) / FAST_DECAY_PCT
    min_decay = math.log(DECAY_TARGET) / SLOW_DECAY_PCT
    return jnp.asarray(np.abs(np.linspace(min_decay, max_decay, width))[None, :], F32)


def _filter_kernel(z_ref, w1_ref, b1_ref, fr1_ref, w2_ref, b2_ref, fr2_ref,
                   w3f_ref, w3b_ref, dl_ref, f1_ref, f2_ref, kf_ref,
                   h_ref, kern_ref, a_ref, *, plan, ch):
    L, N1, N2, K1 = plan.L, plan.N1, plan.N2, plan.K1
    n_ch = (2 * L) // ch

    @pl.when(pl.program_id(0) == 0)
    def _():
        def mlp(i, _):
            r0 = pl.multiple_of(i * ch, ch)
            h = jnp.sin(fr1_ref[...] * (_dot(z_ref[pl.ds(r0, ch), :], w1_ref[...]) + b1_ref[...]))
            h = jnp.sin(fr2_ref[...] * (_dot(h.astype(BF16), w2_ref[...]) + b2_ref[...]))
            h_ref[pl.ds(r0, ch), :] = h.astype(BF16)
            return 0
        lax.fori_loop(0, n_ch, mlp, 0)

    tc = kf_ref.shape[1]
    local = lax.broadcasted_iota(jnp.int32, (ch, tc), 0)

    def synth(i, asum):
        r0 = pl.multiple_of(i * ch, ch)
        row = local + r0
        h = h_ref[pl.ds(r0, ch), :]
        fwd = _dot(h, w3f_ref[...])
        bwd = _dot(h, w3b_ref[...])
        lag = jnp.where(row < L, row, 2 * L - row).astype(F32)
        win = jnp.exp(lag * (-1.0 / (L - 1)) * dl_ref[...])
        val = jnp.where(row < L, fwd + jnp.where(row == 0, bwd, 0.0), bwd) * win
        val = jnp.where(row == L, 0.0, val)
        _store_slabs(kern_ref, i * (ch // N2), val, plan)
        return asum + jnp.sum(jnp.abs(val), axis=0, keepdims=True)

    asum = lax.fori_loop(0, n_ch, synth, jnp.zeros((1, tc), F32))
    scale = 1.0 / (asum * float(plan.n))

    _dft_stage1(kern_ref, f1_ref, a_ref, N1, plan)

    def stage2(k1, _):
        r0 = pl.multiple_of(k1 * 2 * N2, 2 * N2)
        slab = _load_slabs(a_ref, 2 * k1, 2, plan).astype(BF16)
        kf_ref[pl.ds(r0, 2 * N2), :] = (_dot(f2_ref[...], slab) * scale).astype(kf_ref.dtype)
        return 0

    lax.fori_loop(0, K1, stage2, 0, unroll=STAGE2_UNROLL)
    pad = plan.R - 2 * K1
    if pad:
        kf_ref[pl.ds(2 * K1 * N2, pad * N2), :] = jnp.zeros((pad * N2, tc), kf_ref.dtype)


def _hyena_filter_spectrum(L, w1, b1, fr1, w2, b2, fr2, w3, tc):
    plan = _FftPlan(L)
    C = w3.shape[1] // 2
    zz = _filter_features(L)
    f1, _, f2, _ = _fft_consts(L)
    w1p = jnp.zeros((LANES, FILTER_ORDER), BF16).at[:FILTER_EMB].set(w1.astype(BF16))
    ch = min(512, L)
    rows_a = plan.R * plan.N2
    cblock = lambda off: pl.BlockSpec((FILTER_ORDER, tc), lambda c: (0, off + c))
    vmem = (2 * L * LANES * 2 * 2 + 2 * L * tc * 4 + rows_a * tc * 4 + 2 * rows_a * tc * 2
            + f1.size * 2 * 2 + 8 * ch * tc * 4)
    return pl.pallas_call(
        functools.partial(_filter_kernel, plan=plan, ch=ch),
        grid=(C // tc,),
        in_specs=[_const_spec(zz.shape),
                  _const_spec(w1p.shape), _const_spec((1, FILTER_ORDER)), _const_spec((1, FILTER_ORDER)),
                  _const_spec((FILTER_ORDER, FILTER_ORDER)), _const_spec((1, FILTER_ORDER)),
                  _const_spec((1, FILTER_ORDER)),
                  cblock(0), cblock(C // tc),
                  pl.BlockSpec((1, tc), lambda c: (0, c)),
                  _const_spec(f1.shape), _const_spec(f2.shape)],
        out_specs=pl.BlockSpec((rows_a, tc), lambda c: (0, c)),
        out_shape=jax.ShapeDtypeStruct((rows_a, C), BF16),
        scratch_shapes=[pltpu.VMEM((2 * L, FILTER_ORDER), BF16),
                        pltpu.VMEM((tc // LANES, plan.N1 * plan.P, LANES), F32),
                        pltpu.VMEM((tc // LANES, plan.R * plan.P, LANES), F32)],
        compiler_params=_params(("arbitrary",), vmem),
        name="hyena_filter",
    )(zz, w1p, b1, fr1, w2.astype(BF16), b2, fr2, w3.astype(BF16), w3.astype(BF16),
      _decay_rates(C), f1, f2)


def _short_conv_chunk(raw_ref, w_ref, b_ref, r0, ch, L):
    halo = BF16_SUBLANES
    c = raw_ref[0, pl.ds(r0, ch), :].astype(F32)
    p0 = pl.multiple_of(jnp.maximum(r0 - halo, 0), halo)
    n0 = pl.multiple_of(jnp.minimum(r0 + ch, L - halo), halo)
    prev = raw_ref[0, pl.ds(p0, halo), :].astype(F32)[halo - 1:halo]
    nxt = raw_ref[0, pl.ds(n0, halo), :].astype(F32)[0:1]
    prev = jnp.where(r0 == 0, 0.0, prev)
    nxt = jnp.where(r0 + ch == L, 0.0, nxt)
    rows = lax.broadcasted_iota(jnp.int32, c.shape, 0)
    up = jnp.where(rows == 0, prev, pltpu.roll(c, 1, 0))
    un = jnp.where(rows == ch - 1, nxt, pltpu.roll(c, ch - 1, 0))
    w = w_ref[...]
    return w[0:1] * up + w[1:2] * c + w[2:3] * un + b_ref[...]


def _hyena_kernel(x0_ref, x1_ref, vh_ref, w0_ref, w1_ref, wv_ref, b0_ref, b1_ref, bv_ref,
                  d_ref, kf_ref, f1_ref, g1_ref, f2_ref, f2i_ref, o_ref,
                  u_ref, a_ref, y_ref, *, plan, ch):
    L, N1, N2, K1, R = plan.L, plan.N1, plan.N2, plan.K1, plan.R
    n_ch = L // ch

    def gate_in(i, _):
        r0 = pl.multiple_of(i * ch, ch)
        x1 = _short_conv_chunk(x1_ref, w1_ref, b1_ref, r0, ch, L)
        vh = _short_conv_chunk(vh_ref, wv_ref, bv_ref, r0, ch, L)
        _store_slabs(u_ref, i * (ch // N2), x1 * vh, plan)
        return 0

    lax.fori_loop(0, n_ch, gate_in, 0)

    _dft_stage1(u_ref, f1_ref, a_ref, N1 // 2, plan)

    def spectral(k1, _):
        r0 = pl.multiple_of(k1 * 2 * N2, 2 * N2)
        x = _dot(f2_ref[...], _load_slabs(a_ref, 2 * k1, 2, plan).astype(BF16))
        xr, xi = x[:N2], x[N2:]
        kr = kf_ref[pl.ds(r0, N2), :].astype(F32)
        ki = kf_ref[pl.ds(r0 + N2, N2), :].astype(F32)
        y = jnp.concatenate([xr * kr - xi * ki, xr * ki + xi * kr], axis=0).astype(BF16)
        _store_slabs(a_ref, 2 * k1, _dot(f2i_ref[...], y), plan)
        return 0

    lax.fori_loop(0, K1, spectral, 0, unroll=STAGE2_UNROLL)

    def inverse1(t2, _):
        rows = _load_rows(a_ref, pl.ds(t2, R, stride=plan.P)).astype(BF16)
        _store_rows(y_ref, pl.ds(t2, N1 // 2, stride=plan.P), _dot(g1_ref[t2], rows))
        return 0

    lax.fori_loop(0, N2, inverse1, 0, unroll=STAGE1_UNROLL)

    def gate_out(i, _):
        r0 = pl.multiple_of(i * ch, ch)
        x0 = _short_conv_chunk(x0_ref, w0_ref, b0_ref, r0, ch, L)
        u = _load_slabs(u_ref, i * (ch // N2), ch // N2, plan)
        y = _load_slabs(y_ref, i * (ch // N2), ch // N2, plan)
        o_ref[0, pl.ds(r0, ch), :] = ((y + d_ref[...] * u) * x0).astype(o_ref.dtype)
        return 0

    lax.fori_loop(0, n_ch, gate_out, 0)


def _hyena_vmem_bytes(plan, tc, ch):
    L, rows_a = plan.L, plan.R * plan.N2
    pad_l = lambda m: _round_up(m, LANES)
    return (6 * L * tc * 2 + rows_a * tc * 2 + 2 * L * tc * 4 + rows_a * tc * 4 + 2 * L * tc * 2
            + plan.N2 * plan.R * pad_l(plan.N1 // 2) * 2 + plan.N2 * (plan.N1 // 2) * pad_l(plan.R) * 2
            + 16 * ch * tc * 4 + 16 * plan.N2 * tc * 4)


def _hyena_operator(hy, conv_w, conv_b, d, kf):
    B, L, W3 = hy.shape
    C = W3 // 3
    plan = _FftPlan(L)
    ch = min(512, L)
    tc = 2 * LANES if int(_hyena_vmem_bytes(plan, 2 * LANES, ch) * 1.25) + (4 << 20) <= VMEM_LIMIT_CAP else LANES
    nb = C // tc
    f1full, g1, f2, f2i = _fft_consts(L)
    f1 = f1full[:, :, :plan.N1 // 2]
    rows_a = plan.R * plan.N2
    seq = lambda part: pl.BlockSpec((1, L, tc), lambda c, b: (b, 0, part * nb + c))
    wspec = lambda part: pl.BlockSpec((SHORT_CONV, tc), lambda c, b: (0, part * nb + c))
    bspec = lambda part: pl.BlockSpec((1, tc), lambda c, b: (0, part * nb + c))
    vmem = _hyena_vmem_bytes(plan, tc, ch)
    return pl.pallas_call(
        functools.partial(_hyena_kernel, plan=plan, ch=ch),
        grid=(nb, B),
        in_specs=[seq(0), seq(1), seq(2), wspec(0), wspec(1), wspec(2), bspec(0), bspec(1), bspec(2),
                  pl.BlockSpec((1, tc), lambda c, b: (0, c)),
                  pl.BlockSpec((rows_a, tc), lambda c, b: (0, c), pipeline_mode=pl.Buffered(1)),
                  _const_spec(f1.shape), _const_spec(g1.shape), _const_spec(f2.shape), _const_spec(f2i.shape)],
        out_specs=pl.BlockSpec((1, L, tc), lambda c, b: (b, 0, c)),
        out_shape=jax.ShapeDtypeStruct((B, L, C), BF16),
        scratch_shapes=[pltpu.VMEM((tc // LANES, (plan.N1 // 2) * plan.P, LANES), F32),
                        pltpu.VMEM((tc // LANES, plan.R * plan.P, LANES), F32),
                        pltpu.VMEM((tc // LANES, (plan.N1 // 2) * plan.P, LANES), F32)],
        compiler_params=_params(("parallel", "parallel"), vmem),
        name="hyena_op",
    )(hy, hy, hy, conv_w, conv_w, conv_w, conv_b, conv_b, conv_b, d, kf, f1, g1, f2, f2i)


def _memkv_kernel(m_ref, g_ref, wk_ref, wv_ref, k_ref, v_ref):
    mn = _rms(m_ref[...], g_ref[...]).astype(BF16)
    k_ref[...] = _dot(mn, wk_ref[...]).astype(BF16)
    v_ref[...] = _dot(mn, wv_ref[...]).astype(BF16)


def _memkv(mem, g, wk, wv, tm):
    T, D = mem.shape
    row = pl.BlockSpec((tm, D), lambda i: (i, 0))
    vmem = 2 * tm * D * 4 + 2 * D * D * 2 + 4 * tm * D * 2 + 3 * tm * D * 4
    return pl.pallas_call(
        _memkv_kernel,
        grid=(T // tm,),
        in_specs=[row, _const_spec((1, D)), _const_spec((D, D)), _const_spec((D, D))],
        out_specs=[row, row],
        out_shape=[jax.ShapeDtypeStruct((T, D), BF16)] * 2,
        compiler_params=_params(("parallel",), vmem),
        name="mem_kv",
    )(mem, g, wk, wv)


def _mixout_cross_kernel(a_ref, y_ref, x_ref, wa_ref, wy_ref, gmix_ref, k_ref, v_ref,
                         gpre_ref, wq_ref, wo_ref, gpost_ref, o_ref, *, q_scale):
    z = _dot(a_ref[...], wa_ref[...]) + _dot(y_ref[...], wy_ref[...])
    x = x_ref[...] + _rms(z, gmix_ref[...])
    xn = _rms(x, gpre_ref[...]).astype(BF16)
    q = (_dot(xn, wq_ref[...]) * q_scale).astype(BF16)
    hd = q.shape[1] // X_HEADS
    outs = []
    for h in range(X_HEADS):
        sl = slice(h * hd, (h + 1) * hd)
        s = _dot_nt(q[:, sl], k_ref[0, :, sl])
        p = jnp.exp2(s - jnp.max(s, axis=-1, keepdims=True))
        l = jnp.sum(p, axis=-1, keepdims=True)
        outs.append((_dot(p.astype(BF16), v_ref[0, :, sl]) / l).astype(BF16))
    z = _dot(jnp.concatenate(outs, axis=1), wo_ref[...])
    o_ref[...] = x + _rms(z, gpost_ref[...])


def _mixout_cross_block(attn, y, x, w_out, gmix, k, v, gpre, wq, wo, gpost, L, tm):
    T, D = x.shape
    a, c = attn.shape[1], y.shape[1]
    n_mem = k.shape[1]
    blocks_per_seq = L // tm
    row = lambda width: pl.BlockSpec((tm, width), lambda i: (i, 0))
    kv = pl.BlockSpec((1, n_mem, D), lambda i: (i // blocks_per_seq, 0, 0))
    vmem = (2 * tm * (a + c) * 2 + 4 * tm * D * 4 + 4 * n_mem * D * 2 + (a + c + 2 * D) * D * 2
            + 8 * tm * D * 4)
    q_scale = (D // X_HEADS) ** -0.5 * LOG2E
    return pl.pallas_call(
        functools.partial(_mixout_cross_kernel, q_scale=q_scale),
        grid=(T // tm,),
        in_specs=[row(a), row(c), row(D), _const_spec((a, D)), _const_spec((c, D)), _const_spec((1, D)),
                  kv, kv, _const_spec((1, D)), _const_spec((D, D)), _const_spec((D, D)), _const_spec((1, D))],
        out_specs=row(D),
        out_shape=jax.ShapeDtypeStruct((T, D), F32),
        compiler_params=_params(("parallel",), vmem),
        name="mixout_cross",
    )(attn, y, x, w_out[:a], w_out[a:], gmix, k, v, gpre, wq, wo, gpost)


def _swiglu_kernel(x_ref, gpre_ref, wg_ref, wu_ref, wd_ref, gpost_ref, o_ref, *, n_split):
    x = x_ref[...]
    xn = _rms(x, gpre_ref[...]).astype(BF16)
    ff = wg_ref.shape[1]
    cw = ff // n_split
    z = None
    for c in range(n_split):
        sl = slice(c * cw, (c + 1) * cw)
        gate = _dot(xn, wg_ref[:, sl])
        up = _dot(xn, wu_ref[:, sl])
        h = (gate * (1.0 / (1.0 + jnp.exp(-gate))) * up).astype(BF16)
        part = _dot(h, wd_ref[sl, :])
        z = part if z is None else z + part
    o_ref[...] = x + _rms(z, gpost_ref[...])


def _swiglu_block(x, gpre, wg, wu, wd, gpost, tm):
    T, D = x.shape
    ff = wg.shape[1]
    n_split = ff // MXU_WIDTH if ff % MXU_WIDTH == 0 else 1
    row = pl.BlockSpec((tm, D), lambda i: (i, 0))
    vmem = 4 * tm * D * 4 + 3 * D * ff * 2 + 4 * tm * (ff // n_split) * 4 + 4 * tm * D * 4
    return pl.pallas_call(
        functools.partial(_swiglu_kernel, n_split=n_split),
        grid=(T // tm,),
        in_specs=[row, _const_spec((1, D)), _const_spec((D, ff)), _const_spec((D, ff)),
                  _const_spec((ff, D)), _const_spec((1, D))],
        out_specs=row,
        out_shape=jax.ShapeDtypeStruct((T, D), F32),
        compiler_params=_params(("parallel",), vmem),
        name="swiglu",
    )(x, gpre, wg, wu, wd, gpost)


def _trunk(x, mem, P):
    B, L, D = x.shape
    depth = P['w_in'].shape[0]
    n_mem = mem.shape[1]
    tm = min(512, L)
    tq, tk = tm, min(1024, L)
    tc = LANES
    a = DA_HEADS * 2 * DA_HEAD_DIM
    tables = _rope_tables(L)
    xf = x.reshape(B * L, D)
    memf = mem.reshape(B * n_mem, D)
    row = lambda v: v.reshape(1, -1)
    for l in range(depth):
        lam_init = 0.8 - 0.6 * math.exp(-0.3 * l)
        qt, k, vt, hy = _inproj(xf, row(P['ln_mix_pre'][l]), P['w_in'][l].astype(BF16), tables, B, L, tm, tk)
        lam_params = jnp.stack([P['lambda_q1'][l], P['lambda_k1'][l], P['lambda_q2'][l], P['lambda_k2'][l]])
        attn = _diff_attention(lam_params, qt, k.reshape(B, L, a), vt, P['subln_g'][l], lam_init, tq, tk)
        kf = _hyena_filter_spectrum(L, P['filt_w1'][l], row(P['filt_b1'][l]), row(P['filt_freq1'][l]),
                                    P['filt_w2'][l], row(P['filt_b2'][l]), row(P['filt_freq2'][l]),
                                    P['filt_w3'][l], tc)
        y = _hyena_operator(hy.reshape(B, L, -1), P['conv_w'][l], row(P['conv_b'][l]),
                            row(P['hyena_d'][l]), kf)
        km, vm = _memkv(memf, row(P['ln_mem'][l]), P['wk_x'][l].astype(BF16), P['wv_x'][l].astype(BF16),
                        min(512, B * n_mem))
        xf = _mixout_cross_block(attn.reshape(B * L, a), y.reshape(B * L, -1), xf, P['w_out'][l].astype(BF16),
                                 row(P['ln_mix_post'][l]), km.reshape(B, n_mem, D), vm.reshape(B, n_mem, D),
                                 row(P['ln_x_pre'][l]), P['wq_x'][l].astype(BF16), P['wo_x'][l].astype(BF16),
                                 row(P['ln_x_post'][l]), L, tm)
        xf = _swiglu_block(xf, row(P['ln_ffn_pre'][l]), P['w_gate'][l].astype(BF16), P['w_up'][l].astype(BF16),
                           P['w_down'][l].astype(BF16), row(P['ln_ffn_post'][l]), tm)
    return xf.reshape(B, L, D)


def kernel(x_prompt, x_sample, mem_prompt, mem_sample, ln_mix_pre, ln_mix_post, w_in, lambda_q1, lambda_k1, lambda_q2, lambda_k2, subln_g, conv_w, conv_b, filt_w1, filt_b1, filt_freq1, filt_w2, filt_b2, filt_freq2, filt_w3, hyena_d, w_out, ln_x_pre, ln_x_post, ln_mem, wq_x, wk_x, wv_x, wo_x, ln_ffn_pre, ln_ffn_post, w_gate, w_up, w_down):
    P = dict(ln_mix_pre=ln_mix_pre, ln_mix_post=ln_mix_post, w_in=w_in,
             lambda_q1=lambda_q1, lambda_k1=lambda_k1, lambda_q2=lambda_q2, lambda_k2=lambda_k2,
             subln_g=subln_g, conv_w=conv_w, conv_b=conv_b,
             filt_w1=filt_w1, filt_b1=filt_b1, filt_freq1=filt_freq1,
             filt_w2=filt_w2, filt_b2=filt_b2, filt_freq2=filt_freq2, filt_w3=filt_w3,
             hyena_d=hyena_d, w_out=w_out,
             ln_x_pre=ln_x_pre, ln_x_post=ln_x_post, ln_mem=ln_mem,
             wq_x=wq_x, wk_x=wk_x, wv_x=wv_x, wo_x=wo_x,
             ln_ffn_pre=ln_ffn_pre, ln_ffn_post=ln_ffn_post,
             w_gate=w_gate, w_up=w_up, w_down=w_down)
    return (_trunk(x_prompt, mem_prompt, P), _trunk(x_sample, mem_sample, P))
```

```python
import functools
import math

import numpy as np
import jax
import jax.numpy as jnp
from jax import lax
from jax.experimental import pallas as pl
from jax.experimental.pallas import tpu as pltpu

F32 = jnp.float32
BF16 = jnp.bfloat16

DA_HEADS = 4
DA_HEAD_DIM = 64
DA_V_DIM = 128
ROPE_DIM = 16
ROPE_THETA = 500000.0
SHORT_CONV = 3
FILTER_EMB = 33
FILTER_ORDER = 64
FAST_DECAY_PCT = 0.3
SLOW_DECAY_PCT = 1.5
DECAY_TARGET = 1e-2
X_HEADS = 4
EPS = 1e-6
LOG2E = 1.4426950408889634

LANES = 128
SUBLANES = 8
BF16_SUBLANES = 16
MXU_WIDTH = 256
VMEM_LIMIT_CAP = 58 * 2**20
VMEM_TEMP_FACTOR = 1.25
VMEM_TEMP_BYTES = 4 * 2**20

ROW_TILE = 512
KEY_CHUNK = 1024
KEY_BLOCK = 512
CONV_CHUNK = 512
ATTN_PASSES_PER_STEP = 16

STAGE1_UNROLL = 32
STAGE2_UNROLL = 16


def _round_up(a, b):
    return -(-a // b) * b


def _vmem_limit(vmem_bytes):
    return int(vmem_bytes * VMEM_TEMP_FACTOR) + VMEM_TEMP_BYTES


def _params(semantics, vmem_bytes):
    limit = min(_vmem_limit(vmem_bytes), VMEM_LIMIT_CAP)
    return pltpu.CompilerParams(dimension_semantics=semantics, vmem_limit_bytes=limit)


def _const_spec(shape):
    nd = len(shape)
    return pl.BlockSpec(shape, lambda *_: (0,) * nd, pipeline_mode=pl.Buffered(1))


def _rms(x, g):
    ms = jnp.mean(x * x, axis=-1, keepdims=True)
    return x * lax.rsqrt(ms + EPS) * g


def _dot(a, b):
    return jnp.dot(a, b, preferred_element_type=F32)


def _dot_nt(a, b):
    return lax.dot_general(a, b, (((1,), (1,)), ((), ())), preferred_element_type=F32)


def _rope_tables(L):
    inv = ROPE_THETA ** (-np.arange(0, ROPE_DIM, 2, dtype=np.float64) / ROPE_DIM)
    ang = np.arange(L, dtype=np.float64)[:, None] * inv[None, :]
    half = ROPE_DIM // 2
    cos = np.ones((L, LANES)); sa = np.zeros((L, LANES)); sb = np.zeros((L, LANES))
    for g in range(LANES // DA_HEAD_DIM):
        o = g * DA_HEAD_DIM
        cos[:, o:o + half] = np.cos(ang)
        cos[:, o + half:o + ROPE_DIM] = np.cos(ang)
        sb[:, o:o + half] = -np.sin(ang)
        sa[:, o + half:o + ROPE_DIM] = np.sin(ang)
    return (jnp.asarray(cos, F32), jnp.asarray(sa, F32), jnp.asarray(sb, F32),
            jnp.asarray(np.cos(ang).T, F32), jnp.asarray(np.sin(ang).T, F32))


def _inproj_kernel(x_ref, g_ref, w_ref, wqt_ref, wvt_ref, cos_ref, sa_ref, sb_ref, cost_ref, sint_ref,
                   qt_ref, k_ref, vt_ref, hy_ref, *, q_scale):
    xn = _rms(x_ref[...], g_ref[...]).astype(BF16)
    cos, sa, sb = cos_ref[...], sa_ref[...], sb_ref[...]
    half = ROPE_DIM // 2

    def rope(y):
        parts = []
        for s in range(y.shape[1] // LANES):
            ys = y[:, s * LANES:(s + 1) * LANES]
            parts.append(ys * cos + pltpu.roll(ys, half, 1) * sa
                         + pltpu.roll(ys, LANES - half, 1) * sb)
        return jnp.concatenate(parts, axis=1)

    def rope_t(y):
        ct, st = cost_ref[...], sint_ref[...]
        parts = []
        for o in range(0, y.shape[0], DA_HEAD_DIM):
            x1, x2 = y[o:o + half], y[o + half:o + ROPE_DIM]
            parts += [x1 * ct - x2 * st, x2 * ct + x1 * st, y[o + ROPE_DIM:o + DA_HEAD_DIM]]
        return jnp.concatenate(parts, axis=0)

    a = k_ref.shape[1]
    qt_ref[0, 0] = (rope_t(_dot_nt(wqt_ref[...], xn)) * q_scale).astype(BF16)
    k_ref[...] = rope(_dot(xn, w_ref[:, a:2 * a])).astype(BF16)
    vt_ref[0, 0] = _dot_nt(wvt_ref[...], xn).astype(BF16)
    hy_ref[...] = _dot(xn, w_ref[:, 3 * a:]).astype(BF16)


def _inproj(x, g, w, tables, B, L, tm, tk):
    T, D = x.shape
    n_out = w.shape[1]
    a = DA_HEADS * 2 * DA_HEAD_DIM
    hyw = n_out - 3 * a
    blocks_per_seq = L // tm
    tiles_per_chunk = tk // tm
    tab_spec = pl.BlockSpec((tm, LANES), lambda i: (i % blocks_per_seq, 0))
    tabt_spec = pl.BlockSpec((ROPE_DIM // 2, tm), lambda i: (0, i % blocks_per_seq))
    row = lambda width: pl.BlockSpec((tm, width), lambda i: (i, 0))
    qt_spec = pl.BlockSpec((1, 1, a, tm), lambda i: (i // blocks_per_seq, i % blocks_per_seq, 0, 0))
    vt_spec = pl.BlockSpec((1, 1, a, tm), lambda i: (i // blocks_per_seq, (i % blocks_per_seq) // tiles_per_chunk,
                                                     0, i % tiles_per_chunk))
    vmem = 2 * tm * D * 4 + D * (n_out + 2 * a) * 2 + 8 * tm * LANES * 4 + 2 * tm * n_out * 2 + 4 * tm * 1536 * 4
    q_scale = DA_HEAD_DIM ** -0.5 * LOG2E
    wqt, wvt = w[:, 0:a].T, w[:, 2 * a:3 * a].T
    return pl.pallas_call(
        functools.partial(_inproj_kernel, q_scale=q_scale),
        grid=(T // tm,),
        in_specs=[row(D), _const_spec((1, D)), _const_spec((D, n_out)), _const_spec((a, D)), _const_spec((a, D)),
                  tab_spec, tab_spec, tab_spec, tabt_spec, tabt_spec],
        out_specs=[qt_spec, row(a), vt_spec, row(hyw)],
        out_shape=[jax.ShapeDtypeStruct((B, L // tm, a, tm), BF16), jax.ShapeDtypeStruct((T, a), BF16),
                   jax.ShapeDtypeStruct((B, L // tk, a, tk), BF16), jax.ShapeDtypeStruct((T, hyw), BF16)],
        compiler_params=_params(("parallel",), vmem),
        name="inproj",
    )(x, g, w, wqt, wvt, *tables)


def _attn_kernel(lam_ref, qt_ref, k_ref, vt_ref, g_ref, o_ref, s_ref, *, tk, kb, lam_init):
    nt, _, tq = qt_ref.shape[1:]
    L = k_ref.shape[1]
    feat = lax.broadcasted_iota(jnp.int32, (2 * DA_HEAD_DIM, tq), 0)

    def split(qt):
        zero = jnp.zeros_like(qt)
        return (jnp.where(feat < DA_HEAD_DIM, qt, zero), jnp.where(feat >= DA_HEAD_DIM, qt, zero))

    qz = [split(qt_ref[0, t]) for t in range(nt)]

    lp = lam_ref[...]
    lam = (jnp.exp(jnp.sum(lp[0:1] * lp[1:2], axis=-1, keepdims=True))
           - jnp.exp(jnp.sum(lp[2:3] * lp[3:4], axis=-1, keepdims=True)) + lam_init)
    g = jnp.concatenate([g_ref[...]] * (tq // LANES), axis=1)

    nk, nb = L // tk, tk // kb
    neg = jnp.full((1, tq), -1e30, F32)
    fresh = (neg, jnp.zeros((1, tq), F32), jnp.zeros((DA_V_DIM, tq), F32))

    def score_block(t, j, c, b, mx):
        s = _dot(k_ref[0, j * tk + b * kb:j * tk + (b + 1) * kb, :], qz[t][c])
        s_ref[c, b * kb:(b + 1) * kb, :] = s
        return jnp.maximum(mx, jnp.max(s, axis=0, keepdims=True))

    def value_block(j, c, b, mn, lsum, pv):
        p = jnp.exp2(s_ref[c, b * kb:(b + 1) * kb, :] - mn)
        lsum = lsum + jnp.sum(p, axis=0, keepdims=True)
        pv = pv + _dot(vt_ref[0, j, :, b * kb:(b + 1) * kb], p.astype(BF16))
        return lsum, pv

    def step(score_of, value_of, mx_cur, state):
        m, l, acc = state
        mn = jnp.maximum(m, mx_cur)
        alpha = jnp.exp2(m - mn)
        lsum, pv, mx = jnp.zeros((1, tq), F32), jnp.zeros((DA_V_DIM, tq), F32), neg
        for b in range(nb):
            if score_of is not None:
                mx = score_block(*score_of, b, mx)
            lsum, pv = value_block(*value_of[1:], b, mn, lsum, pv)
        return mx, (mn, alpha * l + lsum, alpha * acc + pv)

    def finish(t, st0, st1):
        (_, l0, a0), (_, l1, a1) = st0, st1
        o = a0 / l0 - lam * (a1 / l1)
        ms = jnp.mean(o * o, axis=0, keepdims=True)
        o = o * lax.rsqrt(ms + EPS) * g * (1.0 - lam_init)
        o_ref[0, t * tq:(t + 1) * tq, :] = o.T.astype(BF16)

    passes = [(t, j, c) for t in range(nt) for j in range(nk) for c in range(2)]
    mx = neg
    for b in range(nb):
        mx = score_block(*passes[0], b, mx)
    states = {}
    for n, cur in enumerate(passes):
        t, j, c = cur
        nxt = passes[n + 1] if n + 1 < len(passes) else None
        mx, states[(t, c)] = step(nxt, cur, mx, states.get((t, c), fresh))
        if j == nk - 1 and c == 1:
            finish(t, states.pop((t, 0)), states.pop((t, 1)))


def _diff_attention(lam_params, qt, k, vt, g, lam_init, tq, tk):
    B, L, A = k.shape
    nt = max(1, min(L // tq, ATTN_PASSES_PER_STEP // (2 * (L // tk))))
    qspec = pl.BlockSpec((1, nt, LANES, tq), lambda b, h, i: (b, i, h, 0))
    ospec = pl.BlockSpec((1, nt * tq, LANES), lambda b, h, i: (b, i, h))
    kspec = pl.BlockSpec((1, L, LANES), lambda b, h, i: (b, 0, h))
    vspec = pl.BlockSpec((1, L // tk, LANES, tk), lambda b, h, i: (b, 0, h, 0))
    g_cols = jnp.broadcast_to(g.reshape(DA_V_DIM, 1), (DA_V_DIM, LANES))
    kb = min(KEY_BLOCK, tk)
    vmem = (4 * L * LANES * 2 + 4 * nt * tq * LANES * 2 + 2 * tq * tk * 4 + 4 * tq * kb * 6
            + 16 * tq * LANES * 4)
    return pl.pallas_call(
        functools.partial(_attn_kernel, tk=tk, kb=kb, lam_init=lam_init),
        grid=(B, DA_HEADS, L // (nt * tq)),
        in_specs=[_const_spec(lam_params.shape), qspec, kspec, vspec, _const_spec((DA_V_DIM, LANES))],
        out_specs=ospec,
        out_shape=jax.ShapeDtypeStruct((B, L, A), BF16),
        scratch_shapes=[pltpu.VMEM((2, tk, tq), F32)],
        compiler_params=_params(("parallel", "parallel", "parallel"), vmem),
        name="diff_attn",
    )(lam_params, qt, k, vt, g_cols)


class _FftPlan:
    def __init__(self, L):
        n = 2 * L
        lg = int(math.log2(n))
        assert 2 ** lg == n
        self.L, self.n = L, n
        self.N1 = 2 ** (lg // 2)
        self.N2 = n // self.N1
        self.K1 = self.N1 // 2 + 1
        self.R = _round_up(2 * self.K1, BF16_SUBLANES)
        self.P = self.N2 + SUBLANES
        assert (self.P // SUBLANES) % 2 == 1
        assert self.N2 % BF16_SUBLANES == 0 and (self.N1 // 2) % BF16_SUBLANES == 0


@functools.lru_cache(maxsize=None)
def _fft_consts(L):
    p = _FftPlan(L)
    N1, N2, K1, R, n = p.N1, p.N2, p.K1, p.R, p.n
    t2 = np.arange(N2, dtype=np.float64)[:, None, None]
    k1 = np.arange(K1, dtype=np.float64)[None, :, None]
    t1 = np.arange(N1, dtype=np.float64)[None, None, :]
    ang = -2.0 * np.pi * (t1 * k1 / N1 + t2 * k1 / n)
    f1 = np.zeros((N2, R, N1))
    f1[:, 0:2 * K1:2, :] = np.cos(ang)
    f1[:, 1:2 * K1:2, :] = np.sin(ang)
    c = np.where((np.arange(K1) == 0) | (np.arange(K1) == N1 // 2), 1.0, 2.0)[None, :, None]
    th = -ang[:, :, :N1 // 2]
    g1 = np.zeros((N2, N1 // 2, R))
    g1[:, :, 0:2 * K1:2] = np.transpose(c * np.cos(th), (0, 2, 1))
    g1[:, :, 1:2 * K1:2] = np.transpose(-c * np.sin(th), (0, 2, 1))
    a2 = 2.0 * np.pi * np.outer(np.arange(N2), np.arange(N2)) / N2
    C, S = np.cos(a2), np.sin(a2)
    f2 = np.block([[C, S], [-S, C]])
    return (jnp.asarray(f1, BF16), jnp.asarray(g1, BF16), jnp.asarray(f2, BF16), jnp.asarray(f2.T, BF16))


def _load_rows(ref, idx):
    parts = [ref[g, idx, :] for g in range(ref.shape[0])]
    return parts[0] if len(parts) == 1 else jnp.concatenate(parts, axis=1)


def _store_rows(ref, idx, val):
    for g in range(ref.shape[0]):
        ref[g, idx, :] = val[:, g * LANES:(g + 1) * LANES]


def _store_slabs(ref, first_slab, val, plan):
    for s in range(val.shape[0] // plan.N2):
        start = pl.multiple_of((first_slab + s) * plan.P, SUBLANES)
        _store_rows(ref, pl.ds(start, plan.N2), val[s * plan.N2:(s + 1) * plan.N2])


def _load_slabs(ref, first_slab, count, plan):
    parts = [_load_rows(ref, pl.ds(pl.multiple_of((first_slab + s) * plan.P, SUBLANES), plan.N2))
             for s in range(count)]
    return parts[0] if count == 1 else jnp.concatenate(parts, axis=0)


def _dft_stage1(src_ref, f1_ref, a_ref, n_slabs, plan):
    def body(t2, _):
        rows = _load_rows(src_ref, pl.ds(t2, n_slabs, stride=plan.P)).astype(BF16)
        _store_rows(a_ref, pl.ds(t2, plan.R, stride=plan.P), _dot(f1_ref[t2], rows))
        return 0

    lax.fori_loop(0, plan.N2, body, 0, unroll=STAGE1_UNROLL)


@functools.lru_cache(maxsize=None)
def _filter_features(L):
    t = np.linspace(0.0, 1.0, L)[:, None]
    bands = (FILTER_EMB - 1) // 2
    w = 2.0 * np.pi * np.arange(L)[:, None] / L
    f = np.linspace(1e-4, bands - 1, bands)[None, :]
    z = np.concatenate([t, np.cos(f * w), -np.sin(f * w)], axis=-1)
    z_rev = np.concatenate([z[:1], z[:0:-1]], axis=0)
    zz = np.zeros((2 * L, LANES))
    zz[:L, :FILTER_EMB] = z
    zz[L:, :FILTER_EMB] = z_rev
    return jnp.asarray(zz, BF16)


def _decay_rates(width):
    max_decay = math.log(DECAY_TARGET) / FAST_DECAY_PCT
    min_decay = math.log(DECAY_TARGET) / SLOW_DECAY_PCT
    return jnp.asarray(np.abs(np.linspace(min_decay, max_decay, width))[None, :], F32)


def _filter_kernel(z_ref, w1_ref, b1_ref, fr1_ref, w2_ref, b2_ref, fr2_ref,
                   w3f_ref, w3b_ref, dl_ref, f1_ref, f2_ref, kf_ref,
                   h_ref, kern_ref, a_ref, *, plan, ch):
    L, N1, N2, K1 = plan.L, plan.N1, plan.N2, plan.K1
    n_ch = (2 * L) // ch

    @pl.when(pl.program_id(0) == 0)
    def _():
        def mlp(i, _):
            r0 = pl.multiple_of(i * ch, ch)
            h = jnp.sin(fr1_ref[...] * (_dot(z_ref[pl.ds(r0, ch), :], w1_ref[...]) + b1_ref[...]))
            h = jnp.sin(fr2_ref[...] * (_dot(h.astype(BF16), w2_ref[...]) + b2_ref[...]))
            h_ref[pl.ds(r0, ch), :] = h.astype(BF16)
            return 0
        lax.fori_loop(0, n_ch, mlp, 0)

    tc = kf_ref.shape[1]
    local = lax.broadcasted_iota(jnp.int32, (ch, tc), 0)

    def synth(i, asum):
        r0 = pl.multiple_of(i * ch, ch)
        row = local + r0
        h = h_ref[pl.ds(r0, ch), :]
        fwd = _dot(h, w3f_ref[...])
        bwd = _dot(h, w3b_ref[...])
        lag = jnp.where(row < L, row, 2 * L - row).astype(F32)
        win = jnp.exp(lag * (-1.0 / (L - 1)) * dl_ref[...])
        val = jnp.where(row < L, fwd + jnp.where(row == 0, bwd, 0.0), bwd) * win
        val = jnp.where(row == L, 0.0, val)
        _store_slabs(kern_ref, i * (ch // N2), val, plan)
        return asum + jnp.sum(jnp.abs(val), axis=0, keepdims=True)

    asum = lax.fori_loop(0, n_ch, synth, jnp.zeros((1, tc), F32))
    scale = 1.0 / (asum * float(plan.n))

    _dft_stage1(kern_ref, f1_ref, a_ref, N1, plan)

    def stage2(k1, _):
        r0 = pl.multiple_of(k1 * 2 * N2, 2 * N2)
        slab = _load_slabs(a_ref, 2 * k1, 2, plan).astype(BF16)
        kf_ref[pl.ds(r0, 2 * N2), :] = (_dot(f2_ref[...], slab) * scale).astype(kf_ref.dtype)
        return 0

    lax.fori_loop(0, K1, stage2, 0, unroll=STAGE2_UNROLL)
    pad = plan.R - 2 * K1
    if pad:
        kf_ref[pl.ds(2 * K1 * N2, pad * N2), :] = jnp.zeros((pad * N2, tc), kf_ref.dtype)


def _hyena_filter_spectrum(L, w1, b1, fr1, w2, b2, fr2, w3, tc):
    plan = _FftPlan(L)
    C = w3.shape[1] // 2
    zz = _filter_features(L)
    f1, _, f2, _ = _fft_consts(L)
    w1p = jnp.zeros((LANES, FILTER_ORDER), BF16).at[:FILTER_EMB].set(w1.astype(BF16))
    ch = min(CONV_CHUNK, L)
    rows_a = plan.R * plan.N2
    cblock = lambda off: pl.BlockSpec((FILTER_ORDER, tc), lambda c: (0, off + c))
    vmem = (2 * L * LANES * 2 * 2 + 2 * L * tc * 4 + rows_a * tc * 4 + 2 * rows_a * tc * 2
            + f1.size * 2 * 2 + 8 * ch * tc * 4)
    return pl.pallas_call(
        functools.partial(_filter_kernel, plan=plan, ch=ch),
        grid=(C // tc,),
        in_specs=[_const_spec(zz.shape),
                  _const_spec(w1p.shape), _const_spec((1, FILTER_ORDER)), _const_spec((1, FILTER_ORDER)),
                  _const_spec((FILTER_ORDER, FILTER_ORDER)), _const_spec((1, FILTER_ORDER)),
                  _const_spec((1, FILTER_ORDER)),
                  cblock(0), cblock(C // tc),
                  pl.BlockSpec((1, tc), lambda c: (0, c)),
                  _const_spec(f1.shape), _const_spec(f2.shape)],
        out_specs=pl.BlockSpec((rows_a, tc), lambda c: (0, c)),
        out_shape=jax.ShapeDtypeStruct((rows_a, C), BF16),
        scratch_shapes=[pltpu.VMEM((2 * L, FILTER_ORDER), BF16),
                        pltpu.VMEM((tc // LANES, plan.N1 * plan.P, LANES), F32),
                        pltpu.VMEM((tc // LANES, plan.R * plan.P, LANES), F32)],
        compiler_params=_params(("arbitrary",), vmem),
        name="hyena_filter",
    )(zz, w1p, b1, fr1, w2.astype(BF16), b2, fr2, w3.astype(BF16), w3.astype(BF16),
      _decay_rates(C), f1, f2)


def _short_conv_chunk(raw_ref, w_ref, b_ref, r0, ch, L):
    halo = BF16_SUBLANES
    c = raw_ref[0, pl.ds(r0, ch), :].astype(F32)
    p0 = pl.multiple_of(jnp.maximum(r0 - halo, 0), halo)
    n0 = pl.multiple_of(jnp.minimum(r0 + ch, L - halo), halo)
    prev = raw_ref[0, pl.ds(p0, halo), :].astype(F32)[halo - 1:halo]
    nxt = raw_ref[0, pl.ds(n0, halo), :].astype(F32)[0:1]
    prev = jnp.where(r0 == 0, 0.0, prev)
    nxt = jnp.where(r0 + ch == L, 0.0, nxt)
    rows = lax.broadcasted_iota(jnp.int32, c.shape, 0)
    up = jnp.where(rows == 0, prev, pltpu.roll(c, 1, 0))
    un = jnp.where(rows == ch - 1, nxt, pltpu.roll(c, ch - 1, 0))
    w = w_ref[...]
    return w[0:1] * up + w[1:2] * c + w[2:3] * un + b_ref[...]


def _hyena_kernel(x0_ref, x1_ref, vh_ref, w0_ref, w1_ref, wv_ref, b0_ref, b1_ref, bv_ref,
                  d_ref, kf_ref, f1_ref, g1_ref, f2_ref, f2i_ref, o_ref,
                  u_ref, a_ref, y_ref, *, plan, ch):
    L, N1, N2, K1, R = plan.L, plan.N1, plan.N2, plan.K1, plan.R
    n_ch = L // ch

    def gate_in(i, _):
        r0 = pl.multiple_of(i * ch, ch)
        x1 = _short_conv_chunk(x1_ref, w1_ref, b1_ref, r0, ch, L)
        vh = _short_conv_chunk(vh_ref, wv_ref, bv_ref, r0, ch, L)
        _store_slabs(u_ref, i * (ch // N2), x1 * vh, plan)
        return 0

    lax.fori_loop(0, n_ch, gate_in, 0)

    _dft_stage1(u_ref, f1_ref, a_ref, N1 // 2, plan)

    def spectral(k1, _):
        r0 = pl.multiple_of(k1 * 2 * N2, 2 * N2)
        x = _dot(f2_ref[...], _load_slabs(a_ref, 2 * k1, 2, plan).astype(BF16))
        xr, xi = x[:N2], x[N2:]
        kr = kf_ref[pl.ds(r0, N2), :].astype(F32)
        ki = kf_ref[pl.ds(r0 + N2, N2), :].astype(F32)
        y = jnp.concatenate([xr * kr - xi * ki, xr * ki + xi * kr], axis=0).astype(BF16)
        _store_slabs(a_ref, 2 * k1, _dot(f2i_ref[...], y), plan)
        return 0

    lax.fori_loop(0, K1, spectral, 0, unroll=STAGE2_UNROLL)

    def inverse1(t2, _):
        rows = _load_rows(a_ref, pl.ds(t2, R, stride=plan.P)).astype(BF16)
        _store_rows(y_ref, pl.ds(t2, N1 // 2, stride=plan.P), _dot(g1_ref[t2], rows))
        return 0

    lax.fori_loop(0, N2, inverse1, 0, unroll=STAGE1_UNROLL)

    def gate_out(i, _):
        r0 = pl.multiple_of(i * ch, ch)
        x0 = _short_conv_chunk(x0_ref, w0_ref, b0_ref, r0, ch, L)
        u = _load_slabs(u_ref, i * (ch // N2), ch // N2, plan)
        y = _load_slabs(y_ref, i * (ch // N2), ch // N2, plan)
        o_ref[0, pl.ds(r0, ch), :] = ((y + d_ref[...] * u) * x0).astype(o_ref.dtype)
        return 0

    lax.fori_loop(0, n_ch, gate_out, 0)


def _hyena_vmem_bytes(plan, tc, ch):
    L, rows_a = plan.L, plan.R * plan.N2
    pad_l = lambda m: _round_up(m, LANES)
    return (6 * L * tc * 2 + rows_a * tc * 2 + 2 * L * tc * 4 + rows_a * tc * 4 + 2 * L * tc * 2
            + plan.N2 * plan.R * pad_l(plan.N1 // 2) * 2 + plan.N2 * (plan.N1 // 2) * pad_l(plan.R) * 2
            + 16 * ch * tc * 4 + 16 * plan.N2 * tc * 4)


def _hyena_operator(hy, conv_w, conv_b, d, kf):
    B, L, W3 = hy.shape
    C = W3 // 3
    plan = _FftPlan(L)
    ch = min(CONV_CHUNK, L)
    tc = 2 * LANES if _vmem_limit(_hyena_vmem_bytes(plan, 2 * LANES, ch)) <= VMEM_LIMIT_CAP else LANES
    nb = C // tc
    f1full, g1, f2, f2i = _fft_consts(L)
    f1 = f1full[:, :, :plan.N1 // 2]
    rows_a = plan.R * plan.N2
    seq = lambda part: pl.BlockSpec((1, L, tc), lambda c, b: (b, 0, part * nb + c))
    wspec = lambda part: pl.BlockSpec((SHORT_CONV, tc), lambda c, b: (0, part * nb + c))
    bspec = lambda part: pl.BlockSpec((1, tc), lambda c, b: (0, part * nb + c))
    vmem = _hyena_vmem_bytes(plan, tc, ch)
    return pl.pallas_call(
        functools.partial(_hyena_kernel, plan=plan, ch=ch),
        grid=(nb, B),
        in_specs=[seq(0), seq(1), seq(2), wspec(0), wspec(1), wspec(2), bspec(0), bspec(1), bspec(2),
                  pl.BlockSpec((1, tc), lambda c, b: (0, c)),
                  pl.BlockSpec((rows_a, tc), lambda c, b: (0, c), pipeline_mode=pl.Buffered(1)),
                  _const_spec(f1.shape), _const_spec(g1.shape), _const_spec(f2.shape), _const_spec(f2i.shape)],
        out_specs=pl.BlockSpec((1, L, tc), lambda c, b: (b, 0, c)),
        out_shape=jax.ShapeDtypeStruct((B, L, C), BF16),
        scratch_shapes=[pltpu.VMEM((tc // LANES, (plan.N1 // 2) * plan.P, LANES), F32),
                        pltpu.VMEM((tc // LANES, plan.R * plan.P, LANES), F32),
                        pltpu.VMEM((tc // LANES, (plan.N1 // 2) * plan.P, LANES), F32)],
        compiler_params=_params(("parallel", "parallel"), vmem),
        name="hyena_op",
    )(hy, hy, hy, conv_w, conv_w, conv_w, conv_b, conv_b, conv_b, d, kf, f1, g1, f2, f2i)


def _memkv_kernel(m_ref, g_ref, wk_ref, wv_ref, k_ref, v_ref):
    mn = _rms(m_ref[...], g_ref[...]).astype(BF16)
    k_ref[...] = _dot(mn, wk_ref[...]).astype(BF16)
    v_ref[...] = _dot(mn, wv_ref[...]).astype(BF16)


def _memkv(mem, g, wk, wv, tm):
    T, D = mem.shape
    row = pl.BlockSpec((tm, D), lambda i: (i, 0))
    vmem = 2 * tm * D * 4 + 2 * D * D * 2 + 4 * tm * D * 2 + 3 * tm * D * 4
    return pl.pallas_call(
        _memkv_kernel,
        grid=(T // tm,),
        in_specs=[row, _const_spec((1, D)), _const_spec((D, D)), _const_spec((D, D))],
        out_specs=[row, row],
        out_shape=[jax.ShapeDtypeStruct((T, D), BF16)] * 2,
        compiler_params=_params(("parallel",), vmem),
        name="mem_kv",
    )(mem, g, wk, wv)


def _mixout_cross_kernel(a_ref, y_ref, x_ref, wa_ref, wy_ref, gmix_ref, k_ref, v_ref,
                         gpre_ref, wq_ref, wo_ref, gpost_ref, o_ref, *, q_scale):
    z = _dot(a_ref[...], wa_ref[...]) + _dot(y_ref[...], wy_ref[...])
    x = x_ref[...] + _rms(z, gmix_ref[...])
    xn = _rms(x, gpre_ref[...]).astype(BF16)
    q = (_dot(xn, wq_ref[...]) * q_scale).astype(BF16)
    hd = q.shape[1] // X_HEADS
    outs = []
    for h in range(X_HEADS):
        sl = slice(h * hd, (h + 1) * hd)
        s = _dot_nt(q[:, sl], k_ref[0, :, sl])
        p = jnp.exp2(s - jnp.max(s, axis=-1, keepdims=True))
        l = jnp.sum(p, axis=-1, keepdims=True)
        outs.append((_dot(p.astype(BF16), v_ref[0, :, sl]) / l).astype(BF16))
    z = _dot(jnp.concatenate(outs, axis=1), wo_ref[...])
    o_ref[...] = x + _rms(z, gpost_ref[...])


def _mixout_cross_block(attn, y, x, w_out, gmix, k, v, gpre, wq, wo, gpost, L, tm):
    T, D = x.shape
    a, c = attn.shape[1], y.shape[1]
    n_mem = k.shape[1]
    blocks_per_seq = L // tm
    row = lambda width: pl.BlockSpec((tm, width), lambda i: (i, 0))
    kv = pl.BlockSpec((1, n_mem, D), lambda i: (i // blocks_per_seq, 0, 0))
    vmem = (2 * tm * (a + c) * 2 + 4 * tm * D * 4 + 4 * n_mem * D * 2 + (a + c + 2 * D) * D * 2
            + 8 * tm * D * 4)
    q_scale = (D // X_HEADS) ** -0.5 * LOG2E
    return pl.pallas_call(
        functools.partial(_mixout_cross_kernel, q_scale=q_scale),
        grid=(T // tm,),
        in_specs=[row(a), row(c), row(D), _const_spec((a, D)), _const_spec((c, D)), _const_spec((1, D)),
                  kv, kv, _const_spec((1, D)), _const_spec((D, D)), _const_spec((D, D)), _const_spec((1, D))],
        out_specs=row(D),
        out_shape=jax.ShapeDtypeStruct((T, D), F32),
        compiler_params=_params(("parallel",), vmem),
        name="mixout_cross",
    )(attn, y, x, w_out[:a], w_out[a:], gmix, k, v, gpre, wq, wo, gpost)


def _swiglu_kernel(x_ref, gpre_ref, wg_ref, wu_ref, wd_ref, gpost_ref, o_ref, *, n_split):
    x = x_ref[...]
    xn = _rms(x, gpre_ref[...]).astype(BF16)
    ff = wg_ref.shape[1]
    cw = ff // n_split
    z = None
    for c in range(n_split):
        sl = slice(c * cw, (c + 1) * cw)
        gate = _dot(xn, wg_ref[:, sl])
        up = _dot(xn, wu_ref[:, sl])
        h = (gate * (1.0 / (1.0 + jnp.exp(-gate))) * up).astype(BF16)
        part = _dot(h, wd_ref[sl, :])
        z = part if z is None else z + part
    o_ref[...] = x + _rms(z, gpost_ref[...])


def _swiglu_block(x, gpre, wg, wu, wd, gpost, tm):
    T, D = x.shape
    ff = wg.shape[1]
    n_split = ff // MXU_WIDTH if ff % MXU_WIDTH == 0 else 1
    row = pl.BlockSpec((tm, D), lambda i: (i, 0))
    vmem = 4 * tm * D * 4 + 3 * D * ff * 2 + 4 * tm * (ff // n_split) * 4 + 4 * tm * D * 4
    return pl.pallas_call(
        functools.partial(_swiglu_kernel, n_split=n_split),
        grid=(T // tm,),
        in_specs=[row, _const_spec((1, D)), _const_spec((D, ff)), _const_spec((D, ff)),
                  _const_spec((ff, D)), _const_spec((1, D))],
        out_specs=row,
        out_shape=jax.ShapeDtypeStruct((T, D), F32),
        compiler_params=_params(("parallel",), vmem),
        name="swiglu",
    )(x, gpre, wg, wu, wd, gpost)


def _trunk(x, mem, P):
    B, L, D = x.shape
    depth = P['w_in'].shape[0]
    n_mem = mem.shape[1]
    tm = min(ROW_TILE, L)
    tq, tk = tm, min(KEY_CHUNK, L)
    tc = LANES
    a = DA_HEADS * 2 * DA_HEAD_DIM
    tables = _rope_tables(L)
    xf = x.reshape(B * L, D)
    memf = mem.reshape(B * n_mem, D)
    row = lambda v: v.reshape(1, -1)
    for l in range(depth):
        lam_init = 0.8 - 0.6 * math.exp(-0.3 * l)
        qt, k, vt, hy = _inproj(xf, row(P['ln_mix_pre'][l]), P['w_in'][l].astype(BF16), tables, B, L, tm, tk)
        lam_params = jnp.stack([P['lambda_q1'][l], P['lambda_k1'][l], P['lambda_q2'][l], P['lambda_k2'][l]])
        attn = _diff_attention(lam_params, qt, k.reshape(B, L, a), vt, P['subln_g'][l], lam_init, tq, tk)
        kf = _hyena_filter_spectrum(L, P['filt_w1'][l], row(P['filt_b1'][l]), row(P['filt_freq1'][l]),
                                    P['filt_w2'][l], row(P['filt_b2'][l]), row(P['filt_freq2'][l]),
                                    P['filt_w3'][l], tc)
        y = _hyena_operator(hy.reshape(B, L, -1), P['conv_w'][l], row(P['conv_b'][l]),
                            row(P['hyena_d'][l]), kf)
        km, vm = _memkv(memf, row(P['ln_mem'][l]), P['wk_x'][l].astype(BF16), P['wv_x'][l].astype(BF16),
                        min(ROW_TILE, B * n_mem))
        xf = _mixout_cross_block(attn.reshape(B * L, a), y.reshape(B * L, -1), xf, P['w_out'][l].astype(BF16),
                                 row(P['ln_mix_post'][l]), km.reshape(B, n_mem, D), vm.reshape(B, n_mem, D),
                                 row(P['ln_x_pre'][l]), P['wq_x'][l].astype(BF16), P['wo_x'][l].astype(BF16),
                                 row(P['ln_x_post'][l]), L, tm)
        xf = _swiglu_block(xf, row(P['ln_ffn_pre'][l]), P['w_gate'][l].astype(BF16), P['w_up'][l].astype(BF16),
                           P['w_down'][l].astype(BF16), row(P['ln_ffn_post'][l]), tm)
    return xf.reshape(B, L, D)


def kernel(x_prompt, x_sample, mem_prompt, mem_sample, ln_mix_pre, ln_mix_post, w_in, lambda_q1, lambda_k1, lambda_q2, lambda_k2, subln_g, conv_w, conv_b, filt_w1, filt_b1, filt_freq1, filt_w2, filt_b2, filt_freq2, filt_w3, hyena_d, w_out, ln_x_pre, ln_x_post, ln_mem, wq_x, wk_x, wv_x, wo_x, ln_ffn_pre, ln_ffn_post, w_gate, w_up, w_down):
    P = dict(ln_mix_pre=ln_mix_pre, ln_mix_post=ln_mix_post, w_in=w_in,
             lambda_q1=lambda_q1, lambda_k1=lambda_k1, lambda_q2=lambda_q2, lambda_k2=lambda_k2,
             subln_g=subln_g, conv_w=conv_w, conv_b=conv_b,
             filt_w1=filt_w1, filt_b1=filt_b1, filt_freq1=filt_freq1,
             filt_w2=filt_w2, filt_b2=filt_b2, filt_freq2=filt_freq2, filt_w3=filt_w3,
             hyena_d=hyena_d, w_out=w_out,
             ln_x_pre=ln_x_pre, ln_x_post=ln_x_post, ln_mem=ln_mem,
             wq_x=wq_x, wk_x=wk_x, wv_x=wv_x, wo_x=wo_x,
             ln_ffn_pre=ln_ffn_pre, ln_ffn_post=ln_ffn_post,
             w_gate=w_gate, w_up=w_up, w_down=w_down)
    return (_trunk(x_prompt, mem_prompt, P), _trunk(x_sample, mem_sample, P))
```

```python
import functools
import math

import numpy as np
import jax
import jax.numpy as jnp
from jax import lax
from jax.experimental import pallas as pl
from jax.experimental.pallas import tpu as pltpu

F32 = jnp.float32
BF16 = jnp.bfloat16

DA_HEADS = 4
DA_HEAD_DIM = 64
DA_V_DIM = 128
ROPE_DIM = 16
ROPE_THETA = 500000.0
SHORT_CONV = 3
FILTER_EMB = 33
FILTER_ORDER = 64
FAST_DECAY_PCT = 0.3
SLOW_DECAY_PCT = 1.5
DECAY_TARGET = 1e-2
X_HEADS = 4
EPS = 1e-6
LOG2E = 1.4426950408889634

LANES = 128
SUBLANES = 8
BF16_SUBLANES = 16
MXU_WIDTH = 256
VMEM_LIMIT_CAP = 58 * 2**20
VMEM_TEMP_FACTOR = 1.25
VMEM_TEMP_BYTES = 4 * 2**20

ROW_TILE = 512
MIXOUT_ROW_TILE = 1024
KEY_CHUNK = 1024
KEY_BLOCK = 512
CONV_CHUNK = 512
ATTN_PASSES_PER_STEP = 16

STAGE1_UNROLL = 32
STAGE2_UNROLL = 16


def _round_up(a, b):
    return -(-a // b) * b


def _vmem_limit(vmem_bytes):
    return int(vmem_bytes * VMEM_TEMP_FACTOR) + VMEM_TEMP_BYTES


def _params(semantics, vmem_bytes):
    limit = min(_vmem_limit(vmem_bytes), VMEM_LIMIT_CAP)
    return pltpu.CompilerParams(dimension_semantics=semantics, vmem_limit_bytes=limit)


def _const_spec(shape):
    nd = len(shape)
    return pl.BlockSpec(shape, lambda *_: (0,) * nd, pipeline_mode=pl.Buffered(1))


def _rms(x, g):
    ms = jnp.mean(x * x, axis=-1, keepdims=True)
    return x * lax.rsqrt(ms + EPS) * g


def _dot(a, b):
    return jnp.dot(a, b, preferred_element_type=F32)


def _dot_nt(a, b):
    return lax.dot_general(a, b, (((1,), (1,)), ((), ())), preferred_element_type=F32)


def _rope_tables(L):
    inv = ROPE_THETA ** (-np.arange(0, ROPE_DIM, 2, dtype=np.float64) / ROPE_DIM)
    ang = np.arange(L, dtype=np.float64)[:, None] * inv[None, :]
    half = ROPE_DIM // 2
    cos = np.ones((L, LANES)); sa = np.zeros((L, LANES)); sb = np.zeros((L, LANES))
    for g in range(LANES // DA_HEAD_DIM):
        o = g * DA_HEAD_DIM
        cos[:, o:o + half] = np.cos(ang)
        cos[:, o + half:o + ROPE_DIM] = np.cos(ang)
        sb[:, o:o + half] = -np.sin(ang)
        sa[:, o + half:o + ROPE_DIM] = np.sin(ang)
    return (jnp.asarray(cos, F32), jnp.asarray(sa, F32), jnp.asarray(sb, F32),
            jnp.asarray(np.cos(ang).T, F32), jnp.asarray(np.sin(ang).T, F32))


def _inproj_kernel(x_ref, g_ref, w_ref, wqt_ref, wvt_ref, cos_ref, sa_ref, sb_ref, cost_ref, sint_ref,
                   qt_ref, k_ref, vt_ref, hy_ref, *, q_scale):
    xn = _rms(x_ref[...], g_ref[...]).astype(BF16)
    cos, sa, sb = cos_ref[...], sa_ref[...], sb_ref[...]
    half = ROPE_DIM // 2

    def rope(y):
        parts = []
        for s in range(y.shape[1] // LANES):
            ys = y[:, s * LANES:(s + 1) * LANES]
            parts.append(ys * cos + pltpu.roll(ys, half, 1) * sa
                         + pltpu.roll(ys, LANES - half, 1) * sb)
        return jnp.concatenate(parts, axis=1)

    def rope_t(y):
        ct, st = cost_ref[...], sint_ref[...]
        parts = []
        for o in range(0, y.shape[0], DA_HEAD_DIM):
            x1, x2 = y[o:o + half], y[o + half:o + ROPE_DIM]
            parts += [x1 * ct - x2 * st, x2 * ct + x1 * st, y[o + ROPE_DIM:o + DA_HEAD_DIM]]
        return jnp.concatenate(parts, axis=0)

    a = k_ref.shape[1]
    qt_ref[0, 0] = (rope_t(_dot_nt(wqt_ref[...], xn)) * q_scale).astype(BF16)
    k_ref[...] = rope(_dot(xn, w_ref[:, a:2 * a])).astype(BF16)
    vt_ref[0, 0] = _dot_nt(wvt_ref[...], xn).astype(BF16)
    hy_ref[...] = _dot(xn, w_ref[:, 3 * a:]).astype(BF16)


def _inproj(x, g, w, tables, B, L, tm, tk):
    T, D = x.shape
    n_out = w.shape[1]
    a = DA_HEADS * 2 * DA_HEAD_DIM
    hyw = n_out - 3 * a
    blocks_per_seq = L // tm
    tiles_per_chunk = tk // tm
    tab_spec = pl.BlockSpec((tm, LANES), lambda i: (i % blocks_per_seq, 0))
    tabt_spec = pl.BlockSpec((ROPE_DIM // 2, tm), lambda i: (0, i % blocks_per_seq))
    row = lambda width: pl.BlockSpec((tm, width), lambda i: (i, 0))
    qt_spec = pl.BlockSpec((1, 1, a, tm), lambda i: (i // blocks_per_seq, i % blocks_per_seq, 0, 0))
    vt_spec = pl.BlockSpec((1, 1, a, tm), lambda i: (i // blocks_per_seq, (i % blocks_per_seq) // tiles_per_chunk,
                                                     0, i % tiles_per_chunk))
    vmem = 2 * tm * D * 4 + D * (n_out + 2 * a) * 2 + 8 * tm * LANES * 4 + 2 * tm * n_out * 2 + 4 * tm * 1536 * 4
    q_scale = DA_HEAD_DIM ** -0.5 * LOG2E
    wqt, wvt = w[:, 0:a].T, w[:, 2 * a:3 * a].T
    return pl.pallas_call(
        functools.partial(_inproj_kernel, q_scale=q_scale),
        grid=(T // tm,),
        in_specs=[row(D), _const_spec((1, D)), _const_spec((D, n_out)), _const_spec((a, D)), _const_spec((a, D)),
                  tab_spec, tab_spec, tab_spec, tabt_spec, tabt_spec],
        out_specs=[qt_spec, row(a), vt_spec, row(hyw)],
        out_shape=[jax.ShapeDtypeStruct((B, L // tm, a, tm), BF16), jax.ShapeDtypeStruct((T, a), BF16),
                   jax.ShapeDtypeStruct((B, L // tk, a, tk), BF16), jax.ShapeDtypeStruct((T, hyw), BF16)],
        compiler_params=_params(("parallel",), vmem),
        name="inproj",
    )(x, g, w, wqt, wvt, *tables)


def _attn_kernel(lam_ref, qt_ref, k_ref, vt_ref, g_ref, o_ref, s_ref, *, tk, kb, lam_init):
    nt, _, tq = qt_ref.shape[1:]
    L = k_ref.shape[1]
    feat = lax.broadcasted_iota(jnp.int32, (2 * DA_HEAD_DIM, tq), 0)

    def split(qt):
        zero = jnp.zeros_like(qt)
        return (jnp.where(feat < DA_HEAD_DIM, qt, zero), jnp.where(feat >= DA_HEAD_DIM, qt, zero))

    qz = [split(qt_ref[0, t]) for t in range(nt)]

    lp = lam_ref[...]
    lam = (jnp.exp(jnp.sum(lp[0:1] * lp[1:2], axis=-1, keepdims=True))
           - jnp.exp(jnp.sum(lp[2:3] * lp[3:4], axis=-1, keepdims=True)) + lam_init)
    g = jnp.concatenate([g_ref[...]] * (tq // LANES), axis=1)

    nk, nb = L // tk, tk // kb
    neg = jnp.full((1, tq), -1e30, F32)
    fresh = (neg, jnp.zeros((1, tq), F32), jnp.zeros((DA_V_DIM, tq), F32))

    def score_block(t, j, c, b, mx):
        s = _dot(k_ref[0, j * tk + b * kb:j * tk + (b + 1) * kb, :], qz[t][c])
        s_ref[c, b * kb:(b + 1) * kb, :] = s
        return jnp.maximum(mx, jnp.max(s, axis=0, keepdims=True))

    def value_block(j, c, b, mn, lsum, pv):
        p = jnp.exp2(s_ref[c, b * kb:(b + 1) * kb, :] - mn)
        lsum = lsum + jnp.sum(p, axis=0, keepdims=True)
        pv = pv + _dot(vt_ref[0, j, :, b * kb:(b + 1) * kb], p.astype(BF16))
        return lsum, pv

    def step(score_of, value_of, mx_cur, state):
        m, l, acc = state
        mn = jnp.maximum(m, mx_cur)
        alpha = jnp.exp2(m - mn)
        lsum, pv, mx = jnp.zeros((1, tq), F32), jnp.zeros((DA_V_DIM, tq), F32), neg
        for b in range(nb):
            if score_of is not None:
                mx = score_block(*score_of, b, mx)
            lsum, pv = value_block(*value_of[1:], b, mn, lsum, pv)
        return mx, (mn, alpha * l + lsum, alpha * acc + pv)

    def finish(t, st0, st1):
        (_, l0, a0), (_, l1, a1) = st0, st1
        o = a0 / l0 - lam * (a1 / l1)
        ms = jnp.mean(o * o, axis=0, keepdims=True)
        o = o * lax.rsqrt(ms + EPS) * g * (1.0 - lam_init)
        o_ref[0, t * tq:(t + 1) * tq, :] = o.T.astype(BF16)

    passes = [(t, j, c) for t in range(nt) for j in range(nk) for c in range(2)]
    mx = neg
    for b in range(nb):
        mx = score_block(*passes[0], b, mx)
    states = {}
    for n, cur in enumerate(passes):
        t, j, c = cur
        nxt = passes[n + 1] if n + 1 < len(passes) else None
        mx, states[(t, c)] = step(nxt, cur, mx, states.get((t, c), fresh))
        if j == nk - 1 and c == 1:
            finish(t, states.pop((t, 0)), states.pop((t, 1)))


def _diff_attention(lam_params, qt, k, vt, g, lam_init, tq, tk):
    B, L, A = k.shape
    nt = max(1, min(L // tq, ATTN_PASSES_PER_STEP // (2 * (L // tk))))
    qspec = pl.BlockSpec((1, nt, LANES, tq), lambda b, h, i: (b, i, h, 0))
    ospec = pl.BlockSpec((1, nt * tq, LANES), lambda b, h, i: (b, i, h))
    kspec = pl.BlockSpec((1, L, LANES), lambda b, h, i: (b, 0, h))
    vspec = pl.BlockSpec((1, L // tk, LANES, tk), lambda b, h, i: (b, 0, h, 0))
    g_cols = jnp.broadcast_to(g.reshape(DA_V_DIM, 1), (DA_V_DIM, LANES))
    kb = min(KEY_BLOCK, tk)
    vmem = (4 * L * LANES * 2 + 4 * nt * tq * LANES * 2 + 2 * tq * tk * 4 + 4 * tq * kb * 6
            + 16 * tq * LANES * 4)
    return pl.pallas_call(
        functools.partial(_attn_kernel, tk=tk, kb=kb, lam_init=lam_init),
        grid=(B, DA_HEADS, L // (nt * tq)),
        in_specs=[_const_spec(lam_params.shape), qspec, kspec, vspec, _const_spec((DA_V_DIM, LANES))],
        out_specs=ospec,
        out_shape=jax.ShapeDtypeStruct((B, L, A), BF16),
        scratch_shapes=[pltpu.VMEM((2, tk, tq), F32)],
        compiler_params=_params(("parallel", "parallel", "parallel"), vmem),
        name="diff_attn",
    )(lam_params, qt, k, vt, g_cols)


class _FftPlan:
    def __init__(self, L):
        n = 2 * L
        lg = int(math.log2(n))
        assert 2 ** lg == n
        self.L, self.n = L, n
        self.N1 = 2 ** (lg // 2)
        self.N2 = n // self.N1
        self.K1 = self.N1 // 2 + 1
        self.R = _round_up(2 * self.K1, BF16_SUBLANES)
        self.P = self.N2 + SUBLANES
        assert (self.P // SUBLANES) % 2 == 1
        assert self.N2 % BF16_SUBLANES == 0 and (self.N1 // 2) % BF16_SUBLANES == 0


@functools.lru_cache(maxsize=None)
def _fft_consts(L):
    p = _FftPlan(L)
    N1, N2, K1, R, n = p.N1, p.N2, p.K1, p.R, p.n
    t2 = np.arange(N2, dtype=np.float64)[:, None, None]
    k1 = np.arange(K1, dtype=np.float64)[None, :, None]
    t1 = np.arange(N1, dtype=np.float64)[None, None, :]
    ang = -2.0 * np.pi * (t1 * k1 / N1 + t2 * k1 / n)
    f1 = np.zeros((N2, R, N1))
    f1[:, 0:2 * K1:2, :] = np.cos(ang)
    f1[:, 1:2 * K1:2, :] = np.sin(ang)
    c = np.where((np.arange(K1) == 0) | (np.arange(K1) == N1 // 2), 1.0, 2.0)[None, :, None]
    th = -ang[:, :, :N1 // 2]
    g1 = np.zeros((N2, N1 // 2, R))
    g1[:, :, 0:2 * K1:2] = np.transpose(c * np.cos(th), (0, 2, 1))
    g1[:, :, 1:2 * K1:2] = np.transpose(-c * np.sin(th), (0, 2, 1))
    a2 = 2.0 * np.pi * np.outer(np.arange(N2), np.arange(N2)) / N2
    C, S = np.cos(a2), np.sin(a2)
    f2 = np.block([[C, S], [-S, C]])
    return (jnp.asarray(f1, BF16), jnp.asarray(g1, BF16), jnp.asarray(f2, BF16), jnp.asarray(f2.T, BF16))


def _load_rows(ref, idx):
    parts = [ref[g, idx, :] for g in range(ref.shape[0])]
    return parts[0] if len(parts) == 1 else jnp.concatenate(parts, axis=1)


def _store_rows(ref, idx, val):
    for g in range(ref.shape[0]):
        ref[g, idx, :] = val[:, g * LANES:(g + 1) * LANES]


def _store_slabs(ref, first_slab, val, plan):
    for s in range(val.shape[0] // plan.N2):
        start = pl.multiple_of((first_slab + s) * plan.P, SUBLANES)
        _store_rows(ref, pl.ds(start, plan.N2), val[s * plan.N2:(s + 1) * plan.N2])


def _load_slabs(ref, first_slab, count, plan):
    parts = [_load_rows(ref, pl.ds(pl.multiple_of((first_slab + s) * plan.P, SUBLANES), plan.N2))
             for s in range(count)]
    return parts[0] if count == 1 else jnp.concatenate(parts, axis=0)


def _dft_stage1(src_ref, f1_ref, a_ref, n_slabs, plan):
    def body(t2, _):
        rows = _load_rows(src_ref, pl.ds(t2, n_slabs, stride=plan.P)).astype(BF16)
        _store_rows(a_ref, pl.ds(t2, plan.R, stride=plan.P), _dot(f1_ref[t2], rows))
        return 0

    lax.fori_loop(0, plan.N2, body, 0, unroll=STAGE1_UNROLL)


@functools.lru_cache(maxsize=None)
def _filter_features(L):
    t = np.linspace(0.0, 1.0, L)[:, None]
    bands = (FILTER_EMB - 1) // 2
    w = 2.0 * np.pi * np.arange(L)[:, None] / L
    f = np.linspace(1e-4, bands - 1, bands)[None, :]
    z = np.concatenate([t, np.cos(f * w), -np.sin(f * w)], axis=-1)
    z_rev = np.concatenate([z[:1], z[:0:-1]], axis=0)
    zz = np.zeros((2 * L, LANES))
    zz[:L, :FILTER_EMB] = z
    zz[L:, :FILTER_EMB] = z_rev
    return jnp.asarray(zz, BF16)


def _decay_rates(width):
    max_decay = math.log(DECAY_TARGET) / FAST_DECAY_PCT
    min_decay = math.log(DECAY_TARGET) / SLOW_DECAY_PCT
    return jnp.asarray(np.abs(np.linspace(min_decay, max_decay, width))[None, :], F32)


def _filter_kernel(z_ref, w1_ref, b1_ref, fr1_ref, w2_ref, b2_ref, fr2_ref,
                   w3f_ref, w3b_ref, dl_ref, f1_ref, f2_ref, kf_ref,
                   h_ref, kern_ref, a_ref, *, plan, ch):
    L, N1, N2, K1 = plan.L, plan.N1, plan.N2, plan.K1
    n_ch = (2 * L) // ch

    @pl.when(pl.program_id(0) == 0)
    def _():
        def mlp(i, _):
            r0 = pl.multiple_of(i * ch, ch)
            h = jnp.sin(fr1_ref[...] * (_dot(z_ref[pl.ds(r0, ch), :], w1_ref[...]) + b1_ref[...]))
            h = jnp.sin(fr2_ref[...] * (_dot(h.astype(BF16), w2_ref[...]) + b2_ref[...]))
            h_ref[pl.ds(r0, ch), :] = h.astype(BF16)
            return 0
        lax.fori_loop(0, n_ch, mlp, 0)

    tc = kf_ref.shape[1]
    local = lax.broadcasted_iota(jnp.int32, (ch, tc), 0)

    def synth(i, asum):
        r0 = pl.multiple_of(i * ch, ch)
        row = local + r0
        h = h_ref[pl.ds(r0, ch), :]
        fwd = _dot(h, w3f_ref[...])
        bwd = _dot(h, w3b_ref[...])
        lag = jnp.where(row < L, row, 2 * L - row).astype(F32)
        win = jnp.exp(lag * (-1.0 / (L - 1)) * dl_ref[...])
        val = jnp.where(row < L, fwd + jnp.where(row == 0, bwd, 0.0), bwd) * win
        val = jnp.where(row == L, 0.0, val)
        _store_slabs(kern_ref, i * (ch // N2), val, plan)
        return asum + jnp.sum(jnp.abs(val), axis=0, keepdims=True)

    asum = lax.fori_loop(0, n_ch, synth, jnp.zeros((1, tc), F32))
    scale = 1.0 / (asum * float(plan.n))

    _dft_stage1(kern_ref, f1_ref, a_ref, N1, plan)

    def stage2(k1, _):
        r0 = pl.multiple_of(k1 * 2 * N2, 2 * N2)
        slab = _load_slabs(a_ref, 2 * k1, 2, plan).astype(BF16)
        kf_ref[pl.ds(r0, 2 * N2), :] = (_dot(f2_ref[...], slab) * scale).astype(kf_ref.dtype)
        return 0

    lax.fori_loop(0, K1, stage2, 0, unroll=STAGE2_UNROLL)
    pad = plan.R - 2 * K1
    if pad:
        kf_ref[pl.ds(2 * K1 * N2, pad * N2), :] = jnp.zeros((pad * N2, tc), kf_ref.dtype)


def _hyena_filter_spectrum(L, w1, b1, fr1, w2, b2, fr2, w3, tc):
    plan = _FftPlan(L)
    C = w3.shape[1] // 2
    zz = _filter_features(L)
    f1, _, f2, _ = _fft_consts(L)
    w1p = jnp.zeros((LANES, FILTER_ORDER), BF16).at[:FILTER_EMB].set(w1.astype(BF16))
    ch = min(CONV_CHUNK, L)
    rows_a = plan.R * plan.N2
    cblock = lambda off: pl.BlockSpec((FILTER_ORDER, tc), lambda c: (0, off + c))
    vmem = (2 * L * LANES * 2 * 2 + 2 * L * tc * 4 + rows_a * tc * 4 + 2 * rows_a * tc * 2
            + f1.size * 2 * 2 + 8 * ch * tc * 4)
    return pl.pallas_call(
        functools.partial(_filter_kernel, plan=plan, ch=ch),
        grid=(C // tc,),
        in_specs=[_const_spec(zz.shape),
                  _const_spec(w1p.shape), _const_spec((1, FILTER_ORDER)), _const_spec((1, FILTER_ORDER)),
                  _const_spec((FILTER_ORDER, FILTER_ORDER)), _const_spec((1, FILTER_ORDER)),
                  _const_spec((1, FILTER_ORDER)),
                  cblock(0), cblock(C // tc),
                  pl.BlockSpec((1, tc), lambda c: (0, c)),
                  _const_spec(f1.shape), _const_spec(f2.shape)],
        out_specs=pl.BlockSpec((rows_a, tc), lambda c: (0, c)),
        out_shape=jax.ShapeDtypeStruct((rows_a, C), BF16),
        scratch_shapes=[pltpu.VMEM((2 * L, FILTER_ORDER), BF16),
                        pltpu.VMEM((tc // LANES, plan.N1 * plan.P, LANES), F32),
                        pltpu.VMEM((tc // LANES, plan.R * plan.P, LANES), F32)],
        compiler_params=_params(("arbitrary",), vmem),
        name="hyena_filter",
    )(zz, w1p, b1, fr1, w2.astype(BF16), b2, fr2, w3.astype(BF16), w3.astype(BF16),
      _decay_rates(C), f1, f2)


def _short_conv_chunk(raw_ref, w_ref, b_ref, r0, ch, L):
    halo = BF16_SUBLANES
    c = raw_ref[0, pl.ds(r0, ch), :].astype(F32)
    p0 = pl.multiple_of(jnp.maximum(r0 - halo, 0), halo)
    n0 = pl.multiple_of(jnp.minimum(r0 + ch, L - halo), halo)
    prev = raw_ref[0, pl.ds(p0, halo), :].astype(F32)[halo - 1:halo]
    nxt = raw_ref[0, pl.ds(n0, halo), :].astype(F32)[0:1]
    prev = jnp.where(r0 == 0, 0.0, prev)
    nxt = jnp.where(r0 + ch == L, 0.0, nxt)
    rows = lax.broadcasted_iota(jnp.int32, c.shape, 0)
    up = jnp.where(rows == 0, prev, pltpu.roll(c, 1, 0))
    un = jnp.where(rows == ch - 1, nxt, pltpu.roll(c, ch - 1, 0))
    w = w_ref[...]
    return w[0:1] * up + w[1:2] * c + w[2:3] * un + b_ref[...]


def _hyena_kernel(x0_ref, x1_ref, vh_ref, w0_ref, w1_ref, wv_ref, b0_ref, b1_ref, bv_ref,
                  d_ref, kf_ref, f1_ref, g1_ref, f2_ref, f2i_ref, o_ref,
                  u_ref, a_ref, y_ref, *, plan, ch):
    L, N1, N2, K1, R = plan.L, plan.N1, plan.N2, plan.K1, plan.R
    n_ch = L // ch

    def gate_in(i, _):
        r0 = pl.multiple_of(i * ch, ch)
        x1 = _short_conv_chunk(x1_ref, w1_ref, b1_ref, r0, ch, L)
        vh = _short_conv_chunk(vh_ref, wv_ref, bv_ref, r0, ch, L)
        _store_slabs(u_ref, i * (ch // N2), x1 * vh, plan)
        return 0

    lax.fori_loop(0, n_ch, gate_in, 0)

    _dft_stage1(u_ref, f1_ref, a_ref, N1 // 2, plan)

    def spectral(k1, _):
        r0 = pl.multiple_of(k1 * 2 * N2, 2 * N2)
        x = _dot(f2_ref[...], _load_slabs(a_ref, 2 * k1, 2, plan).astype(BF16))
        xr, xi = x[:N2], x[N2:]
        kr = kf_ref[pl.ds(r0, N2), :].astype(F32)
        ki = kf_ref[pl.ds(r0 + N2, N2), :].astype(F32)
        y = jnp.concatenate([xr * kr - xi * ki, xr * ki + xi * kr], axis=0).astype(BF16)
        _store_slabs(a_ref, 2 * k1, _dot(f2i_ref[...], y), plan)
        return 0

    lax.fori_loop(0, K1, spectral, 0, unroll=STAGE2_UNROLL)

    def inverse1(t2, _):
        rows = _load_rows(a_ref, pl.ds(t2, R, stride=plan.P)).astype(BF16)
        _store_rows(y_ref, pl.ds(t2, N1 // 2, stride=plan.P), _dot(g1_ref[t2], rows))
        return 0

    lax.fori_loop(0, N2, inverse1, 0, unroll=STAGE1_UNROLL)

    def gate_out(i, _):
        r0 = pl.multiple_of(i * ch, ch)
        x0 = _short_conv_chunk(x0_ref, w0_ref, b0_ref, r0, ch, L)
        u = _load_slabs(u_ref, i * (ch // N2), ch // N2, plan)
        y = _load_slabs(y_ref, i * (ch // N2), ch // N2, plan)
        o_ref[0, pl.ds(r0, ch), :] = ((y + d_ref[...] * u) * x0).astype(o_ref.dtype)
        return 0

    lax.fori_loop(0, n_ch, gate_out, 0)


def _hyena_vmem_bytes(plan, tc, ch):
    L, rows_a = plan.L, plan.R * plan.N2
    pad_l = lambda m: _round_up(m, LANES)
    return (6 * L * tc * 2 + rows_a * tc * 2 + 2 * L * tc * 4 + rows_a * tc * 4 + 2 * L * tc * 2
            + plan.N2 * plan.R * pad_l(plan.N1 // 2) * 2 + plan.N2 * (plan.N1 // 2) * pad_l(plan.R) * 2
            + 16 * ch * tc * 4 + 16 * plan.N2 * tc * 4)


def _hyena_operator(hy, conv_w, conv_b, d, kf):
    B, L, W3 = hy.shape
    C = W3 // 3
    plan = _FftPlan(L)
    ch = min(CONV_CHUNK, L)
    tc = 2 * LANES if _vmem_limit(_hyena_vmem_bytes(plan, 2 * LANES, ch)) <= VMEM_LIMIT_CAP else LANES
    nb = C // tc
    f1full, g1, f2, f2i = _fft_consts(L)
    f1 = f1full[:, :, :plan.N1 // 2]
    rows_a = plan.R * plan.N2
    seq = lambda part: pl.BlockSpec((1, L, tc), lambda c, b: (b, 0, part * nb + c))
    wspec = lambda part: pl.BlockSpec((SHORT_CONV, tc), lambda c, b: (0, part * nb + c))
    bspec = lambda part: pl.BlockSpec((1, tc), lambda c, b: (0, part * nb + c))
    vmem = _hyena_vmem_bytes(plan, tc, ch)
    return pl.pallas_call(
        functools.partial(_hyena_kernel, plan=plan, ch=ch),
        grid=(nb, B),
        in_specs=[seq(0), seq(1), seq(2), wspec(0), wspec(1), wspec(2), bspec(0), bspec(1), bspec(2),
                  pl.BlockSpec((1, tc), lambda c, b: (0, c)),
                  pl.BlockSpec((rows_a, tc), lambda c, b: (0, c), pipeline_mode=pl.Buffered(1)),
                  _const_spec(f1.shape), _const_spec(g1.shape), _const_spec(f2.shape), _const_spec(f2i.shape)],
        out_specs=pl.BlockSpec((1, L, tc), lambda c, b: (b, 0, c)),
        out_shape=jax.ShapeDtypeStruct((B, L, C), BF16),
        scratch_shapes=[pltpu.VMEM((tc // LANES, (plan.N1 // 2) * plan.P, LANES), F32),
                        pltpu.VMEM((tc // LANES, plan.R * plan.P, LANES), F32),
                        pltpu.VMEM((tc // LANES, (plan.N1 // 2) * plan.P, LANES), F32)],
        compiler_params=_params(("parallel", "parallel"), vmem),
        name="hyena_op",
    )(hy, hy, hy, conv_w, conv_w, conv_w, conv_b, conv_b, conv_b, d, kf, f1, g1, f2, f2i)


def _memkv_kernel(m_ref, g_ref, wk_ref, wv_ref, k_ref, v_ref):
    mn = _rms(m_ref[...], g_ref[...]).astype(BF16)
    k_ref[...] = _dot(mn, wk_ref[...]).astype(BF16)
    v_ref[...] = _dot(mn, wv_ref[...]).astype(BF16)


def _memkv(mem, g, wk, wv, tm):
    T, D = mem.shape
    row = pl.BlockSpec((tm, D), lambda i: (i, 0))
    vmem = 2 * tm * D * 4 + 2 * D * D * 2 + 4 * tm * D * 2 + 3 * tm * D * 4
    return pl.pallas_call(
        _memkv_kernel,
        grid=(T // tm,),
        in_specs=[row, _const_spec((1, D)), _const_spec((D, D)), _const_spec((D, D))],
        out_specs=[row, row],
        out_shape=[jax.ShapeDtypeStruct((T, D), BF16)] * 2,
        compiler_params=_params(("parallel",), vmem),
        name="mem_kv",
    )(mem, g, wk, wv)


def _mixout_cross_kernel(a_ref, y_ref, x_ref, wa_ref, wy_ref, gmix_ref, k_ref, v_ref,
                         gpre_ref, wq_ref, wo_ref, gpost_ref, o_ref, *, q_scale):
    z = _dot(a_ref[...], wa_ref[...]) + _dot(y_ref[...], wy_ref[...])
    x = x_ref[...] + _rms(z, gmix_ref[...])
    xn = _rms(x, gpre_ref[...]).astype(BF16)
    q = (_dot(xn, wq_ref[...]) * q_scale).astype(BF16)
    hd = q.shape[1] // X_HEADS
    outs = []
    for h in range(X_HEADS):
        sl = slice(h * hd, (h + 1) * hd)
        s = _dot_nt(q[:, sl], k_ref[0, :, sl])
        p = jnp.exp2(s - jnp.max(s, axis=-1, keepdims=True))
        l = jnp.sum(p, axis=-1, keepdims=True)
        outs.append((_dot(p.astype(BF16), v_ref[0, :, sl]) / l).astype(BF16))
    z = _dot(jnp.concatenate(outs, axis=1), wo_ref[...])
    o_ref[...] = x + _rms(z, gpost_ref[...])


def _mixout_cross_block(attn, y, x, w_out, gmix, k, v, gpre, wq, wo, gpost, L, tm):
    T, D = x.shape
    a, c = attn.shape[1], y.shape[1]
    n_mem = k.shape[1]
    blocks_per_seq = L // tm
    row = lambda width: pl.BlockSpec((tm, width), lambda i: (i, 0))
    kv = pl.BlockSpec((1, n_mem, D), lambda i: (i // blocks_per_seq, 0, 0))
    vmem = (2 * tm * (a + c) * 2 + 4 * tm * D * 4 + 4 * n_mem * D * 2 + (a + c + 2 * D) * D * 2
            + 8 * tm * D * 4)
    q_scale = (D // X_HEADS) ** -0.5 * LOG2E
    return pl.pallas_call(
        functools.partial(_mixout_cross_kernel, q_scale=q_scale),
        grid=(T // tm,),
        in_specs=[row(a), row(c), row(D), _const_spec((a, D)), _const_spec((c, D)), _const_spec((1, D)),
                  kv, kv, _const_spec((1, D)), _const_spec((D, D)), _const_spec((D, D)), _const_spec((1, D))],
        out_specs=row(D),
        out_shape=jax.ShapeDtypeStruct((T, D), F32),
        compiler_params=_params(("parallel",), vmem),
        name="mixout_cross",
    )(attn, y, x, w_out[:a], w_out[a:], gmix, k, v, gpre, wq, wo, gpost)


def _swiglu_kernel(x_ref, gpre_ref, wg_ref, wu_ref, wd_ref, gpost_ref, o_ref, *, n_split):
    x = x_ref[...]
    xn = _rms(x, gpre_ref[...]).astype(BF16)
    ff = wg_ref.shape[1]
    cw = ff // n_split
    z = None
    for c in range(n_split):
        sl = slice(c * cw, (c + 1) * cw)
        gate = _dot(xn, wg_ref[:, sl])
        up = _dot(xn, wu_ref[:, sl])
        h = (gate * (1.0 / (1.0 + jnp.exp(-gate))) * up).astype(BF16)
        part = _dot(h, wd_ref[sl, :])
        z = part if z is None else z + part
    o_ref[...] = x + _rms(z, gpost_ref[...])


def _swiglu_block(x, gpre, wg, wu, wd, gpost, tm):
    T, D = x.shape
    ff = wg.shape[1]
    n_split = ff // MXU_WIDTH if ff % MXU_WIDTH == 0 else 1
    row = pl.BlockSpec((tm, D), lambda i: (i, 0))
    vmem = 4 * tm * D * 4 + 3 * D * ff * 2 + 4 * tm * (ff // n_split) * 4 + 4 * tm * D * 4
    return pl.pallas_call(
        functools.partial(_swiglu_kernel, n_split=n_split),
        grid=(T // tm,),
        in_specs=[row, _const_spec((1, D)), _const_spec((D, ff)), _const_spec((D, ff)),
                  _const_spec((ff, D)), _const_spec((1, D))],
        out_specs=row,
        out_shape=jax.ShapeDtypeStruct((T, D), F32),
        compiler_params=_params(("parallel",), vmem),
        name="swiglu",
    )(x, gpre, wg, wu, wd, gpost)


def _trunk(x, mem, P):
    B, L, D = x.shape
    depth = P['w_in'].shape[0]
    n_mem = mem.shape[1]
    tm = min(ROW_TILE, L)
    tq, tk = tm, min(KEY_CHUNK, L)
    tc = LANES
    a = DA_HEADS * 2 * DA_HEAD_DIM
    tables = _rope_tables(L)
    xf = x.reshape(B * L, D)
    memf = mem.reshape(B * n_mem, D)
    row = lambda v: v.reshape(1, -1)
    for l in range(depth):
        lam_init = 0.8 - 0.6 * math.exp(-0.3 * l)
        qt, k, vt, hy = _inproj(xf, row(P['ln_mix_pre'][l]), P['w_in'][l].astype(BF16), tables, B, L, tm, tk)
        lam_params = jnp.stack([P['lambda_q1'][l], P['lambda_k1'][l], P['lambda_q2'][l], P['lambda_k2'][l]])
        attn = _diff_attention(lam_params, qt, k.reshape(B, L, a), vt, P['subln_g'][l], lam_init, tq, tk)
        kf = _hyena_filter_spectrum(L, P['filt_w1'][l], row(P['filt_b1'][l]), row(P['filt_freq1'][l]),
                                    P['filt_w2'][l], row(P['filt_b2'][l]), row(P['filt_freq2'][l]),
                                    P['filt_w3'][l], tc)
        y = _hyena_operator(hy.reshape(B, L, -1), P['conv_w'][l], row(P['conv_b'][l]),
                            row(P['hyena_d'][l]), kf)
        km, vm = _memkv(memf, row(P['ln_mem'][l]), P['wk_x'][l].astype(BF16), P['wv_x'][l].astype(BF16),
                        min(ROW_TILE, B * n_mem))
        xf = _mixout_cross_block(attn.reshape(B * L, a), y.reshape(B * L, -1), xf, P['w_out'][l].astype(BF16),
                                 row(P['ln_mix_post'][l]), km.reshape(B, n_mem, D), vm.reshape(B, n_mem, D),
                                 row(P['ln_x_pre'][l]), P['wq_x'][l].astype(BF16), P['wo_x'][l].astype(BF16),
                                 row(P['ln_x_post'][l]), L, min(MIXOUT_ROW_TILE, L))
        xf = _swiglu_block(xf, row(P['ln_ffn_pre'][l]), P['w_gate'][l].astype(BF16), P['w_up'][l].astype(BF16),
                           P['w_down'][l].astype(BF16), row(P['ln_ffn_post'][l]), tm)
    return xf.reshape(B, L, D)


def kernel(x_prompt, x_sample, mem_prompt, mem_sample, ln_mix_pre, ln_mix_post, w_in, lambda_q1, lambda_k1, lambda_q2, lambda_k2, subln_g, conv_w, conv_b, filt_w1, filt_b1, filt_freq1, filt_w2, filt_b2, filt_freq2, filt_w3, hyena_d, w_out, ln_x_pre, ln_x_post, ln_mem, wq_x, wk_x, wv_x, wo_x, ln_ffn_pre, ln_ffn_post, w_gate, w_up, w_down):
    P = dict(ln_mix_pre=ln_mix_pre, ln_mix_post=ln_mix_post, w_in=w_in,
             lambda_q1=lambda_q1, lambda_k1=lambda_k1, lambda_q2=lambda_q2, lambda_k2=lambda_k2,
             subln_g=subln_g, conv_w=conv_w, conv_b=conv_b,
             filt_w1=filt_w1, filt_b1=filt_b1, filt_freq1=filt_freq1,
             filt_w2=filt_w2, filt_b2=filt_b2, filt_freq2=filt_freq2, filt_w3=filt_w3,
             hyena_d=hyena_d, w_out=w_out,
             ln_x_pre=ln_x_pre, ln_x_post=ln_x_post, ln_mem=ln_mem,
             wq_x=wq_x, wk_x=wk_x, wv_x=wv_x, wo_x=wo_x,
             ln_ffn_pre=ln_ffn_pre, ln_ffn_post=ln_ffn_post,
             w_gate=w_gate, w_up=w_up, w_down=w_down)
    return (_trunk(x_prompt, mem_prompt, P), _trunk(x_sample, mem_sample, P))
```

```python
import functools
import math

import numpy as np
import jax
import jax.numpy as jnp
from jax import lax
from jax.experimental import pallas as pl
from jax.experimental.pallas import tpu as pltpu

F32 = jnp.float32
BF16 = jnp.bfloat16

DA_HEADS = 4
DA_HEAD_DIM = 64
DA_V_DIM = 128
ROPE_DIM = 16
ROPE_THETA = 500000.0
SHORT_CONV = 3
FILTER_EMB = 33
FILTER_ORDER = 64
FAST_DECAY_PCT = 0.3
SLOW_DECAY_PCT = 1.5
DECAY_TARGET = 1e-2
X_HEADS = 4
EPS = 1e-6
LOG2E = 1.4426950408889634

LANES = 128
SUBLANES = 8
BF16_SUBLANES = 16
MXU_WIDTH = 256
VMEM_LIMIT_CAP = 58 * 2**20
VMEM_TEMP_FACTOR = 1.25
VMEM_TEMP_BYTES = 4 * 2**20

ROW_TILE = 512
MIXOUT_ROW_TILE = 1024
KEY_CHUNK = 1024
KEY_BLOCK = 512
CONV_CHUNK = 512
ATTN_PASSES_PER_STEP = 16

STAGE1_UNROLL = 32
STAGE2_UNROLL = 16


def _round_up(a, b):
    return -(-a // b) * b


def _vmem_limit(vmem_bytes):
    return int(vmem_bytes * VMEM_TEMP_FACTOR) + VMEM_TEMP_BYTES


def _params(semantics, vmem_bytes):
    limit = min(_vmem_limit(vmem_bytes), VMEM_LIMIT_CAP)
    return pltpu.CompilerParams(dimension_semantics=semantics, vmem_limit_bytes=limit)


def _const_spec(shape):
    nd = len(shape)
    return pl.BlockSpec(shape, lambda *_: (0,) * nd, pipeline_mode=pl.Buffered(1))


def _rms(x, g):
    ms = jnp.mean(x * x, axis=-1, keepdims=True)
    return x * lax.rsqrt(ms + EPS) * g


def _dot(a, b):
    return jnp.dot(a, b, preferred_element_type=F32)


def _dot_nt(a, b):
    return lax.dot_general(a, b, (((1,), (1,)), ((), ())), preferred_element_type=F32)


def _rope_tables(L):
    inv = ROPE_THETA ** (-np.arange(0, ROPE_DIM, 2, dtype=np.float64) / ROPE_DIM)
    ang = np.arange(L, dtype=np.float64)[:, None] * inv[None, :]
    half = ROPE_DIM // 2
    cos = np.ones((L, LANES)); sa = np.zeros((L, LANES)); sb = np.zeros((L, LANES))
    for g in range(LANES // DA_HEAD_DIM):
        o = g * DA_HEAD_DIM
        cos[:, o:o + half] = np.cos(ang)
        cos[:, o + half:o + ROPE_DIM] = np.cos(ang)
        sb[:, o:o + half] = -np.sin(ang)
        sa[:, o + half:o + ROPE_DIM] = np.sin(ang)
    return (jnp.asarray(cos, F32), jnp.asarray(sa, F32), jnp.asarray(sb, F32),
            jnp.asarray(np.cos(ang).T, F32), jnp.asarray(np.sin(ang).T, F32))


def _inproj_kernel(x_ref, g_ref, w_ref, wqt_ref, wvt_ref, cos_ref, sa_ref, sb_ref, cost_ref, sint_ref,
                   qt_ref, k_ref, vt_ref, hy_ref, *, q_scale):
    xn = _rms(x_ref[...], g_ref[...]).astype(BF16)
    cos, sa, sb = cos_ref[...], sa_ref[...], sb_ref[...]
    half = ROPE_DIM // 2

    def rope(y):
        parts = []
        for s in range(y.shape[1] // LANES):
            ys = y[:, s * LANES:(s + 1) * LANES]
            parts.append(ys * cos + pltpu.roll(ys, half, 1) * sa
                         + pltpu.roll(ys, LANES - half, 1) * sb)
        return jnp.concatenate(parts, axis=1)

    def rope_t(y):
        ct, st = cost_ref[...], sint_ref[...]
        parts = []
        for o in range(0, y.shape[0], DA_HEAD_DIM):
            x1, x2 = y[o:o + half], y[o + half:o + ROPE_DIM]
            parts += [x1 * ct - x2 * st, x2 * ct + x1 * st, y[o + ROPE_DIM:o + DA_HEAD_DIM]]
        return jnp.concatenate(parts, axis=0)

    a = k_ref.shape[1]
    qt_ref[0, 0] = (rope_t(_dot_nt(wqt_ref[...], xn)) * q_scale).astype(BF16)
    k_ref[...] = rope(_dot(xn, w_ref[:, a:2 * a])).astype(BF16)
    vt_ref[0, 0] = _dot_nt(wvt_ref[...], xn).astype(BF16)
    hy_ref[...] = _dot(xn, w_ref[:, 3 * a:]).astype(BF16)


def _inproj(x, g, w, tables, B, L, tm, tk):
    T, D = x.shape
    n_out = w.shape[1]
    a = DA_HEADS * 2 * DA_HEAD_DIM
    hyw = n_out - 3 * a
    blocks_per_seq = L // tm
    tiles_per_chunk = tk // tm
    tab_spec = pl.BlockSpec((tm, LANES), lambda i: (i % blocks_per_seq, 0))
    tabt_spec = pl.BlockSpec((ROPE_DIM // 2, tm), lambda i: (0, i % blocks_per_seq))
    row = lambda width: pl.BlockSpec((tm, width), lambda i: (i, 0))
    qt_spec = pl.BlockSpec((1, 1, a, tm), lambda i: (i // blocks_per_seq, i % blocks_per_seq, 0, 0))
    vt_spec = pl.BlockSpec((1, 1, a, tm), lambda i: (i // blocks_per_seq, (i % blocks_per_seq) // tiles_per_chunk,
                                                     0, i % tiles_per_chunk))
    vmem = 2 * tm * D * 4 + D * (n_out + 2 * a) * 2 + 8 * tm * LANES * 4 + 2 * tm * n_out * 2 + 4 * tm * 1536 * 4
    q_scale = DA_HEAD_DIM ** -0.5 * LOG2E
    wqt, wvt = w[:, 0:a].T, w[:, 2 * a:3 * a].T
    return pl.pallas_call(
        functools.partial(_inproj_kernel, q_scale=q_scale),
        grid=(T // tm,),
        in_specs=[row(D), _const_spec((1, D)), _const_spec((D, n_out)), _const_spec((a, D)), _const_spec((a, D)),
                  tab_spec, tab_spec, tab_spec, tabt_spec, tabt_spec],
        out_specs=[qt_spec, row(a), vt_spec, row(hyw)],
        out_shape=[jax.ShapeDtypeStruct((B, L // tm, a, tm), BF16), jax.ShapeDtypeStruct((T, a), BF16),
                   jax.ShapeDtypeStruct((B, L // tk, a, tk), BF16), jax.ShapeDtypeStruct((T, hyw), BF16)],
        compiler_params=_params(("parallel",), vmem),
        name="inproj",
    )(x, g, w, wqt, wvt, *tables)


def _attn_kernel(lam_ref, qt_ref, k_ref, vt_ref, g_ref, o_ref, s_ref, *, tk, kb, lam_init):
    nt, _, tq = qt_ref.shape[1:]
    L = k_ref.shape[1]
    feat = lax.broadcasted_iota(jnp.int32, (2 * DA_HEAD_DIM, tq), 0)

    def split(qt):
        zero = jnp.zeros_like(qt)
        return (jnp.where(feat < DA_HEAD_DIM, qt, zero), jnp.where(feat >= DA_HEAD_DIM, qt, zero))

    qz = [split(qt_ref[0, t]) for t in range(nt)]

    lp = lam_ref[...]
    lam = (jnp.exp(jnp.sum(lp[0:1] * lp[1:2], axis=-1, keepdims=True))
           - jnp.exp(jnp.sum(lp[2:3] * lp[3:4], axis=-1, keepdims=True)) + lam_init)
    g = jnp.concatenate([g_ref[...]] * (tq // LANES), axis=1)

    nk, nb = L // tk, tk // kb
    neg = jnp.full((1, tq), -1e30, F32)
    fresh = (neg, jnp.zeros((1, tq), F32), jnp.zeros((DA_V_DIM, tq), F32))

    def score_block(t, j, c, b, mx):
        s = _dot(k_ref[0, j * tk + b * kb:j * tk + (b + 1) * kb, :], qz[t][c])
        s_ref[c, b * kb:(b + 1) * kb, :] = s
        return jnp.maximum(mx, jnp.max(s, axis=0, keepdims=True))

    def value_block(j, c, b, mn, lsum, pv):
        p = jnp.exp2(s_ref[c, b * kb:(b + 1) * kb, :] - mn)
        lsum = lsum + jnp.sum(p, axis=0, keepdims=True)
        pv = pv + _dot(vt_ref[0, j, :, b * kb:(b + 1) * kb], p.astype(BF16))
        return lsum, pv

    def step(score_of, value_of, mx_cur, state):
        m, l, acc = state
        mn = jnp.maximum(m, mx_cur)
        alpha = jnp.exp2(m - mn)
        lsum, pv, mx = jnp.zeros((1, tq), F32), jnp.zeros((DA_V_DIM, tq), F32), neg
        for b in range(nb):
            if score_of is not None:
                mx = score_block(*score_of, b, mx)
            lsum, pv = value_block(*value_of[1:], b, mn, lsum, pv)
        return mx, (mn, alpha * l + lsum, alpha * acc + pv)

    def finish(t, st0, st1):
        (_, l0, a0), (_, l1, a1) = st0, st1
        o = a0 / l0 - lam * (a1 / l1)
        ms = jnp.mean(o * o, axis=0, keepdims=True)
        o = o * lax.rsqrt(ms + EPS) * g * (1.0 - lam_init)
        o_ref[0, t * tq:(t + 1) * tq, :] = o.T.astype(BF16)

    passes = [(t, j, c) for t in range(nt) for j in range(nk) for c in range(2)]
    mx = neg
    for b in range(nb):
        mx = score_block(*passes[0], b, mx)
    states = {}
    for n, cur in enumerate(passes):
        t, j, c = cur
        nxt = passes[n + 1] if n + 1 < len(passes) else None
        mx, states[(t, c)] = step(nxt, cur, mx, states.get((t, c), fresh))
        if j == nk - 1 and c == 1:
            finish(t, states.pop((t, 0)), states.pop((t, 1)))


def _diff_attention(lam_params, qt, k, vt, g, lam_init, tq, tk):
    B, L, A = k.shape
    nt = max(1, min(L // tq, ATTN_PASSES_PER_STEP // (2 * (L // tk))))
    qspec = pl.BlockSpec((1, nt, LANES, tq), lambda b, h, i: (b, i, h, 0))
    ospec = pl.BlockSpec((1, nt * tq, LANES), lambda b, h, i: (b, i, h))
    kspec = pl.BlockSpec((1, L, LANES), lambda b, h, i: (b, 0, h))
    vspec = pl.BlockSpec((1, L // tk, LANES, tk), lambda b, h, i: (b, 0, h, 0))
    g_cols = jnp.broadcast_to(g.reshape(DA_V_DIM, 1), (DA_V_DIM, LANES))
    kb = min(KEY_BLOCK, tk)
    vmem = (4 * L * LANES * 2 + 4 * nt * tq * LANES * 2 + 2 * tq * tk * 4 + 4 * tq * kb * 6
            + 16 * tq * LANES * 4)
    return pl.pallas_call(
        functools.partial(_attn_kernel, tk=tk, kb=kb, lam_init=lam_init),
        grid=(B, DA_HEADS, L // (nt * tq)),
        in_specs=[_const_spec(lam_params.shape), qspec, kspec, vspec, _const_spec((DA_V_DIM, LANES))],
        out_specs=ospec,
        out_shape=jax.ShapeDtypeStruct((B, L, A), BF16),
        scratch_shapes=[pltpu.VMEM((2, tk, tq), F32)],
        compiler_params=_params(("parallel", "parallel", "parallel"), vmem),
        name="diff_attn",
    )(lam_params, qt, k, vt, g_cols)


class _FftPlan:
    def __init__(self, L):
        n = 2 * L
        lg = int(math.log2(n))
        assert 2 ** lg == n
        self.L, self.n = L, n
        self.N1 = 2 ** (lg // 2)
        self.N2 = n // self.N1
        self.K1 = self.N1 // 2 + 1
        self.R = _round_up(2 * self.K1, BF16_SUBLANES)
        self.P = self.N2 + SUBLANES
        assert (self.P // SUBLANES) % 2 == 1
        assert self.N2 % BF16_SUBLANES == 0 and (self.N1 // 2) % BF16_SUBLANES == 0


@functools.lru_cache(maxsize=None)
def _fft_consts(L):
    p = _FftPlan(L)
    N1, N2, K1, R, n = p.N1, p.N2, p.K1, p.R, p.n
    t2 = np.arange(N2, dtype=np.float64)[:, None, None]
    k1 = np.arange(K1, dtype=np.float64)[None, :, None]
    t1 = np.arange(N1, dtype=np.float64)[None, None, :]
    ang = -2.0 * np.pi * (t1 * k1 / N1 + t2 * k1 / n)
    f1 = np.zeros((N2, R, N1))
    f1[:, 0:2 * K1:2, :] = np.cos(ang)
    f1[:, 1:2 * K1:2, :] = np.sin(ang)
    c = np.where((np.arange(K1) == 0) | (np.arange(K1) == N1 // 2), 1.0, 2.0)[None, :, None]
    th = -ang[:, :, :N1 // 2]
    g1 = np.zeros((N2, N1 // 2, R))
    g1[:, :, 0:2 * K1:2] = np.transpose(c * np.cos(th), (0, 2, 1))
    g1[:, :, 1:2 * K1:2] = np.transpose(-c * np.sin(th), (0, 2, 1))
    a2 = 2.0 * np.pi * np.outer(np.arange(N2), np.arange(N2)) / N2
    C, S = np.cos(a2), np.sin(a2)
    f2 = np.block([[C, S], [-S, C]])
    return (jnp.asarray(f1, BF16), jnp.asarray(g1, BF16), jnp.asarray(f2, BF16), jnp.asarray(f2.T, BF16))


def _load_rows(ref, idx):
    parts = [ref[g, idx, :] for g in range(ref.shape[0])]
    return parts[0] if len(parts) == 1 else jnp.concatenate(parts, axis=1)


def _store_rows(ref, idx, val):
    for g in range(ref.shape[0]):
        ref[g, idx, :] = val[:, g * LANES:(g + 1) * LANES]


def _store_slabs(ref, first_slab, val, plan):
    for s in range(val.shape[0] // plan.N2):
        start = pl.multiple_of((first_slab + s) * plan.P, SUBLANES)
        _store_rows(ref, pl.ds(start, plan.N2), val[s * plan.N2:(s + 1) * plan.N2])


def _load_slabs(ref, first_slab, count, plan):
    parts = [_load_rows(ref, pl.ds(pl.multiple_of((first_slab + s) * plan.P, SUBLANES), plan.N2))
             for s in range(count)]
    return parts[0] if count == 1 else jnp.concatenate(parts, axis=0)


def _dft_stage1(src_ref, f1_ref, a_ref, n_slabs, plan):
    def body(t2, _):
        rows = _load_rows(src_ref, pl.ds(t2, n_slabs, stride=plan.P)).astype(BF16)
        _store_rows(a_ref, pl.ds(t2, plan.R, stride=plan.P), _dot(f1_ref[t2], rows))
        return 0

    lax.fori_loop(0, plan.N2, body, 0, unroll=STAGE1_UNROLL)


@functools.lru_cache(maxsize=None)
def _filter_features(L):
    t = np.linspace(0.0, 1.0, L)[:, None]
    bands = (FILTER_EMB - 1) // 2
    w = 2.0 * np.pi * np.arange(L)[:, None] / L
    f = np.linspace(1e-4, bands - 1, bands)[None, :]
    z = np.concatenate([t, np.cos(f * w), -np.sin(f * w)], axis=-1)
    z_rev = np.concatenate([z[:1], z[:0:-1]], axis=0)
    zz = np.zeros((2 * L, LANES))
    zz[:L, :FILTER_EMB] = z
    zz[L:, :FILTER_EMB] = z_rev
    return jnp.asarray(zz, BF16)


def _decay_rates(width):
    max_decay = math.log(DECAY_TARGET) / FAST_DECAY_PCT
    min_decay = math.log(DECAY_TARGET) / SLOW_DECAY_PCT
    return jnp.asarray(np.abs(np.linspace(min_decay, max_decay, width))[None, :], F32)


def _filter_kernel(z_ref, w1_ref, b1_ref, fr1_ref, w2_ref, b2_ref, fr2_ref,
                   w3f_ref, w3b_ref, dl_ref, f1_ref, f2_ref, kf_ref,
                   h_ref, kern_ref, a_ref, *, plan, ch):
    L, N1, N2, K1 = plan.L, plan.N1, plan.N2, plan.K1
    n_ch = (2 * L) // ch

    @pl.when(pl.program_id(0) == 0)
    def _():
        def mlp(i, _):
            r0 = pl.multiple_of(i * ch, ch)
            h = jnp.sin(fr1_ref[...] * (_dot(z_ref[pl.ds(r0, ch), :], w1_ref[...]) + b1_ref[...]))
            h = jnp.sin(fr2_ref[...] * (_dot(h.astype(BF16), w2_ref[...]) + b2_ref[...]))
            h_ref[pl.ds(r0, ch), :] = h.astype(BF16)
            return 0
        lax.fori_loop(0, n_ch, mlp, 0)

    tc = kf_ref.shape[1]
    local = lax.broadcasted_iota(jnp.int32, (ch, tc), 0)

    def synth(i, asum):
        r0 = pl.multiple_of(i * ch, ch)
        row = local + r0
        h = h_ref[pl.ds(r0, ch), :]
        fwd = _dot(h, w3f_ref[...])
        bwd = _dot(h, w3b_ref[...])
        lag = jnp.where(row < L, row, 2 * L - row).astype(F32)
        win = jnp.exp(lag * (-1.0 / (L - 1)) * dl_ref[...])
        val = jnp.where(row < L, fwd + jnp.where(row == 0, bwd, 0.0), bwd) * win
        val = jnp.where(row == L, 0.0, val)
        _store_slabs(kern_ref, i * (ch // N2), val, plan)
        return asum + jnp.sum(jnp.abs(val), axis=0, keepdims=True)

    asum = lax.fori_loop(0, n_ch, synth, jnp.zeros((1, tc), F32))
    scale = 1.0 / (asum * float(plan.n))

    _dft_stage1(kern_ref, f1_ref, a_ref, N1, plan)

    def stage2(k1, _):
        r0 = pl.multiple_of(k1 * 2 * N2, 2 * N2)
        slab = _load_slabs(a_ref, 2 * k1, 2, plan).astype(BF16)
        kf_ref[pl.ds(r0, 2 * N2), :] = (_dot(f2_ref[...], slab) * scale).astype(kf_ref.dtype)
        return 0

    lax.fori_loop(0, K1, stage2, 0, unroll=STAGE2_UNROLL)
    pad = plan.R - 2 * K1
    if pad:
        kf_ref[pl.ds(2 * K1 * N2, pad * N2), :] = jnp.zeros((pad * N2, tc), kf_ref.dtype)


def _hyena_filter_spectrum(L, w1, b1, fr1, w2, b2, fr2, w3, tc):
    plan = _FftPlan(L)
    C = w3.shape[1] // 2
    zz = _filter_features(L)
    f1, _, f2, _ = _fft_consts(L)
    w1p = jnp.zeros((LANES, FILTER_ORDER), BF16).at[:FILTER_EMB].set(w1.astype(BF16))
    ch = min(CONV_CHUNK, L)
    rows_a = plan.R * plan.N2
    cblock = lambda off: pl.BlockSpec((FILTER_ORDER, tc), lambda c: (0, off + c))
    vmem = (2 * L * LANES * 2 * 2 + 2 * L * tc * 4 + rows_a * tc * 4 + 2 * rows_a * tc * 2
            + f1.size * 2 * 2 + 8 * ch * tc * 4)
    return pl.pallas_call(
        functools.partial(_filter_kernel, plan=plan, ch=ch),
        grid=(C // tc,),
        in_specs=[_const_spec(zz.shape),
                  _const_spec(w1p.shape), _const_spec((1, FILTER_ORDER)), _const_spec((1, FILTER_ORDER)),
                  _const_spec((FILTER_ORDER, FILTER_ORDER)), _const_spec((1, FILTER_ORDER)),
                  _const_spec((1, FILTER_ORDER)),
                  cblock(0), cblock(C // tc),
                  pl.BlockSpec((1, tc), lambda c: (0, c)),
                  _const_spec(f1.shape), _const_spec(f2.shape)],
        out_specs=pl.BlockSpec((rows_a, tc), lambda c: (0, c)),
        out_shape=jax.ShapeDtypeStruct((rows_a, C), BF16),
        scratch_shapes=[pltpu.VMEM((2 * L, FILTER_ORDER), BF16),
                        pltpu.VMEM((tc // LANES, plan.N1 * plan.P, LANES), F32),
                        pltpu.VMEM((tc // LANES, plan.R * plan.P, LANES), F32)],
        compiler_params=_params(("arbitrary",), vmem),
        name="hyena_filter",
    )(zz, w1p, b1, fr1, w2.astype(BF16), b2, fr2, w3.astype(BF16), w3.astype(BF16),
      _decay_rates(C), f1, f2)


def _short_conv_chunk(raw_ref, w_ref, b_ref, r0, ch, L):
    halo = BF16_SUBLANES
    c = raw_ref[0, pl.ds(r0, ch), :].astype(F32)
    p0 = pl.multiple_of(jnp.maximum(r0 - halo, 0), halo)
    n0 = pl.multiple_of(jnp.minimum(r0 + ch, L - halo), halo)
    prev = raw_ref[0, pl.ds(p0, halo), :].astype(F32)[halo - 1:halo]
    nxt = raw_ref[0, pl.ds(n0, halo), :].astype(F32)[0:1]
    prev = jnp.where(r0 == 0, 0.0, prev)
    nxt = jnp.where(r0 + ch == L, 0.0, nxt)
    rows = lax.broadcasted_iota(jnp.int32, c.shape, 0)
    up = jnp.where(rows == 0, prev, pltpu.roll(c, 1, 0))
    un = jnp.where(rows == ch - 1, nxt, pltpu.roll(c, ch - 1, 0))
    w = w_ref[...]
    return w[0:1] * up + w[1:2] * c + w[2:3] * un + b_ref[...]


def _hyena_kernel(x0_ref, x1_ref, vh_ref, w0_ref, w1_ref, wv_ref, b0_ref, b1_ref, bv_ref,
                  d_ref, kf_ref, f1_ref, g1_ref, f2_ref, f2i_ref, o_ref,
                  u_ref, a_ref, y_ref, *, plan, ch):
    L, N1, N2, K1, R = plan.L, plan.N1, plan.N2, plan.K1, plan.R
    n_ch = L // ch

    def gate_in(i, _):
        r0 = pl.multiple_of(i * ch, ch)
        x1 = _short_conv_chunk(x1_ref, w1_ref, b1_ref, r0, ch, L)
        vh = _short_conv_chunk(vh_ref, wv_ref, bv_ref, r0, ch, L)
        _store_slabs(u_ref, i * (ch // N2), x1 * vh, plan)
        return 0

    lax.fori_loop(0, n_ch, gate_in, 0)

    _dft_stage1(u_ref, f1_ref, a_ref, N1 // 2, plan)

    tc = o_ref.shape[2]
    group = max(1, MXU_WIDTH // tc)

    def spectral_group(first, count):
        x = _dot(f2_ref[...], jnp.concatenate(
            [_load_slabs(a_ref, 2 * (first + i), 2, plan) for i in range(count)], axis=1).astype(BF16))
        ys = []
        for i in range(count):
            r0 = pl.multiple_of((first + i) * 2 * N2, 2 * N2)
            xr, xi = x[:N2, i * tc:(i + 1) * tc], x[N2:, i * tc:(i + 1) * tc]
            kr = kf_ref[pl.ds(r0, N2), :].astype(F32)
            ki = kf_ref[pl.ds(r0 + N2, N2), :].astype(F32)
            ys.append(jnp.concatenate([xr * kr - xi * ki, xr * ki + xi * kr], axis=0))
        z = _dot(f2i_ref[...], jnp.concatenate(ys, axis=1).astype(BF16))
        for i in range(count):
            _store_slabs(a_ref, 2 * (first + i), z[:, i * tc:(i + 1) * tc], plan)

    def spectral(i, _):
        spectral_group(i * group, group)
        return 0

    lax.fori_loop(0, K1 // group, spectral, 0, unroll=STAGE2_UNROLL // group)
    for k1 in range(K1 - K1 % group, K1):
        spectral_group(k1, 1)

    def inverse1(t2, _):
        rows = _load_rows(a_ref, pl.ds(t2, R, stride=plan.P)).astype(BF16)
        _store_rows(y_ref, pl.ds(t2, N1 // 2, stride=plan.P), _dot(g1_ref[t2], rows))
        return 0

    lax.fori_loop(0, N2, inverse1, 0, unroll=STAGE1_UNROLL)

    def gate_out(i, _):
        r0 = pl.multiple_of(i * ch, ch)
        x0 = _short_conv_chunk(x0_ref, w0_ref, b0_ref, r0, ch, L)
        u = _load_slabs(u_ref, i * (ch // N2), ch // N2, plan)
        y = _load_slabs(y_ref, i * (ch // N2), ch // N2, plan)
        o_ref[0, pl.ds(r0, ch), :] = ((y + d_ref[...] * u) * x0).astype(o_ref.dtype)
        return 0

    lax.fori_loop(0, n_ch, gate_out, 0)


def _hyena_vmem_bytes(plan, tc, ch):
    L, rows_a = plan.L, plan.R * plan.N2
    pad_l = lambda m: _round_up(m, LANES)
    return (6 * L * tc * 2 + rows_a * tc * 2 + 2 * L * tc * 4 + rows_a * tc * 4 + 2 * L * tc * 2
            + plan.N2 * plan.R * pad_l(plan.N1 // 2) * 2 + plan.N2 * (plan.N1 // 2) * pad_l(plan.R) * 2
            + 16 * ch * tc * 4 + 16 * plan.N2 * tc * 4)


def _hyena_operator(hy, conv_w, conv_b, d, kf):
    B, L, W3 = hy.shape
    C = W3 // 3
    plan = _FftPlan(L)
    ch = min(CONV_CHUNK, L)
    tc = 2 * LANES if _vmem_limit(_hyena_vmem_bytes(plan, 2 * LANES, ch)) <= VMEM_LIMIT_CAP else LANES
    nb = C // tc
    f1full, g1, f2, f2i = _fft_consts(L)
    f1 = f1full[:, :, :plan.N1 // 2]
    rows_a = plan.R * plan.N2
    seq = lambda part: pl.BlockSpec((1, L, tc), lambda c, b: (b, 0, part * nb + c))
    wspec = lambda part: pl.BlockSpec((SHORT_CONV, tc), lambda c, b: (0, part * nb + c))
    bspec = lambda part: pl.BlockSpec((1, tc), lambda c, b: (0, part * nb + c))
    vmem = _hyena_vmem_bytes(plan, tc, ch)
    return pl.pallas_call(
        functools.partial(_hyena_kernel, plan=plan, ch=ch),
        grid=(nb, B),
        in_specs=[seq(0), seq(1), seq(2), wspec(0), wspec(1), wspec(2), bspec(0), bspec(1), bspec(2),
                  pl.BlockSpec((1, tc), lambda c, b: (0, c)),
                  pl.BlockSpec((rows_a, tc), lambda c, b: (0, c), pipeline_mode=pl.Buffered(1)),
                  _const_spec(f1.shape), _const_spec(g1.shape), _const_spec(f2.shape), _const_spec(f2i.shape)],
        out_specs=pl.BlockSpec((1, L, tc), lambda c, b: (b, 0, c)),
        out_shape=jax.ShapeDtypeStruct((B, L, C), BF16),
        scratch_shapes=[pltpu.VMEM((tc // LANES, (plan.N1 // 2) * plan.P, LANES), F32),
                        pltpu.VMEM((tc // LANES, plan.R * plan.P, LANES), F32),
                        pltpu.VMEM((tc // LANES, (plan.N1 // 2) * plan.P, LANES), F32)],
        compiler_params=_params(("parallel", "parallel"), vmem),
        name="hyena_op",
    )(hy, hy, hy, conv_w, conv_w, conv_w, conv_b, conv_b, conv_b, d, kf, f1, g1, f2, f2i)


def _memkv_kernel(m_ref, g_ref, wk_ref, wv_ref, k_ref, v_ref):
    mn = _rms(m_ref[...], g_ref[...]).astype(BF16)
    k_ref[...] = _dot(mn, wk_ref[...]).astype(BF16)
    v_ref[...] = _dot(mn, wv_ref[...]).astype(BF16)


def _memkv(mem, g, wk, wv, tm):
    T, D = mem.shape
    row = pl.BlockSpec((tm, D), lambda i: (i, 0))
    vmem = 2 * tm * D * 4 + 2 * D * D * 2 + 4 * tm * D * 2 + 3 * tm * D * 4
    return pl.pallas_call(
        _memkv_kernel,
        grid=(T // tm,),
        in_specs=[row, _const_spec((1, D)), _const_spec((D, D)), _const_spec((D, D))],
        out_specs=[row, row],
        out_shape=[jax.ShapeDtypeStruct((T, D), BF16)] * 2,
        compiler_params=_params(("parallel",), vmem),
        name="mem_kv",
    )(mem, g, wk, wv)


def _mixout_cross_kernel(a_ref, y_ref, x_ref, wa_ref, wy_ref, gmix_ref, k_ref, v_ref,
                         gpre_ref, wq_ref, wo_ref, gpost_ref, o_ref, *, q_scale):
    z = _dot(a_ref[...], wa_ref[...]) + _dot(y_ref[...], wy_ref[...])
    x = x_ref[...] + _rms(z, gmix_ref[...])
    xn = _rms(x, gpre_ref[...]).astype(BF16)
    q = (_dot(xn, wq_ref[...]) * q_scale).astype(BF16)
    hd = q.shape[1] // X_HEADS
    outs = []
    for h in range(X_HEADS):
        sl = slice(h * hd, (h + 1) * hd)
        s = _dot_nt(q[:, sl], k_ref[0, :, sl])
        p = jnp.exp2(s - jnp.max(s, axis=-1, keepdims=True))
        l = jnp.sum(p, axis=-1, keepdims=True)
        outs.append((_dot(p.astype(BF16), v_ref[0, :, sl]) / l).astype(BF16))
    z = _dot(jnp.concatenate(outs, axis=1), wo_ref[...])
    o_ref[...] = x + _rms(z, gpost_ref[...])


def _mixout_cross_block(attn, y, x, w_out, gmix, k, v, gpre, wq, wo, gpost, L, tm):
    T, D = x.shape
    a, c = attn.shape[1], y.shape[1]
    n_mem = k.shape[1]
    blocks_per_seq = L // tm
    row = lambda width: pl.BlockSpec((tm, width), lambda i: (i, 0))
    kv = pl.BlockSpec((1, n_mem, D), lambda i: (i // blocks_per_seq, 0, 0))
    vmem = (2 * tm * (a + c) * 2 + 4 * tm * D * 4 + 4 * n_mem * D * 2 + (a + c + 2 * D) * D * 2
            + 8 * tm * D * 4)
    q_scale = (D // X_HEADS) ** -0.5 * LOG2E
    return pl.pallas_call(
        functools.partial(_mixout_cross_kernel, q_scale=q_scale),
        grid=(T // tm,),
        in_specs=[row(a), row(c), row(D), _const_spec((a, D)), _const_spec((c, D)), _const_spec((1, D)),
                  kv, kv, _const_spec((1, D)), _const_spec((D, D)), _const_spec((D, D)), _const_spec((1, D))],
        out_specs=row(D),
        out_shape=jax.ShapeDtypeStruct((T, D), F32),
        compiler_params=_params(("parallel",), vmem),
        name="mixout_cross",
    )(attn, y, x, w_out[:a], w_out[a:], gmix, k, v, gpre, wq, wo, gpost)


def _swiglu_kernel(x_ref, gpre_ref, wg_ref, wu_ref, wd_ref, gpost_ref, o_ref, *, n_split):
    x = x_ref[...]
    xn = _rms(x, gpre_ref[...]).astype(BF16)
    ff = wg_ref.shape[1]
    cw = ff // n_split
    z = None
    for c in range(n_split):
        sl = slice(c * cw, (c + 1) * cw)
        gate = _dot(xn, wg_ref[:, sl])
        up = _dot(xn, wu_ref[:, sl])
        h = (gate * (1.0 / (1.0 + jnp.exp(-gate))) * up).astype(BF16)
        part = _dot(h, wd_ref[sl, :])
        z = part if z is None else z + part
    o_ref[...] = x + _rms(z, gpost_ref[...])


def _swiglu_block(x, gpre, wg, wu, wd, gpost, tm):
    T, D = x.shape
    ff = wg.shape[1]
    n_split = ff // MXU_WIDTH if ff % MXU_WIDTH == 0 else 1
    row = pl.BlockSpec((tm, D), lambda i: (i, 0))
    vmem = 4 * tm * D * 4 + 3 * D * ff * 2 + 4 * tm * (ff // n_split) * 4 + 4 * tm * D * 4
    return pl.pallas_call(
        functools.partial(_swiglu_kernel, n_split=n_split),
        grid=(T // tm,),
        in_specs=[row, _const_spec((1, D)), _const_spec((D, ff)), _const_spec((D, ff)),
                  _const_spec((ff, D)), _const_spec((1, D))],
        out_specs=row,
        out_shape=jax.ShapeDtypeStruct((T, D), F32),
        compiler_params=_params(("parallel",), vmem),
        name="swiglu",
    )(x, gpre, wg, wu, wd, gpost)


def _trunk(x, mem, P):
    B, L, D = x.shape
    depth = P['w_in'].shape[0]
    n_mem = mem.shape[1]
    tm = min(ROW_TILE, L)
    tq, tk = tm, min(KEY_CHUNK, L)
    tc = LANES
    a = DA_HEADS * 2 * DA_HEAD_DIM
    tables = _rope_tables(L)
    xf = x.reshape(B * L, D)
    memf = mem.reshape(B * n_mem, D)
    row = lambda v: v.reshape(1, -1)
    for l in range(depth):
        lam_init = 0.8 - 0.6 * math.exp(-0.3 * l)
        qt, k, vt, hy = _inproj(xf, row(P['ln_mix_pre'][l]), P['w_in'][l].astype(BF16), tables, B, L, tm, tk)
        lam_params = jnp.stack([P['lambda_q1'][l], P['lambda_k1'][l], P['lambda_q2'][l], P['lambda_k2'][l]])
        attn = _diff_attention(lam_params, qt, k.reshape(B, L, a), vt, P['subln_g'][l], lam_init, tq, tk)
        kf = _hyena_filter_spectrum(L, P['filt_w1'][l], row(P['filt_b1'][l]), row(P['filt_freq1'][l]),
                                    P['filt_w2'][l], row(P['filt_b2'][l]), row(P['filt_freq2'][l]),
                                    P['filt_w3'][l], tc)
        y = _hyena_operator(hy.reshape(B, L, -1), P['conv_w'][l], row(P['conv_b'][l]),
                            row(P['hyena_d'][l]), kf)
        km, vm = _memkv(memf, row(P['ln_mem'][l]), P['wk_x'][l].astype(BF16), P['wv_x'][l].astype(BF16),
                        min(ROW_TILE, B * n_mem))
        xf = _mixout_cross_block(attn.reshape(B * L, a), y.reshape(B * L, -1), xf, P['w_out'][l].astype(BF16),
                                 row(P['ln_mix_post'][l]), km.reshape(B, n_mem, D), vm.reshape(B, n_mem, D),
                                 row(P['ln_x_pre'][l]), P['wq_x'][l].astype(BF16), P['wo_x'][l].astype(BF16),
                                 row(P['ln_x_post'][l]), L, min(MIXOUT_ROW_TILE, L))
        xf = _swiglu_block(xf, row(P['ln_ffn_pre'][l]), P['w_gate'][l].astype(BF16), P['w_up'][l].astype(BF16),
                           P['w_down'][l].astype(BF16), row(P['ln_ffn_post'][l]), tm)
    return xf.reshape(B, L, D)


def kernel(x_prompt, x_sample, mem_prompt, mem_sample, ln_mix_pre, ln_mix_post, w_in, lambda_q1, lambda_k1, lambda_q2, lambda_k2, subln_g, conv_w, conv_b, filt_w1, filt_b1, filt_freq1, filt_w2, filt_b2, filt_freq2, filt_w3, hyena_d, w_out, ln_x_pre, ln_x_post, ln_mem, wq_x, wk_x, wv_x, wo_x, ln_ffn_pre, ln_ffn_post, w_gate, w_up, w_down):
    P = dict(ln_mix_pre=ln_mix_pre, ln_mix_post=ln_mix_post, w_in=w_in,
             lambda_q1=lambda_q1, lambda_k1=lambda_k1, lambda_q2=lambda_q2, lambda_k2=lambda_k2,
             subln_g=subln_g, conv_w=conv_w, conv_b=conv_b,
             filt_w1=filt_w1, filt_b1=filt_b1, filt_freq1=filt_freq1,
             filt_w2=filt_w2, filt_b2=filt_b2, filt_freq2=filt_freq2, filt_w3=filt_w3,
             hyena_d=hyena_d, w_out=w_out,
             ln_x_pre=ln_x_pre, ln_x_post=ln_x_post, ln_mem=ln_mem,
             wq_x=wq_x, wk_x=wk_x, wv_x=wv_x, wo_x=wo_x,
             ln_ffn_pre=ln_ffn_pre, ln_ffn_post=ln_ffn_post,
             w_gate=w_gate, w_up=w_up, w_down=w_down)
    return (_trunk(x_prompt, mem_prompt, P), _trunk(x_sample, mem_sample, P))
```

```python
import functools
import math

import numpy as np
import jax
import jax.numpy as jnp
from jax import lax
from jax.experimental import pallas as pl
from jax.experimental.pallas import tpu as pltpu

F32 = jnp.float32
BF16 = jnp.bfloat16

DA_HEADS = 4
DA_HEAD_DIM = 64
DA_V_DIM = 128
ROPE_DIM = 16
ROPE_THETA = 500000.0
SHORT_CONV = 3
FILTER_EMB = 33
FILTER_ORDER = 64
FAST_DECAY_PCT = 0.3
SLOW_DECAY_PCT = 1.5
DECAY_TARGET = 1e-2
X_HEADS = 4
EPS = 1e-6
LOG2E = 1.4426950408889634

LANES = 128
SUBLANES = 8
BF16_SUBLANES = 16
MXU_WIDTH = 256
VMEM_LIMIT_CAP = 58 * 2**20
VMEM_TEMP_FACTOR = 1.25
VMEM_TEMP_BYTES = 4 * 2**20

ROW_TILE = 512
MIXOUT_ROW_TILE = 1024
KEY_CHUNK = 1024
KEY_BLOCK = 512
CONV_CHUNK = 512
ATTN_PASSES_PER_STEP = 32

STAGE1_UNROLL = 32
STAGE2_UNROLL = 16


def _round_up(a, b):
    return -(-a // b) * b


def _vmem_limit(vmem_bytes):
    return int(vmem_bytes * VMEM_TEMP_FACTOR) + VMEM_TEMP_BYTES


def _params(semantics, vmem_bytes):
    limit = min(_vmem_limit(vmem_bytes), VMEM_LIMIT_CAP)
    return pltpu.CompilerParams(dimension_semantics=semantics, vmem_limit_bytes=limit)


def _const_spec(shape):
    nd = len(shape)
    return pl.BlockSpec(shape, lambda *_: (0,) * nd, pipeline_mode=pl.Buffered(1))


def _rms(x, g):
    ms = jnp.mean(x * x, axis=-1, keepdims=True)
    return x * lax.rsqrt(ms + EPS) * g


def _dot(a, b):
    return jnp.dot(a, b, preferred_element_type=F32)


def _dot_nt(a, b):
    return lax.dot_general(a, b, (((1,), (1,)), ((), ())), preferred_element_type=F32)


def _rope_tables(L):
    inv = ROPE_THETA ** (-np.arange(0, ROPE_DIM, 2, dtype=np.float64) / ROPE_DIM)
    ang = np.arange(L, dtype=np.float64)[:, None] * inv[None, :]
    half = ROPE_DIM // 2
    cos = np.ones((L, LANES)); sa = np.zeros((L, LANES)); sb = np.zeros((L, LANES))
    for g in range(LANES // DA_HEAD_DIM):
        o = g * DA_HEAD_DIM
        cos[:, o:o + half] = np.cos(ang)
        cos[:, o + half:o + ROPE_DIM] = np.cos(ang)
        sb[:, o:o + half] = -np.sin(ang)
        sa[:, o + half:o + ROPE_DIM] = np.sin(ang)
    return (jnp.asarray(cos, F32), jnp.asarray(sa, F32), jnp.asarray(sb, F32),
            jnp.asarray(np.cos(ang).T, F32), jnp.asarray(np.sin(ang).T, F32))


def _inproj_kernel(x_ref, g_ref, w_ref, wqt_ref, wvt_ref, cos_ref, sa_ref, sb_ref, cost_ref, sint_ref,
                   qt_ref, k_ref, vt_ref, hy_ref, *, q_scale):
    xn = _rms(x_ref[...], g_ref[...]).astype(BF16)
    cos, sa, sb = cos_ref[...], sa_ref[...], sb_ref[...]
    half = ROPE_DIM // 2

    def rope(y):
        parts = []
        for s in range(y.shape[1] // LANES):
            ys = y[:, s * LANES:(s + 1) * LANES]
            parts.append(ys * cos + pltpu.roll(ys, half, 1) * sa
                         + pltpu.roll(ys, LANES - half, 1) * sb)
        return jnp.concatenate(parts, axis=1)

    def rope_t(y):
        ct, st = cost_ref[...], sint_ref[...]
        parts = []
        for o in range(0, y.shape[0], DA_HEAD_DIM):
            x1, x2 = y[o:o + half], y[o + half:o + ROPE_DIM]
            parts += [x1 * ct - x2 * st, x2 * ct + x1 * st, y[o + ROPE_DIM:o + DA_HEAD_DIM]]
        return jnp.concatenate(parts, axis=0)

    a = k_ref.shape[1]
    qt_ref[0, 0] = (rope_t(_dot_nt(wqt_ref[...], xn)) * q_scale).astype(BF16)
    k_ref[...] = rope(_dot(xn, w_ref[:, a:2 * a])).astype(BF16)
    vt_ref[0, 0] = _dot_nt(wvt_ref[...], xn).astype(BF16)
    hy_ref[...] = _dot(xn, w_ref[:, 3 * a:]).astype(BF16)


def _inproj(x, g, w, tables, B, L, tm, tk):
    T, D = x.shape
    n_out = w.shape[1]
    a = DA_HEADS * 2 * DA_HEAD_DIM
    hyw = n_out - 3 * a
    blocks_per_seq = L // tm
    tiles_per_chunk = tk // tm
    tab_spec = pl.BlockSpec((tm, LANES), lambda i: (i % blocks_per_seq, 0))
    tabt_spec = pl.BlockSpec((ROPE_DIM // 2, tm), lambda i: (0, i % blocks_per_seq))
    row = lambda width: pl.BlockSpec((tm, width), lambda i: (i, 0))
    qt_spec = pl.BlockSpec((1, 1, a, tm), lambda i: (i // blocks_per_seq, i % blocks_per_seq, 0, 0))
    vt_spec = pl.BlockSpec((1, 1, a, tm), lambda i: (i // blocks_per_seq, (i % blocks_per_seq) // tiles_per_chunk,
                                                     0, i % tiles_per_chunk))
    vmem = 2 * tm * D * 4 + D * (n_out + 2 * a) * 2 + 8 * tm * LANES * 4 + 2 * tm * n_out * 2 + 4 * tm * 1536 * 4
    q_scale = DA_HEAD_DIM ** -0.5 * LOG2E
    wqt, wvt = w[:, 0:a].T, w[:, 2 * a:3 * a].T
    return pl.pallas_call(
        functools.partial(_inproj_kernel, q_scale=q_scale),
        grid=(T // tm,),
        in_specs=[row(D), _const_spec((1, D)), _const_spec((D, n_out)), _const_spec((a, D)), _const_spec((a, D)),
                  tab_spec, tab_spec, tab_spec, tabt_spec, tabt_spec],
        out_specs=[qt_spec, row(a), vt_spec, row(hyw)],
        out_shape=[jax.ShapeDtypeStruct((B, L // tm, a, tm), BF16), jax.ShapeDtypeStruct((T, a), BF16),
                   jax.ShapeDtypeStruct((B, L // tk, a, tk), BF16), jax.ShapeDtypeStruct((T, hyw), BF16)],
        compiler_params=_params(("parallel",), vmem),
        name="inproj",
    )(x, g, w, wqt, wvt, *tables)


def _attn_kernel(lam_ref, qt_ref, k_ref, vt_ref, g_ref, o_ref, s_ref, *, tk, kb):
    nt, _, tq = qt_ref.shape[1:]
    L = k_ref.shape[1]
    feat = lax.broadcasted_iota(jnp.int32, (2 * DA_HEAD_DIM, tq), 0)

    def split(qt):
        zero = jnp.zeros_like(qt)
        return (jnp.where(feat < DA_HEAD_DIM, qt, zero), jnp.where(feat >= DA_HEAD_DIM, qt, zero))

    qz = [split(qt_ref[0, t]) for t in range(nt)]

    lp = lam_ref[...]
    lam_init = lp[4:5, 0:1]
    lam = (jnp.exp(jnp.sum(lp[0:1] * lp[1:2], axis=-1, keepdims=True))
           - jnp.exp(jnp.sum(lp[2:3] * lp[3:4], axis=-1, keepdims=True)) + lam_init)
    g = jnp.concatenate([g_ref[...]] * (tq // LANES), axis=1)

    nk, nb = L // tk, tk // kb
    neg = jnp.full((1, tq), -1e30, F32)
    fresh = (neg, jnp.zeros((1, tq), F32), jnp.zeros((DA_V_DIM, tq), F32))

    def score_block(t, j, c, b, mx):
        s = _dot(k_ref[0, j * tk + b * kb:j * tk + (b + 1) * kb, :], qz[t][c])
        s_ref[c, b * kb:(b + 1) * kb, :] = s
        return jnp.maximum(mx, jnp.max(s, axis=0, keepdims=True))

    def value_block(j, c, b, mn, lsum, pv):
        p = jnp.exp2(s_ref[c, b * kb:(b + 1) * kb, :] - mn)
        lsum = lsum + jnp.sum(p, axis=0, keepdims=True)
        pv = pv + _dot(vt_ref[0, j, :, b * kb:(b + 1) * kb], p.astype(BF16))
        return lsum, pv

    def step(score_of, value_of, mx_cur, state):
        m, l, acc = state
        mn = jnp.maximum(m, mx_cur)
        alpha = jnp.exp2(m - mn)
        lsum, pv, mx = jnp.zeros((1, tq), F32), jnp.zeros((DA_V_DIM, tq), F32), neg
        for b in range(nb):
            if score_of is not None:
                mx = score_block(*score_of, b, mx)
            lsum, pv = value_block(*value_of[1:], b, mn, lsum, pv)
        return mx, (mn, alpha * l + lsum, alpha * acc + pv)

    def finish(t, st0, st1):
        (_, l0, a0), (_, l1, a1) = st0, st1
        o = a0 / l0 - lam * (a1 / l1)
        ms = jnp.mean(o * o, axis=0, keepdims=True)
        o = o * lax.rsqrt(ms + EPS) * g * (1.0 - lam_init)
        o_ref[0, t * tq:(t + 1) * tq, :] = o.T.astype(BF16)

    passes = [(t, j, c) for t in range(nt) for j in range(nk) for c in range(2)]
    mx = neg
    for b in range(nb):
        mx = score_block(*passes[0], b, mx)
    states = {}
    for n, cur in enumerate(passes):
        t, j, c = cur
        nxt = passes[n + 1] if n + 1 < len(passes) else None
        mx, states[(t, c)] = step(nxt, cur, mx, states.get((t, c), fresh))
        if j == nk - 1 and c == 1:
            finish(t, states.pop((t, 0)), states.pop((t, 1)))


def _diff_attention(lam_params, qt, k, vt, g, tq, tk):
    B, L, A = k.shape
    nt = max(1, min(L // tq, ATTN_PASSES_PER_STEP // (2 * (L // tk))))
    qspec = pl.BlockSpec((1, nt, LANES, tq), lambda b, h, i: (b, i, h, 0))
    ospec = pl.BlockSpec((1, nt * tq, LANES), lambda b, h, i: (b, i, h))
    kspec = pl.BlockSpec((1, L, LANES), lambda b, h, i: (b, 0, h))
    vspec = pl.BlockSpec((1, L // tk, LANES, tk), lambda b, h, i: (b, 0, h, 0))
    g_cols = jnp.broadcast_to(g.reshape(DA_V_DIM, 1), (DA_V_DIM, LANES))
    kb = min(KEY_BLOCK, tk)
    vmem = (4 * L * LANES * 2 + 4 * nt * tq * LANES * 2 + 2 * tq * tk * 4 + 4 * tq * kb * 6
            + 16 * tq * LANES * 4)
    return pl.pallas_call(
        functools.partial(_attn_kernel, tk=tk, kb=kb),
        grid=(B, DA_HEADS, L // (nt * tq)),
        in_specs=[_const_spec(lam_params.shape), qspec, kspec, vspec, _const_spec((DA_V_DIM, LANES))],
        out_specs=ospec,
        out_shape=jax.ShapeDtypeStruct((B, L, A), BF16),
        scratch_shapes=[pltpu.VMEM((2, tk, tq), F32)],
        compiler_params=_params(("parallel", "parallel", "parallel"), vmem),
        name="diff_attn",
    )(lam_params, qt, k, vt, g_cols)


class _FftPlan:
    def __init__(self, L):
        n = 2 * L
        lg = int(math.log2(n))
        assert 2 ** lg == n
        self.L, self.n = L, n
        self.N1 = 2 ** (lg // 2)
        self.N2 = n // self.N1
        self.K1 = self.N1 // 2 + 1
        self.R = _round_up(2 * self.K1, BF16_SUBLANES)
        self.P = self.N2 + SUBLANES
        assert (self.P // SUBLANES) % 2 == 1
        assert self.N2 % BF16_SUBLANES == 0 and (self.N1 // 2) % BF16_SUBLANES == 0


@functools.lru_cache(maxsize=None)
def _fft_consts(L):
    p = _FftPlan(L)
    N1, N2, K1, R, n = p.N1, p.N2, p.K1, p.R, p.n
    t2 = np.arange(N2, dtype=np.float64)[:, None, None]
    k1 = np.arange(K1, dtype=np.float64)[None, :, None]
    t1 = np.arange(N1, dtype=np.float64)[None, None, :]
    ang = -2.0 * np.pi * (t1 * k1 / N1 + t2 * k1 / n)
    f1 = np.zeros((N2, R, N1))
    f1[:, 0:2 * K1:2, :] = np.cos(ang)
    f1[:, 1:2 * K1:2, :] = np.sin(ang)
    c = np.where((np.arange(K1) == 0) | (np.arange(K1) == N1 // 2), 1.0, 2.0)[None, :, None]
    th = -ang[:, :, :N1 // 2]
    g1 = np.zeros((N2, N1 // 2, R))
    g1[:, :, 0:2 * K1:2] = np.transpose(c * np.cos(th), (0, 2, 1))
    g1[:, :, 1:2 * K1:2] = np.transpose(-c * np.sin(th), (0, 2, 1))
    a2 = 2.0 * np.pi * np.outer(np.arange(N2), np.arange(N2)) / N2
    C, S = np.cos(a2), np.sin(a2)
    f2 = np.block([[C, S], [-S, C]])
    return (jnp.asarray(f1, BF16), jnp.asarray(g1, BF16), jnp.asarray(f2, BF16), jnp.asarray(f2.T, BF16))


def _load_rows(ref, idx):
    parts = [ref[g, idx, :] for g in range(ref.shape[0])]
    return parts[0] if len(parts) == 1 else jnp.concatenate(parts, axis=1)


def _store_rows(ref, idx, val):
    for g in range(ref.shape[0]):
        ref[g, idx, :] = val[:, g * LANES:(g + 1) * LANES]


def _store_slabs(ref, first_slab, val, plan):
    for s in range(val.shape[0] // plan.N2):
        start = pl.multiple_of((first_slab + s) * plan.P, SUBLANES)
        _store_rows(ref, pl.ds(start, plan.N2), val[s * plan.N2:(s + 1) * plan.N2])


def _load_slabs(ref, first_slab, count, plan):
    parts = [_load_rows(ref, pl.ds(pl.multiple_of((first_slab + s) * plan.P, SUBLANES), plan.N2))
             for s in range(count)]
    return parts[0] if count == 1 else jnp.concatenate(parts, axis=0)


def _dft_stage1(src_ref, f1_ref, a_ref, n_slabs, plan):
    def body(t2, _):
        rows = _load_rows(src_ref, pl.ds(t2, n_slabs, stride=plan.P)).astype(BF16)
        _store_rows(a_ref, pl.ds(t2, plan.R, stride=plan.P), _dot(f1_ref[t2], rows))
        return 0

    lax.fori_loop(0, plan.N2, body, 0, unroll=STAGE1_UNROLL)


@functools.lru_cache(maxsize=None)
def _filter_features(L):
    t = np.linspace(0.0, 1.0, L)[:, None]
    bands = (FILTER_EMB - 1) // 2
    w = 2.0 * np.pi * np.arange(L)[:, None] / L
    f = np.linspace(1e-4, bands - 1, bands)[None, :]
    z = np.concatenate([t, np.cos(f * w), -np.sin(f * w)], axis=-1)
    z_rev = np.concatenate([z[:1], z[:0:-1]], axis=0)
    zz = np.zeros((2 * L, LANES))
    zz[:L, :FILTER_EMB] = z
    zz[L:, :FILTER_EMB] = z_rev
    return jnp.asarray(zz, BF16)


def _decay_rates(width):
    max_decay = math.log(DECAY_TARGET) / FAST_DECAY_PCT
    min_decay = math.log(DECAY_TARGET) / SLOW_DECAY_PCT
    return jnp.asarray(np.abs(np.linspace(min_decay, max_decay, width))[None, :], F32)


def _filter_kernel(z_ref, w1_ref, b1_ref, fr1_ref, w2_ref, b2_ref, fr2_ref,
                   w3f_ref, w3b_ref, dl_ref, f1_ref, f2_ref, kf_ref,
                   h_ref, kern_ref, a_ref, *, plan, ch):
    L, N1, N2, K1 = plan.L, plan.N1, plan.N2, plan.K1
    n_ch = (2 * L) // ch

    @pl.when(pl.program_id(0) == 0)
    def _():
        def mlp(i, _):
            r0 = pl.multiple_of(i * ch, ch)
            h = jnp.sin(fr1_ref[...] * (_dot(z_ref[pl.ds(r0, ch), :], w1_ref[...]) + b1_ref[...]))
            h = jnp.sin(fr2_ref[...] * (_dot(h.astype(BF16), w2_ref[...]) + b2_ref[...]))
            h_ref[pl.ds(r0, ch), :] = h.astype(BF16)
            return 0
        lax.fori_loop(0, n_ch, mlp, 0)

    tc = kf_ref.shape[1]
    local = lax.broadcasted_iota(jnp.int32, (ch, tc), 0)

    def synth(i, asum):
        r0 = pl.multiple_of(i * ch, ch)
        row = local + r0
        h = h_ref[pl.ds(r0, ch), :]
        fwd = _dot(h, w3f_ref[...])
        bwd = _dot(h, w3b_ref[...])
        lag = jnp.where(row < L, row, 2 * L - row).astype(F32)
        win = jnp.exp(lag * (-1.0 / (L - 1)) * dl_ref[...])
        val = jnp.where(row < L, fwd + jnp.where(row == 0, bwd, 0.0), bwd) * win
        val = jnp.where(row == L, 0.0, val)
        _store_slabs(kern_ref, i * (ch // N2), val, plan)
        return asum + jnp.sum(jnp.abs(val), axis=0, keepdims=True)

    asum = lax.fori_loop(0, n_ch, synth, jnp.zeros((1, tc), F32))
    scale = 1.0 / (asum * float(plan.n))

    _dft_stage1(kern_ref, f1_ref, a_ref, N1, plan)

    def stage2(k1, _):
        r0 = pl.multiple_of(k1 * 2 * N2, 2 * N2)
        slab = _load_slabs(a_ref, 2 * k1, 2, plan).astype(BF16)
        kf_ref[pl.ds(r0, 2 * N2), :] = (_dot(f2_ref[...], slab) * scale).astype(kf_ref.dtype)
        return 0

    lax.fori_loop(0, K1, stage2, 0, unroll=STAGE2_UNROLL)
    pad = plan.R - 2 * K1
    if pad:
        kf_ref[pl.ds(2 * K1 * N2, pad * N2), :] = jnp.zeros((pad * N2, tc), kf_ref.dtype)


def _hyena_filter_spectrum(L, w1, b1, fr1, w2, b2, fr2, w3, tc):
    plan = _FftPlan(L)
    C = w3.shape[1] // 2
    zz = _filter_features(L)
    f1, _, f2, _ = _fft_consts(L)
    w1p = jnp.zeros((LANES, FILTER_ORDER), BF16).at[:FILTER_EMB].set(w1.astype(BF16))
    ch = min(CONV_CHUNK, L)
    rows_a = plan.R * plan.N2
    cblock = lambda off: pl.BlockSpec((FILTER_ORDER, tc), lambda c: (0, off + c))
    vmem = (2 * L * LANES * 2 * 2 + 2 * L * tc * 4 + rows_a * tc * 4 + 2 * rows_a * tc * 2
            + f1.size * 2 * 2 + 8 * ch * tc * 4)
    return pl.pallas_call(
        functools.partial(_filter_kernel, plan=plan, ch=ch),
        grid=(C // tc,),
        in_specs=[_const_spec(zz.shape),
                  _const_spec(w1p.shape), _const_spec((1, FILTER_ORDER)), _const_spec((1, FILTER_ORDER)),
                  _const_spec((FILTER_ORDER, FILTER_ORDER)), _const_spec((1, FILTER_ORDER)),
                  _const_spec((1, FILTER_ORDER)),
                  cblock(0), cblock(C // tc),
                  pl.BlockSpec((1, tc), lambda c: (0, c)),
                  _const_spec(f1.shape), _const_spec(f2.shape)],
        out_specs=pl.BlockSpec((rows_a, tc), lambda c: (0, c)),
        out_shape=jax.ShapeDtypeStruct((rows_a, C), BF16),
        scratch_shapes=[pltpu.VMEM((2 * L, FILTER_ORDER), BF16),
                        pltpu.VMEM((tc // LANES, plan.N1 * plan.P, LANES), F32),
                        pltpu.VMEM((tc // LANES, plan.R * plan.P, LANES), F32)],
        compiler_params=_params(("arbitrary",), vmem),
        name="hyena_filter",
    )(zz, w1p, b1, fr1, w2.astype(BF16), b2, fr2, w3.astype(BF16), w3.astype(BF16),
      _decay_rates(C), f1, f2)


def _short_conv_chunk(raw_ref, w_ref, b_ref, r0, ch, L):
    halo = BF16_SUBLANES
    c = raw_ref[0, pl.ds(r0, ch), :].astype(F32)
    p0 = pl.multiple_of(jnp.maximum(r0 - halo, 0), halo)
    n0 = pl.multiple_of(jnp.minimum(r0 + ch, L - halo), halo)
    prev = raw_ref[0, pl.ds(p0, halo), :].astype(F32)[halo - 1:halo]
    nxt = raw_ref[0, pl.ds(n0, halo), :].astype(F32)[0:1]
    prev = jnp.where(r0 == 0, 0.0, prev)
    nxt = jnp.where(r0 + ch == L, 0.0, nxt)
    rows = lax.broadcasted_iota(jnp.int32, c.shape, 0)
    up = jnp.where(rows == 0, prev, pltpu.roll(c, 1, 0))
    un = jnp.where(rows == ch - 1, nxt, pltpu.roll(c, ch - 1, 0))
    w = w_ref[...]
    return w[0:1] * up + w[1:2] * c + w[2:3] * un + b_ref[...]


def _hyena_kernel(x0_ref, x1_ref, vh_ref, w0_ref, w1_ref, wv_ref, b0_ref, b1_ref, bv_ref,
                  d_ref, kf_ref, f1_ref, g1_ref, f2_ref, f2i_ref, o_ref,
                  u_ref, a_ref, y_ref, *, plan, ch):
    L, N1, N2, K1, R = plan.L, plan.N1, plan.N2, plan.K1, plan.R
    n_ch = L // ch

    def gate_in(i, _):
        r0 = pl.multiple_of(i * ch, ch)
        x1 = _short_conv_chunk(x1_ref, w1_ref, b1_ref, r0, ch, L)
        vh = _short_conv_chunk(vh_ref, wv_ref, bv_ref, r0, ch, L)
        _store_slabs(u_ref, i * (ch // N2), x1 * vh, plan)
        return 0

    lax.fori_loop(0, n_ch, gate_in, 0)

    _dft_stage1(u_ref, f1_ref, a_ref, N1 // 2, plan)

    tc = o_ref.shape[2]
    group = max(1, MXU_WIDTH // tc)

    def spectral_group(first, count):
        x = _dot(f2_ref[...], jnp.concatenate(
            [_load_slabs(a_ref, 2 * (first + i), 2, plan) for i in range(count)], axis=1).astype(BF16))
        ys = []
        for i in range(count):
            r0 = pl.multiple_of((first + i) * 2 * N2, 2 * N2)
            xr, xi = x[:N2, i * tc:(i + 1) * tc], x[N2:, i * tc:(i + 1) * tc]
            kr = kf_ref[pl.ds(r0, N2), :].astype(F32)
            ki = kf_ref[pl.ds(r0 + N2, N2), :].astype(F32)
            ys.append(jnp.concatenate([xr * kr - xi * ki, xr * ki + xi * kr], axis=0))
        z = _dot(f2i_ref[...], jnp.concatenate(ys, axis=1).astype(BF16))
        for i in range(count):
            _store_slabs(a_ref, 2 * (first + i), z[:, i * tc:(i + 1) * tc], plan)

    def spectral(i, _):
        spectral_group(i * group, group)
        return 0

    lax.fori_loop(0, K1 // group, spectral, 0, unroll=STAGE2_UNROLL // group)
    for k1 in range(K1 - K1 % group, K1):
        spectral_group(k1, 1)

    def inverse1(t2, _):
        rows = _load_rows(a_ref, pl.ds(t2, R, stride=plan.P)).astype(BF16)
        _store_rows(y_ref, pl.ds(t2, N1 // 2, stride=plan.P), _dot(g1_ref[t2], rows))
        return 0

    lax.fori_loop(0, N2, inverse1, 0, unroll=STAGE1_UNROLL)

    def gate_out(i, _):
        r0 = pl.multiple_of(i * ch, ch)
        x0 = _short_conv_chunk(x0_ref, w0_ref, b0_ref, r0, ch, L)
        u = _load_slabs(u_ref, i * (ch // N2), ch // N2, plan)
        y = _load_slabs(y_ref, i * (ch // N2), ch // N2, plan)
        o_ref[0, pl.ds(r0, ch), :] = ((y + d_ref[...] * u) * x0).astype(o_ref.dtype)
        return 0

    lax.fori_loop(0, n_ch, gate_out, 0)


def _hyena_vmem_bytes(plan, tc, ch):
    L, rows_a = plan.L, plan.R * plan.N2
    pad_l = lambda m: _round_up(m, LANES)
    return (6 * L * tc * 2 + rows_a * tc * 2 + 2 * L * tc * 4 + rows_a * tc * 4 + 2 * L * tc * 2
            + plan.N2 * plan.R * pad_l(plan.N1 // 2) * 2 + plan.N2 * (plan.N1 // 2) * pad_l(plan.R) * 2
            + 16 * ch * tc * 4 + 16 * plan.N2 * tc * 4)


def _hyena_operator(hy, conv_w, conv_b, d, kf):
    B, L, W3 = hy.shape
    C = W3 // 3
    plan = _FftPlan(L)
    ch = min(CONV_CHUNK, L)
    tc = 2 * LANES if _vmem_limit(_hyena_vmem_bytes(plan, 2 * LANES, ch)) <= VMEM_LIMIT_CAP else LANES
    nb = C // tc
    f1full, g1, f2, f2i = _fft_consts(L)
    f1 = f1full[:, :, :plan.N1 // 2]
    rows_a = plan.R * plan.N2
    seq = lambda part: pl.BlockSpec((1, L, tc), lambda c, b: (b, 0, part * nb + c))
    wspec = lambda part: pl.BlockSpec((SHORT_CONV, tc), lambda c, b: (0, part * nb + c))
    bspec = lambda part: pl.BlockSpec((1, tc), lambda c, b: (0, part * nb + c))
    vmem = _hyena_vmem_bytes(plan, tc, ch)
    return pl.pallas_call(
        functools.partial(_hyena_kernel, plan=plan, ch=ch),
        grid=(nb, B),
        in_specs=[seq(0), seq(1), seq(2), wspec(0), wspec(1), wspec(2), bspec(0), bspec(1), bspec(2),
                  pl.BlockSpec((1, tc), lambda c, b: (0, c)),
                  pl.BlockSpec((rows_a, tc), lambda c, b: (0, c), pipeline_mode=pl.Buffered(1)),
                  _const_spec(f1.shape), _const_spec(g1.shape), _const_spec(f2.shape), _const_spec(f2i.shape)],
        out_specs=pl.BlockSpec((1, L, tc), lambda c, b: (b, 0, c)),
        out_shape=jax.ShapeDtypeStruct((B, L, C), BF16),
        scratch_shapes=[pltpu.VMEM((tc // LANES, (plan.N1 // 2) * plan.P, LANES), F32),
                        pltpu.VMEM((tc // LANES, plan.R * plan.P, LANES), F32),
                        pltpu.VMEM((tc // LANES, (plan.N1 // 2) * plan.P, LANES), F32)],
        compiler_params=_params(("parallel", "parallel"), vmem),
        name="hyena_op",
    )(hy, hy, hy, conv_w, conv_w, conv_w, conv_b, conv_b, conv_b, d, kf, f1, g1, f2, f2i)


def _memkv_kernel(m_ref, g_ref, wk_ref, wv_ref, k_ref, v_ref):
    mn = _rms(m_ref[...], g_ref[...]).astype(BF16)
    k_ref[...] = _dot(mn, wk_ref[...]).astype(BF16)
    v_ref[...] = _dot(mn, wv_ref[...]).astype(BF16)


def _memkv(mem, g, wk, wv, tm):
    T, D = mem.shape
    row = pl.BlockSpec((tm, D), lambda i: (i, 0))
    vmem = 2 * tm * D * 4 + 2 * D * D * 2 + 4 * tm * D * 2 + 3 * tm * D * 4
    return pl.pallas_call(
        _memkv_kernel,
        grid=(T // tm,),
        in_specs=[row, _const_spec((1, D)), _const_spec((D, D)), _const_spec((D, D))],
        out_specs=[row, row],
        out_shape=[jax.ShapeDtypeStruct((T, D), BF16)] * 2,
        compiler_params=_params(("parallel",), vmem),
        name="mem_kv",
    )(mem, g, wk, wv)


def _mixout_cross_kernel(a_ref, y_ref, x_ref, wa_ref, wy_ref, gmix_ref, k_ref, v_ref,
                         gpre_ref, wq_ref, wo_ref, gpost_ref, o_ref, *, q_scale):
    z = _dot(a_ref[...], wa_ref[...]) + _dot(y_ref[...], wy_ref[...])
    x = x_ref[...] + _rms(z, gmix_ref[...])
    xn = _rms(x, gpre_ref[...]).astype(BF16)
    q = (_dot(xn, wq_ref[...]) * q_scale).astype(BF16)
    hd = q.shape[1] // X_HEADS
    outs = []
    for h in range(X_HEADS):
        sl = slice(h * hd, (h + 1) * hd)
        s = _dot_nt(q[:, sl], k_ref[0, :, sl])
        p = jnp.exp2(s - jnp.max(s, axis=-1, keepdims=True))
        l = jnp.sum(p, axis=-1, keepdims=True)
        outs.append((_dot(p.astype(BF16), v_ref[0, :, sl]) / l).astype(BF16))
    z = _dot(jnp.concatenate(outs, axis=1), wo_ref[...])
    o_ref[...] = x + _rms(z, gpost_ref[...])


def _mixout_cross_block(attn, y, x, w_out, gmix, k, v, gpre, wq, wo, gpost, L, tm):
    T, D = x.shape
    a, c = attn.shape[1], y.shape[1]
    n_mem = k.shape[1]
    blocks_per_seq = L // tm
    row = lambda width: pl.BlockSpec((tm, width), lambda i: (i, 0))
    kv = pl.BlockSpec((1, n_mem, D), lambda i: (i // blocks_per_seq, 0, 0))
    vmem = (2 * tm * (a + c) * 2 + 4 * tm * D * 4 + 4 * n_mem * D * 2 + (a + c + 2 * D) * D * 2
            + 8 * tm * D * 4)
    q_scale = (D // X_HEADS) ** -0.5 * LOG2E
    return pl.pallas_call(
        functools.partial(_mixout_cross_kernel, q_scale=q_scale),
        grid=(T // tm,),
        in_specs=[row(a), row(c), row(D), _const_spec((a, D)), _const_spec((c, D)), _const_spec((1, D)),
                  kv, kv, _const_spec((1, D)), _const_spec((D, D)), _const_spec((D, D)), _const_spec((1, D))],
        out_specs=row(D),
        out_shape=jax.ShapeDtypeStruct((T, D), F32),
        compiler_params=_params(("parallel",), vmem),
        name="mixout_cross",
    )(attn, y, x, w_out[:a], w_out[a:], gmix, k, v, gpre, wq, wo, gpost)


def _swiglu_kernel(x_ref, gpre_ref, wg_ref, wu_ref, wd_ref, gpost_ref, o_ref, *, n_split):
    x = x_ref[...]
    xn = _rms(x, gpre_ref[...]).astype(BF16)
    ff = wg_ref.shape[1]
    cw = ff // n_split
    z = None
    for c in range(n_split):
        sl = slice(c * cw, (c + 1) * cw)
        gate = _dot(xn, wg_ref[:, sl])
        up = _dot(xn, wu_ref[:, sl])
        h = (gate * (1.0 / (1.0 + jnp.exp(-gate))) * up).astype(BF16)
        part = _dot(h, wd_ref[sl, :])
        z = part if z is None else z + part
    o_ref[...] = x + _rms(z, gpost_ref[...])


def _swiglu_block(x, gpre, wg, wu, wd, gpost, tm):
    T, D = x.shape
    ff = wg.shape[1]
    n_split = ff // MXU_WIDTH if ff % MXU_WIDTH == 0 else 1
    row = pl.BlockSpec((tm, D), lambda i: (i, 0))
    vmem = 4 * tm * D * 4 + 3 * D * ff * 2 + 4 * tm * (ff // n_split) * 4 + 4 * tm * D * 4
    return pl.pallas_call(
        functools.partial(_swiglu_kernel, n_split=n_split),
        grid=(T // tm,),
        in_specs=[row, _const_spec((1, D)), _const_spec((D, ff)), _const_spec((D, ff)),
                  _const_spec((ff, D)), _const_spec((1, D))],
        out_specs=row,
        out_shape=jax.ShapeDtypeStruct((T, D), F32),
        compiler_params=_params(("parallel",), vmem),
        name="swiglu",
    )(x, gpre, wg, wu, wd, gpost)


def _trunk(x, mem, P):
    B, L, D = x.shape
    depth = P['w_in'].shape[0]
    n_mem = mem.shape[1]
    tm = min(ROW_TILE, L)
    tq, tk = tm, min(KEY_CHUNK, L)
    tc = LANES
    a = DA_HEADS * 2 * DA_HEAD_DIM
    tables = _rope_tables(L)
    xf = x.reshape(B * L, D)
    memf = mem.reshape(B * n_mem, D)
    row = lambda v: v.reshape(1, -1)
    for l in range(depth):
        lam_init = 0.8 - 0.6 * math.exp(-0.3 * l)
        qt, k, vt, hy = _inproj(xf, row(P['ln_mix_pre'][l]), P['w_in'][l].astype(BF16), tables, B, L, tm, tk)
        lam_params = jnp.stack([P['lambda_q1'][l], P['lambda_k1'][l], P['lambda_q2'][l], P['lambda_k2'][l],
                                jnp.full((DA_HEAD_DIM,), lam_init, F32)])
        attn = _diff_attention(lam_params, qt, k.reshape(B, L, a), vt, P['subln_g'][l], tq, tk)
        kf = _hyena_filter_spectrum(L, P['filt_w1'][l], row(P['filt_b1'][l]), row(P['filt_freq1'][l]),
                                    P['filt_w2'][l], row(P['filt_b2'][l]), row(P['filt_freq2'][l]),
                                    P['filt_w3'][l], tc)
        y = _hyena_operator(hy.reshape(B, L, -1), P['conv_w'][l], row(P['conv_b'][l]),
                            row(P['hyena_d'][l]), kf)
        km, vm = _memkv(memf, row(P['ln_mem'][l]), P['wk_x'][l].astype(BF16), P['wv_x'][l].astype(BF16),
                        min(ROW_TILE, B * n_mem))
        xf = _mixout_cross_block(attn.reshape(B * L, a), y.reshape(B * L, -1), xf, P['w_out'][l].astype(BF16),
                                 row(P['ln_mix_post'][l]), km.reshape(B, n_mem, D), vm.reshape(B, n_mem, D),
                                 row(P['ln_x_pre'][l]), P['wq_x'][l].astype(BF16), P['wo_x'][l].astype(BF16),
                                 row(P['ln_x_post'][l]), L, min(MIXOUT_ROW_TILE, L))
        xf = _swiglu_block(xf, row(P['ln_ffn_pre'][l]), P['w_gate'][l].astype(BF16), P['w_up'][l].astype(BF16),
                           P['w_down'][l].astype(BF16), row(P['ln_ffn_post'][l]), tm)
    return xf.reshape(B, L, D)


def kernel(x_prompt, x_sample, mem_prompt, mem_sample, ln_mix_pre, ln_mix_post, w_in, lambda_q1, lambda_k1, lambda_q2, lambda_k2, subln_g, conv_w, conv_b, filt_w1, filt_b1, filt_freq1, filt_w2, filt_b2, filt_freq2, filt_w3, hyena_d, w_out, ln_x_pre, ln_x_post, ln_mem, wq_x, wk_x, wv_x, wo_x, ln_ffn_pre, ln_ffn_post, w_gate, w_up, w_down):
    P = dict(ln_mix_pre=ln_mix_pre, ln_mix_post=ln_mix_post, w_in=w_in,
             lambda_q1=lambda_q1, lambda_k1=lambda_k1, lambda_q2=lambda_q2, lambda_k2=lambda_k2,
             subln_g=subln_g, conv_w=conv_w, conv_b=conv_b,
             filt_w1=filt_w1, filt_b1=filt_b1, filt_freq1=filt_freq1,
             filt_w2=filt_w2, filt_b2=filt_b2, filt_freq2=filt_freq2, filt_w3=filt_w3,
             hyena_d=hyena_d, w_out=w_out,
             ln_x_pre=ln_x_pre, ln_x_post=ln_x_post, ln_mem=ln_mem,
             wq_x=wq_x, wk_x=wk_x, wv_x=wv_x, wo_x=wo_x,
             ln_ffn_pre=ln_ffn_pre, ln_ffn_post=ln_ffn_post,
             w_gate=w_gate, w_up=w_up, w_down=w_down)
    return (_trunk(x_prompt, mem_prompt, P), _trunk(x_sample, mem_sample, P))
```

```python
import functools
import math

import numpy as np
import jax
import jax.numpy as jnp
from jax import lax
from jax.experimental import pallas as pl
from jax.experimental.pallas import tpu as pltpu

F32 = jnp.float32
BF16 = jnp.bfloat16

DA_HEADS = 4
DA_HEAD_DIM = 64
DA_V_DIM = 128
ROPE_DIM = 16
ROPE_THETA = 500000.0
SHORT_CONV = 3
FILTER_EMB = 33
FILTER_ORDER = 64
FAST_DECAY_PCT = 0.3
SLOW_DECAY_PCT = 1.5
DECAY_TARGET = 1e-2
X_HEADS = 4
EPS = 1e-6
LOG2E = 1.4426950408889634

LANES = 128
SUBLANES = 8
BF16_SUBLANES = 16
MXU_WIDTH = 256
VMEM_LIMIT_CAP = 58 * 2**20
VMEM_TEMP_FACTOR = 1.25
VMEM_TEMP_BYTES = 4 * 2**20

ROW_TILE = 512
MIXOUT_ROW_TILE = 1024
KEY_CHUNK = 1024
KEY_BLOCK = 512
CONV_CHUNK = 512
ATTN_PASSES_PER_STEP = 32

STAGE1_UNROLL = 64
STAGE2_UNROLL = 32


def _round_up(a, b):
    return -(-a // b) * b


def _vmem_limit(vmem_bytes):
    return int(vmem_bytes * VMEM_TEMP_FACTOR) + VMEM_TEMP_BYTES


def _params(semantics, vmem_bytes):
    limit = min(_vmem_limit(vmem_bytes), VMEM_LIMIT_CAP)
    return pltpu.CompilerParams(dimension_semantics=semantics, vmem_limit_bytes=limit)


def _const_spec(shape):
    nd = len(shape)
    return pl.BlockSpec(shape, lambda *_: (0,) * nd, pipeline_mode=pl.Buffered(1))


def _rms(x, g):
    ms = jnp.mean(x * x, axis=-1, keepdims=True)
    return x * lax.rsqrt(ms + EPS) * g


def _dot(a, b):
    return jnp.dot(a, b, preferred_element_type=F32)


def _dot_nt(a, b):
    return lax.dot_general(a, b, (((1,), (1,)), ((), ())), preferred_element_type=F32)


def _rope_tables(L):
    inv = ROPE_THETA ** (-np.arange(0, ROPE_DIM, 2, dtype=np.float64) / ROPE_DIM)
    ang = np.arange(L, dtype=np.float64)[:, None] * inv[None, :]
    half = ROPE_DIM // 2
    cos = np.ones((L, LANES)); sa = np.zeros((L, LANES)); sb = np.zeros((L, LANES))
    for g in range(LANES // DA_HEAD_DIM):
        o = g * DA_HEAD_DIM
        cos[:, o:o + half] = np.cos(ang)
        cos[:, o + half:o + ROPE_DIM] = np.cos(ang)
        sb[:, o:o + half] = -np.sin(ang)
        sa[:, o + half:o + ROPE_DIM] = np.sin(ang)
    return (jnp.asarray(cos, F32), jnp.asarray(sa, F32), jnp.asarray(sb, F32),
            jnp.asarray(np.cos(ang).T, F32), jnp.asarray(np.sin(ang).T, F32))


def _inproj_kernel(x_ref, g_ref, w_ref, wqt_ref, wvt_ref, cos_ref, sa_ref, sb_ref, cost_ref, sint_ref,
                   qt_ref, k_ref, vt_ref, hy_ref, *, q_scale):
    xn = _rms(x_ref[...], g_ref[...]).astype(BF16)
    cos, sa, sb = cos_ref[...], sa_ref[...], sb_ref[...]
    half = ROPE_DIM // 2

    def rope(y):
        parts = []
        for s in range(y.shape[1] // LANES):
            ys = y[:, s * LANES:(s + 1) * LANES]
            parts.append(ys * cos + pltpu.roll(ys, half, 1) * sa
                         + pltpu.roll(ys, LANES - half, 1) * sb)
        return jnp.concatenate(parts, axis=1)

    def rope_t(y):
        ct, st = cost_ref[...], sint_ref[...]
        parts = []
        for o in range(0, y.shape[0], DA_HEAD_DIM):
            x1, x2 = y[o:o + half], y[o + half:o + ROPE_DIM]
            parts += [x1 * ct - x2 * st, x2 * ct + x1 * st, y[o + ROPE_DIM:o + DA_HEAD_DIM]]
        return jnp.concatenate(parts, axis=0)

    a = k_ref.shape[1]
    qt_ref[0, 0] = (rope_t(_dot_nt(wqt_ref[...], xn)) * q_scale).astype(BF16)
    k_ref[...] = rope(_dot(xn, w_ref[:, a:2 * a])).astype(BF16)
    vt_ref[0, 0] = _dot_nt(wvt_ref[...], xn).astype(BF16)
    hy_ref[...] = _dot(xn, w_ref[:, 3 * a:]).astype(BF16)


def _inproj(x, g, w, tables, B, L, tm, tk):
    T, D = x.shape
    n_out = w.shape[1]
    a = DA_HEADS * 2 * DA_HEAD_DIM
    hyw = n_out - 3 * a
    blocks_per_seq = L // tm
    tiles_per_chunk = tk // tm
    tab_spec = pl.BlockSpec((tm, LANES), lambda i: (i % blocks_per_seq, 0))
    tabt_spec = pl.BlockSpec((ROPE_DIM // 2, tm), lambda i: (0, i % blocks_per_seq))
    row = lambda width: pl.BlockSpec((tm, width), lambda i: (i, 0))
    qt_spec = pl.BlockSpec((1, 1, a, tm), lambda i: (i // blocks_per_seq, i % blocks_per_seq, 0, 0))
    vt_spec = pl.BlockSpec((1, 1, a, tm), lambda i: (i // blocks_per_seq, (i % blocks_per_seq) // tiles_per_chunk,
                                                     0, i % tiles_per_chunk))
    vmem = 2 * tm * D * 4 + D * (n_out + 2 * a) * 2 + 8 * tm * LANES * 4 + 2 * tm * n_out * 2 + 4 * tm * 1536 * 4
    q_scale = DA_HEAD_DIM ** -0.5 * LOG2E
    wqt, wvt = w[:, 0:a].T, w[:, 2 * a:3 * a].T
    return pl.pallas_call(
        functools.partial(_inproj_kernel, q_scale=q_scale),
        grid=(T // tm,),
        in_specs=[row(D), _const_spec((1, D)), _const_spec((D, n_out)), _const_spec((a, D)), _const_spec((a, D)),
                  tab_spec, tab_spec, tab_spec, tabt_spec, tabt_spec],
        out_specs=[qt_spec, row(a), vt_spec, row(hyw)],
        out_shape=[jax.ShapeDtypeStruct((B, L // tm, a, tm), BF16), jax.ShapeDtypeStruct((T, a), BF16),
                   jax.ShapeDtypeStruct((B, L // tk, a, tk), BF16), jax.ShapeDtypeStruct((T, hyw), BF16)],
        compiler_params=_params(("parallel",), vmem),
        name="inproj",
    )(x, g, w, wqt, wvt, *tables)


def _attn_kernel(lam_ref, qt_ref, k_ref, vt_ref, g_ref, o_ref, s_ref, *, tk, kb):
    nt, _, tq = qt_ref.shape[1:]
    L = k_ref.shape[1]
    feat = lax.broadcasted_iota(jnp.int32, (2 * DA_HEAD_DIM, tq), 0)

    def split(qt):
        zero = jnp.zeros_like(qt)
        return (jnp.where(feat < DA_HEAD_DIM, qt, zero), jnp.where(feat >= DA_HEAD_DIM, qt, zero))

    qz = [split(qt_ref[0, t]) for t in range(nt)]

    lp = lam_ref[...]
    lam_init = lp[4:5, 0:1]
    lam = (jnp.exp(jnp.sum(lp[0:1] * lp[1:2], axis=-1, keepdims=True))
           - jnp.exp(jnp.sum(lp[2:3] * lp[3:4], axis=-1, keepdims=True)) + lam_init)
    g = jnp.concatenate([g_ref[...]] * (tq // LANES), axis=1)

    nk, nb = L // tk, tk // kb
    neg = jnp.full((1, tq), -1e30, F32)
    fresh = (neg, jnp.zeros((1, tq), F32), jnp.zeros((DA_V_DIM, tq), F32))

    def score_block(t, j, c, b, mx):
        s = _dot(k_ref[0, j * tk + b * kb:j * tk + (b + 1) * kb, :], qz[t][c])
        s_ref[c, b * kb:(b + 1) * kb, :] = s
        return jnp.maximum(mx, jnp.max(s, axis=0, keepdims=True))

    def value_block(j, c, b, mn, lsum, pv):
        p = jnp.exp2(s_ref[c, b * kb:(b + 1) * kb, :] - mn)
        lsum = lsum + jnp.sum(p, axis=0, keepdims=True)
        pv = pv + _dot(vt_ref[0, j, :, b * kb:(b + 1) * kb], p.astype(BF16))
        return lsum, pv

    def step(score_of, value_of, mx_cur, state):
        m, l, acc = state
        mn = jnp.maximum(m, mx_cur)
        alpha = jnp.exp2(m - mn)
        lsum, pv, mx = jnp.zeros((1, tq), F32), jnp.zeros((DA_V_DIM, tq), F32), neg
        for b in range(nb):
            if score_of is not None:
                mx = score_block(*score_of, b, mx)
            lsum, pv = value_block(*value_of[1:], b, mn, lsum, pv)
        return mx, (mn, alpha * l + lsum, alpha * acc + pv)

    def finish(t, st0, st1):
        (_, l0, a0), (_, l1, a1) = st0, st1
        o = a0 / l0 - lam * (a1 / l1)
        ms = jnp.mean(o * o, axis=0, keepdims=True)
        o = o * lax.rsqrt(ms + EPS) * g * (1.0 - lam_init)
        o_ref[0, t * tq:(t + 1) * tq, :] = o.T.astype(BF16)

    passes = [(t, j, c) for t in range(nt) for j in range(nk) for c in range(2)]
    mx = neg
    for b in range(nb):
        mx = score_block(*passes[0], b, mx)
    states = {}
    for n, cur in enumerate(passes):
        t, j, c = cur
        nxt = passes[n + 1] if n + 1 < len(passes) else None
        mx, states[(t, c)] = step(nxt, cur, mx, states.get((t, c), fresh))
        if j == nk - 1 and c == 1:
            finish(t, states.pop((t, 0)), states.pop((t, 1)))


def _diff_attention(lam_params, qt, k, vt, g, tq, tk):
    B, L, A = k.shape
    nt = max(1, min(L // tq, ATTN_PASSES_PER_STEP // (2 * (L // tk))))
    qspec = pl.BlockSpec((1, nt, LANES, tq), lambda b, h, i: (b, i, h, 0))
    ospec = pl.BlockSpec((1, nt * tq, LANES), lambda b, h, i: (b, i, h))
    kspec = pl.BlockSpec((1, L, LANES), lambda b, h, i: (b, 0, h))
    vspec = pl.BlockSpec((1, L // tk, LANES, tk), lambda b, h, i: (b, 0, h, 0))
    g_cols = jnp.broadcast_to(g.reshape(DA_V_DIM, 1), (DA_V_DIM, LANES))
    kb = min(KEY_BLOCK, tk)
    vmem = (4 * L * LANES * 2 + 4 * nt * tq * LANES * 2 + 2 * tq * tk * 4 + 4 * tq * kb * 6
            + 16 * tq * LANES * 4)
    return pl.pallas_call(
        functools.partial(_attn_kernel, tk=tk, kb=kb),
        grid=(B, DA_HEADS, L // (nt * tq)),
        in_specs=[_const_spec(lam_params.shape), qspec, kspec, vspec, _const_spec((DA_V_DIM, LANES))],
        out_specs=ospec,
        out_shape=jax.ShapeDtypeStruct((B, L, A), BF16),
        scratch_shapes=[pltpu.VMEM((2, tk, tq), F32)],
        compiler_params=_params(("parallel", "parallel", "parallel"), vmem),
        name="diff_attn",
    )(lam_params, qt, k, vt, g_cols)


class _FftPlan:
    def __init__(self, L):
        n = 2 * L
        lg = int(math.log2(n))
        assert 2 ** lg == n
        self.L, self.n = L, n
        self.N1 = 2 ** (lg // 2)
        self.N2 = n // self.N1
        self.K1 = self.N1 // 2 + 1
        self.R = _round_up(2 * self.K1, BF16_SUBLANES)
        self.P = self.N2 + SUBLANES
        assert (self.P // SUBLANES) % 2 == 1
        assert self.N2 % BF16_SUBLANES == 0 and (self.N1 // 2) % BF16_SUBLANES == 0


@functools.lru_cache(maxsize=None)
def _fft_consts(L):
    p = _FftPlan(L)
    N1, N2, K1, R, n = p.N1, p.N2, p.K1, p.R, p.n
    t2 = np.arange(N2, dtype=np.float64)[:, None, None]
    k1 = np.arange(K1, dtype=np.float64)[None, :, None]
    t1 = np.arange(N1, dtype=np.float64)[None, None, :]
    ang = -2.0 * np.pi * (t1 * k1 / N1 + t2 * k1 / n)
    f1 = np.zeros((N2, R, N1))
    f1[:, 0:2 * K1:2, :] = np.cos(ang)
    f1[:, 1:2 * K1:2, :] = np.sin(ang)
    c = np.where((np.arange(K1) == 0) | (np.arange(K1) == N1 // 2), 1.0, 2.0)[None, :, None]
    th = -ang[:, :, :N1 // 2]
    g1 = np.zeros((N2, N1 // 2, R))
    g1[:, :, 0:2 * K1:2] = np.transpose(c * np.cos(th), (0, 2, 1))
    g1[:, :, 1:2 * K1:2] = np.transpose(-c * np.sin(th), (0, 2, 1))
    a2 = 2.0 * np.pi * np.outer(np.arange(N2), np.arange(N2)) / N2
    C, S = np.cos(a2), np.sin(a2)
    f2 = np.block([[C, S], [-S, C]])
    return (jnp.asarray(f1, BF16), jnp.asarray(g1, BF16), jnp.asarray(f2, BF16), jnp.asarray(f2.T, BF16))


def _load_rows(ref, idx):
    parts = [ref[g, idx, :] for g in range(ref.shape[0])]
    return parts[0] if len(parts) == 1 else jnp.concatenate(parts, axis=1)


def _store_rows(ref, idx, val):
    for g in range(ref.shape[0]):
        ref[g, idx, :] = val[:, g * LANES:(g + 1) * LANES]


def _store_slabs(ref, first_slab, val, plan):
    for s in range(val.shape[0] // plan.N2):
        start = pl.multiple_of((first_slab + s) * plan.P, SUBLANES)
        _store_rows(ref, pl.ds(start, plan.N2), val[s * plan.N2:(s + 1) * plan.N2])


def _load_slabs(ref, first_slab, count, plan):
    parts = [_load_rows(ref, pl.ds(pl.multiple_of((first_slab + s) * plan.P, SUBLANES), plan.N2))
             for s in range(count)]
    return parts[0] if count == 1 else jnp.concatenate(parts, axis=0)


def _dft_stage1(src_ref, f1_ref, a_ref, n_slabs, plan):
    def body(t2, _):
        rows = _load_rows(src_ref, pl.ds(t2, n_slabs, stride=plan.P)).astype(BF16)
        _store_rows(a_ref, pl.ds(t2, plan.R, stride=plan.P), _dot(f1_ref[t2], rows))
        return 0

    lax.fori_loop(0, plan.N2, body, 0, unroll=STAGE1_UNROLL)


@functools.lru_cache(maxsize=None)
def _filter_features(L):
    t = np.linspace(0.0, 1.0, L)[:, None]
    bands = (FILTER_EMB - 1) // 2
    w = 2.0 * np.pi * np.arange(L)[:, None] / L
    f = np.linspace(1e-4, bands - 1, bands)[None, :]
    z = np.concatenate([t, np.cos(f * w), -np.sin(f * w)], axis=-1)
    z_rev = np.concatenate([z[:1], z[:0:-1]], axis=0)
    zz = np.zeros((2 * L, LANES))
    zz[:L, :FILTER_EMB] = z
    zz[L:, :FILTER_EMB] = z_rev
    return jnp.asarray(zz, BF16)


def _decay_rates(width):
    max_decay = math.log(DECAY_TARGET) / FAST_DECAY_PCT
    min_decay = math.log(DECAY_TARGET) / SLOW_DECAY_PCT
    return jnp.asarray(np.abs(np.linspace(min_decay, max_decay, width))[None, :], F32)


def _filter_kernel(z_ref, w1_ref, b1_ref, fr1_ref, w2_ref, b2_ref, fr2_ref,
                   w3f_ref, w3b_ref, dl_ref, f1_ref, f2_ref, kf_ref,
                   h_ref, kern_ref, a_ref, *, plan, ch):
    L, N1, N2, K1 = plan.L, plan.N1, plan.N2, plan.K1
    n_ch = (2 * L) // ch

    @pl.when(pl.program_id(0) == 0)
    def _():
        def mlp(i, _):
            r0 = pl.multiple_of(i * ch, ch)
            h = jnp.sin(fr1_ref[...] * (_dot(z_ref[pl.ds(r0, ch), :], w1_ref[...]) + b1_ref[...]))
            h = jnp.sin(fr2_ref[...] * (_dot(h.astype(BF16), w2_ref[...]) + b2_ref[...]))
            h_ref[pl.ds(r0, ch), :] = h.astype(BF16)
            return 0
        lax.fori_loop(0, n_ch, mlp, 0)

    tc = kf_ref.shape[1]
    local = lax.broadcasted_iota(jnp.int32, (ch, tc), 0)

    def synth(i, asum):
        r0 = pl.multiple_of(i * ch, ch)
        row = local + r0
        h = h_ref[pl.ds(r0, ch), :]
        fwd = _dot(h, w3f_ref[...])
        bwd = _dot(h, w3b_ref[...])
        lag = jnp.where(row < L, row, 2 * L - row).astype(F32)
        win = jnp.exp(lag * (-1.0 / (L - 1)) * dl_ref[...])
        val = jnp.where(row < L, fwd + jnp.where(row == 0, bwd, 0.0), bwd) * win
        val = jnp.where(row == L, 0.0, val)
        _store_slabs(kern_ref, i * (ch // N2), val, plan)
        return asum + jnp.sum(jnp.abs(val), axis=0, keepdims=True)

    asum = lax.fori_loop(0, n_ch, synth, jnp.zeros((1, tc), F32))
    scale = 1.0 / (asum * float(plan.n))

    _dft_stage1(kern_ref, f1_ref, a_ref, N1, plan)

    def stage2(k1, _):
        r0 = pl.multiple_of(k1 * 2 * N2, 2 * N2)
        slab = _load_slabs(a_ref, 2 * k1, 2, plan).astype(BF16)
        kf_ref[pl.ds(r0, 2 * N2), :] = (_dot(f2_ref[...], slab) * scale).astype(kf_ref.dtype)
        return 0

    lax.fori_loop(0, K1, stage2, 0, unroll=STAGE2_UNROLL)
    pad = plan.R - 2 * K1
    if pad:
        kf_ref[pl.ds(2 * K1 * N2, pad * N2), :] = jnp.zeros((pad * N2, tc), kf_ref.dtype)


def _hyena_filter_spectrum(L, w1, b1, fr1, w2, b2, fr2, w3, tc):
    plan = _FftPlan(L)
    C = w3.shape[1] // 2
    zz = _filter_features(L)
    f1, _, f2, _ = _fft_consts(L)
    w1p = jnp.zeros((LANES, FILTER_ORDER), BF16).at[:FILTER_EMB].set(w1.astype(BF16))
    ch = min(CONV_CHUNK, L)
    rows_a = plan.R * plan.N2
    cblock = lambda off: pl.BlockSpec((FILTER_ORDER, tc), lambda c: (0, off + c))
    vmem = (2 * L * LANES * 2 * 2 + 2 * L * tc * 4 + rows_a * tc * 4 + 2 * rows_a * tc * 2
            + f1.size * 2 * 2 + 8 * ch * tc * 4)
    return pl.pallas_call(
        functools.partial(_filter_kernel, plan=plan, ch=ch),
        grid=(C // tc,),
        in_specs=[_const_spec(zz.shape),
                  _const_spec(w1p.shape), _const_spec((1, FILTER_ORDER)), _const_spec((1, FILTER_ORDER)),
                  _const_spec((FILTER_ORDER, FILTER_ORDER)), _const_spec((1, FILTER_ORDER)),
                  _const_spec((1, FILTER_ORDER)),
                  cblock(0), cblock(C // tc),
                  pl.BlockSpec((1, tc), lambda c: (0, c)),
                  _const_spec(f1.shape), _const_spec(f2.shape)],
        out_specs=pl.BlockSpec((rows_a, tc), lambda c: (0, c)),
        out_shape=jax.ShapeDtypeStruct((rows_a, C), BF16),
        scratch_shapes=[pltpu.VMEM((2 * L, FILTER_ORDER), BF16),
                        pltpu.VMEM((tc // LANES, plan.N1 * plan.P, LANES), F32),
                        pltpu.VMEM((tc // LANES, plan.R * plan.P, LANES), F32)],
        compiler_params=_params(("arbitrary",), vmem),
        name="hyena_filter",
    )(zz, w1p, b1, fr1, w2.astype(BF16), b2, fr2, w3.astype(BF16), w3.astype(BF16),
      _decay_rates(C), f1, f2)


def _short_conv_chunk(raw_ref, w_ref, b_ref, r0, ch, L):
    halo = BF16_SUBLANES
    c = raw_ref[0, pl.ds(r0, ch), :].astype(F32)
    p0 = pl.multiple_of(jnp.maximum(r0 - halo, 0), halo)
    n0 = pl.multiple_of(jnp.minimum(r0 + ch, L - halo), halo)
    prev = raw_ref[0, pl.ds(p0, halo), :].astype(F32)[halo - 1:halo]
    nxt = raw_ref[0, pl.ds(n0, halo), :].astype(F32)[0:1]
    prev = jnp.where(r0 == 0, 0.0, prev)
    nxt = jnp.where(r0 + ch == L, 0.0, nxt)
    rows = lax.broadcasted_iota(jnp.int32, c.shape, 0)
    up = jnp.where(rows == 0, prev, pltpu.roll(c, 1, 0))
    un = jnp.where(rows == ch - 1, nxt, pltpu.roll(c, ch - 1, 0))
    w = w_ref[...]
    return w[0:1] * up + w[1:2] * c + w[2:3] * un + b_ref[...]


def _hyena_kernel(x0_ref, x1_ref, vh_ref, w0_ref, w1_ref, wv_ref, b0_ref, b1_ref, bv_ref,
                  d_ref, kf_ref, f1_ref, g1_ref, f2_ref, f2i_ref, o_ref,
                  u_ref, a_ref, y_ref, *, plan, ch):
    L, N1, N2, K1, R = plan.L, plan.N1, plan.N2, plan.K1, plan.R
    n_ch = L // ch

    def gate_in(i, _):
        r0 = pl.multiple_of(i * ch, ch)
        x1 = _short_conv_chunk(x1_ref, w1_ref, b1_ref, r0, ch, L)
        vh = _short_conv_chunk(vh_ref, wv_ref, bv_ref, r0, ch, L)
        _store_slabs(u_ref, i * (ch // N2), x1 * vh, plan)
        return 0

    lax.fori_loop(0, n_ch, gate_in, 0)

    _dft_stage1(u_ref, f1_ref, a_ref, N1 // 2, plan)

    tc = o_ref.shape[2]
    group = max(1, MXU_WIDTH // tc)

    def spectral_group(first, count):
        x = _dot(f2_ref[...], jnp.concatenate(
            [_load_slabs(a_ref, 2 * (first + i), 2, plan) for i in range(count)], axis=1).astype(BF16))
        ys = []
        for i in range(count):
            r0 = pl.multiple_of((first + i) * 2 * N2, 2 * N2)
            xr, xi = x[:N2, i * tc:(i + 1) * tc], x[N2:, i * tc:(i + 1) * tc]
            kr = kf_ref[pl.ds(r0, N2), :].astype(F32)
            ki = kf_ref[pl.ds(r0 + N2, N2), :].astype(F32)
            ys.append(jnp.concatenate([xr * kr - xi * ki, xr * ki + xi * kr], axis=0))
        z = _dot(f2i_ref[...], jnp.concatenate(ys, axis=1).astype(BF16))
        for i in range(count):
            _store_slabs(a_ref, 2 * (first + i), z[:, i * tc:(i + 1) * tc], plan)

    def spectral(i, _):
        spectral_group(i * group, group)
        return 0

    lax.fori_loop(0, K1 // group, spectral, 0, unroll=STAGE2_UNROLL // group)
    for k1 in range(K1 - K1 % group, K1):
        spectral_group(k1, 1)

    def inverse1(t2, _):
        rows = _load_rows(a_ref, pl.ds(t2, R, stride=plan.P)).astype(BF16)
        _store_rows(y_ref, pl.ds(t2, N1 // 2, stride=plan.P), _dot(g1_ref[t2], rows))
        return 0

    lax.fori_loop(0, N2, inverse1, 0, unroll=STAGE1_UNROLL)

    def gate_out(i, _):
        r0 = pl.multiple_of(i * ch, ch)
        x0 = _short_conv_chunk(x0_ref, w0_ref, b0_ref, r0, ch, L)
        u = _load_slabs(u_ref, i * (ch // N2), ch // N2, plan)
        y = _load_slabs(y_ref, i * (ch // N2), ch // N2, plan)
        o_ref[0, pl.ds(r0, ch), :] = ((y + d_ref[...] * u) * x0).astype(o_ref.dtype)
        return 0

    lax.fori_loop(0, n_ch, gate_out, 0)


def _hyena_vmem_bytes(plan, tc, ch):
    L, rows_a = plan.L, plan.R * plan.N2
    pad_l = lambda m: _round_up(m, LANES)
    return (6 * L * tc * 2 + rows_a * tc * 2 + 2 * L * tc * 4 + rows_a * tc * 4 + 2 * L * tc * 2
            + plan.N2 * plan.R * pad_l(plan.N1 // 2) * 2 + plan.N2 * (plan.N1 // 2) * pad_l(plan.R) * 2
            + 16 * ch * tc * 4 + 16 * plan.N2 * tc * 4)


def _hyena_operator(hy, conv_w, conv_b, d, kf):
    B, L, W3 = hy.shape
    C = W3 // 3
    plan = _FftPlan(L)
    ch = min(CONV_CHUNK, L)
    tc = 2 * LANES if _vmem_limit(_hyena_vmem_bytes(plan, 2 * LANES, ch)) <= VMEM_LIMIT_CAP else LANES
    nb = C // tc
    f1full, g1, f2, f2i = _fft_consts(L)
    f1 = f1full[:, :, :plan.N1 // 2]
    rows_a = plan.R * plan.N2
    seq = lambda part: pl.BlockSpec((1, L, tc), lambda c, b: (b, 0, part * nb + c))
    wspec = lambda part: pl.BlockSpec((SHORT_CONV, tc), lambda c, b: (0, part * nb + c))
    bspec = lambda part: pl.BlockSpec((1, tc), lambda c, b: (0, part * nb + c))
    vmem = _hyena_vmem_bytes(plan, tc, ch)
    return pl.pallas_call(
        functools.partial(_hyena_kernel, plan=plan, ch=ch),
        grid=(nb, B),
        in_specs=[seq(0), seq(1), seq(2), wspec(0), wspec(1), wspec(2), bspec(0), bspec(1), bspec(2),
                  pl.BlockSpec((1, tc), lambda c, b: (0, c)),
                  pl.BlockSpec((rows_a, tc), lambda c, b: (0, c), pipeline_mode=pl.Buffered(1)),
                  _const_spec(f1.shape), _const_spec(g1.shape), _const_spec(f2.shape), _const_spec(f2i.shape)],
        out_specs=pl.BlockSpec((1, L, tc), lambda c, b: (b, 0, c)),
        out_shape=jax.ShapeDtypeStruct((B, L, C), BF16),
        scratch_shapes=[pltpu.VMEM((tc // LANES, (plan.N1 // 2) * plan.P, LANES), F32),
                        pltpu.VMEM((tc // LANES, plan.R * plan.P, LANES), F32),
                        pltpu.VMEM((tc // LANES, (plan.N1 // 2) * plan.P, LANES), F32)],
        compiler_params=_params(("parallel", "parallel"), vmem),
        name="hyena_op",
    )(hy, hy, hy, conv_w, conv_w, conv_w, conv_b, conv_b, conv_b, d, kf, f1, g1, f2, f2i)


def _memkv_kernel(m_ref, g_ref, wk_ref, wv_ref, k_ref, v_ref):
    mn = _rms(m_ref[...], g_ref[...]).astype(BF16)
    k_ref[...] = _dot(mn, wk_ref[...]).astype(BF16)
    v_ref[...] = _dot(mn, wv_ref[...]).astype(BF16)


def _memkv(mem, g, wk, wv, tm):
    T, D = mem.shape
    row = pl.BlockSpec((tm, D), lambda i: (i, 0))
    vmem = 2 * tm * D * 4 + 2 * D * D * 2 + 4 * tm * D * 2 + 3 * tm * D * 4
    return pl.pallas_call(
        _memkv_kernel,
        grid=(T // tm,),
        in_specs=[row, _const_spec((1, D)), _const_spec((D, D)), _const_spec((D, D))],
        out_specs=[row, row],
        out_shape=[jax.ShapeDtypeStruct((T, D), BF16)] * 2,
        compiler_params=_params(("parallel",), vmem),
        name="mem_kv",
    )(mem, g, wk, wv)


def _mixout_cross_kernel(a_ref, y_ref, x_ref, wa_ref, wy_ref, gmix_ref, k_ref, v_ref,
                         gpre_ref, wq_ref, wo_ref, gpost_ref, o_ref, *, q_scale):
    z = _dot(a_ref[...], wa_ref[...]) + _dot(y_ref[...], wy_ref[...])
    x = x_ref[...] + _rms(z, gmix_ref[...])
    xn = _rms(x, gpre_ref[...]).astype(BF16)
    q = (_dot(xn, wq_ref[...]) * q_scale).astype(BF16)
    hd = q.shape[1] // X_HEADS
    outs = []
    for h in range(X_HEADS):
        sl = slice(h * hd, (h + 1) * hd)
        s = _dot_nt(q[:, sl], k_ref[0, :, sl])
        p = jnp.exp2(s - jnp.max(s, axis=-1, keepdims=True))
        l = jnp.sum(p, axis=-1, keepdims=True)
        outs.append((_dot(p.astype(BF16), v_ref[0, :, sl]) / l).astype(BF16))
    z = _dot(jnp.concatenate(outs, axis=1), wo_ref[...])
    o_ref[...] = x + _rms(z, gpost_ref[...])


def _mixout_cross_block(attn, y, x, w_out, gmix, k, v, gpre, wq, wo, gpost, L, tm):
    T, D = x.shape
    a, c = attn.shape[1], y.shape[1]
    n_mem = k.shape[1]
    blocks_per_seq = L // tm
    row = lambda width: pl.BlockSpec((tm, width), lambda i: (i, 0))
    kv = pl.BlockSpec((1, n_mem, D), lambda i: (i // blocks_per_seq, 0, 0))
    vmem = (2 * tm * (a + c) * 2 + 4 * tm * D * 4 + 4 * n_mem * D * 2 + (a + c + 2 * D) * D * 2
            + 8 * tm * D * 4)
    q_scale = (D // X_HEADS) ** -0.5 * LOG2E
    return pl.pallas_call(
        functools.partial(_mixout_cross_kernel, q_scale=q_scale),
        grid=(T // tm,),
        in_specs=[row(a), row(c), row(D), _const_spec((a, D)), _const_spec((c, D)), _const_spec((1, D)),
                  kv, kv, _const_spec((1, D)), _const_spec((D, D)), _const_spec((D, D)), _const_spec((1, D))],
        out_specs=row(D),
        out_shape=jax.ShapeDtypeStruct((T, D), F32),
        compiler_params=_params(("parallel",), vmem),
        name="mixout_cross",
    )(attn, y, x, w_out[:a], w_out[a:], gmix, k, v, gpre, wq, wo, gpost)


def _swiglu_kernel(x_ref, gpre_ref, wg_ref, wu_ref, wd_ref, gpost_ref, o_ref, *, n_split):
    x = x_ref[...]
    xn = _rms(x, gpre_ref[...]).astype(BF16)
    ff = wg_ref.shape[1]
    cw = ff // n_split
    z = None
    for c in range(n_split):
        sl = slice(c * cw, (c + 1) * cw)
        gate = _dot(xn, wg_ref[:, sl])
        up = _dot(xn, wu_ref[:, sl])
        h = (gate * (1.0 / (1.0 + jnp.exp(-gate))) * up).astype(BF16)
        part = _dot(h, wd_ref[sl, :])
        z = part if z is None else z + part
    o_ref[...] = x + _rms(z, gpost_ref[...])


def _swiglu_block(x, gpre, wg, wu, wd, gpost, tm):
    T, D = x.shape
    ff = wg.shape[1]
    n_split = ff // MXU_WIDTH if ff % MXU_WIDTH == 0 else 1
    row = pl.BlockSpec((tm, D), lambda i: (i, 0))
    vmem = 4 * tm * D * 4 + 3 * D * ff * 2 + 4 * tm * (ff // n_split) * 4 + 4 * tm * D * 4
    return pl.pallas_call(
        functools.partial(_swiglu_kernel, n_split=n_split),
        grid=(T // tm,),
        in_specs=[row, _const_spec((1, D)), _const_spec((D, ff)), _const_spec((D, ff)),
                  _const_spec((ff, D)), _const_spec((1, D))],
        out_specs=row,
        out_shape=jax.ShapeDtypeStruct((T, D), F32),
        compiler_params=_params(("parallel",), vmem),
        name="swiglu",
    )(x, gpre, wg, wu, wd, gpost)


def _trunk(x, mem, P):
    B, L, D = x.shape
    depth = P['w_in'].shape[0]
    n_mem = mem.shape[1]
    tm = min(ROW_TILE, L)
    tq, tk = tm, min(KEY_CHUNK, L)
    tc = LANES
    a = DA_HEADS * 2 * DA_HEAD_DIM
    tables = _rope_tables(L)
    xf = x.reshape(B * L, D)
    memf = mem.reshape(B * n_mem, D)
    row = lambda v: v.reshape(1, -1)
    for l in range(depth):
        lam_init = 0.8 - 0.6 * math.exp(-0.3 * l)
        qt, k, vt, hy = _inproj(xf, row(P['ln_mix_pre'][l]), P['w_in'][l].astype(BF16), tables, B, L, tm, tk)
        lam_params = jnp.stack([P['lambda_q1'][l], P['lambda_k1'][l], P['lambda_q2'][l], P['lambda_k2'][l],
                                jnp.full((DA_HEAD_DIM,), lam_init, F32)])
        attn = _diff_attention(lam_params, qt, k.reshape(B, L, a), vt, P['subln_g'][l], tq, tk)
        kf = _hyena_filter_spectrum(L, P['filt_w1'][l], row(P['filt_b1'][l]), row(P['filt_freq1'][l]),
                                    P['filt_w2'][l], row(P['filt_b2'][l]), row(P['filt_freq2'][l]),
                                    P['filt_w3'][l], tc)
        y = _hyena_operator(hy.reshape(B, L, -1), P['conv_w'][l], row(P['conv_b'][l]),
                            row(P['hyena_d'][l]), kf)
        km, vm = _memkv(memf, row(P['ln_mem'][l]), P['wk_x'][l].astype(BF16), P['wv_x'][l].astype(BF16),
                        min(ROW_TILE, B * n_mem))
        xf = _mixout_cross_block(attn.reshape(B * L, a), y.reshape(B * L, -1), xf, P['w_out'][l].astype(BF16),
                                 row(P['ln_mix_post'][l]), km.reshape(B, n_mem, D), vm.reshape(B, n_mem, D),
                                 row(P['ln_x_pre'][l]), P['wq_x'][l].astype(BF16), P['wo_x'][l].astype(BF16),
                                 row(P['ln_x_post'][l]), L, min(MIXOUT_ROW_TILE, L))
        xf = _swiglu_block(xf, row(P['ln_ffn_pre'][l]), P['w_gate'][l].astype(BF16), P['w_up'][l].astype(BF16),
                           P['w_down'][l].astype(BF16), row(P['ln_ffn_post'][l]), tm)
    return xf.reshape(B, L, D)


def kernel(x_prompt, x_sample, mem_prompt, mem_sample, ln_mix_pre, ln_mix_post, w_in, lambda_q1, lambda_k1, lambda_q2, lambda_k2, subln_g, conv_w, conv_b, filt_w1, filt_b1, filt_freq1, filt_w2, filt_b2, filt_freq2, filt_w3, hyena_d, w_out, ln_x_pre, ln_x_post, ln_mem, wq_x, wk_x, wv_x, wo_x, ln_ffn_pre, ln_ffn_post, w_gate, w_up, w_down):
    P = dict(ln_mix_pre=ln_mix_pre, ln_mix_post=ln_mix_post, w_in=w_in,
             lambda_q1=lambda_q1, lambda_k1=lambda_k1, lambda_q2=lambda_q2, lambda_k2=lambda_k2,
             subln_g=subln_g, conv_w=conv_w, conv_b=conv_b,
             filt_w1=filt_w1, filt_b1=filt_b1, filt_freq1=filt_freq1,
             filt_w2=filt_w2, filt_b2=filt_b2, filt_freq2=filt_freq2, filt_w3=filt_w3,
             hyena_d=hyena_d, w_out=w_out,
             ln_x_pre=ln_x_pre, ln_x_post=ln_x_post, ln_mem=ln_mem,
             wq_x=wq_x, wk_x=wk_x, wv_x=wv_x, wo_x=wo_x,
             ln_ffn_pre=ln_ffn_pre, ln_ffn_post=ln_ffn_post,
             w_gate=w_gate, w_up=w_up, w_down=w_down)
    return (_trunk(x_prompt, mem_prompt, P), _trunk(x_sample, mem_sample, P))
```

```python
import functools
import math

import numpy as np
import jax
import jax.numpy as jnp
from jax import lax
from jax.experimental import pallas as pl
from jax.experimental.pallas import tpu as pltpu

F32 = jnp.float32
BF16 = jnp.bfloat16

DA_HEADS = 4
DA_HEAD_DIM = 64
DA_V_DIM = 128
ROPE_DIM = 16
ROPE_THETA = 500000.0
SHORT_CONV = 3
FILTER_EMB = 33
FILTER_ORDER = 64
FAST_DECAY_PCT = 0.3
SLOW_DECAY_PCT = 1.5
DECAY_TARGET = 1e-2
X_HEADS = 4
EPS = 1e-6
LOG2E = 1.4426950408889634

LANES = 128
SUBLANES = 8
BF16_SUBLANES = 16
MXU_WIDTH = 256
VMEM_LIMIT_CAP = 58 * 2**20
VMEM_TEMP_FACTOR = 1.25
VMEM_TEMP_BYTES = 4 * 2**20

ROW_TILE = 512
MIXOUT_ROW_TILE = 1024
KEY_CHUNK = 1024
KEY_BLOCK = 512
CONV_CHUNK = 512
ATTN_PASSES_PER_STEP = 32

STAGE1_UNROLL = 64
STAGE2_UNROLL = 32


def _round_up(a, b):
    return -(-a // b) * b


def _vmem_limit(vmem_bytes):
    return int(vmem_bytes * VMEM_TEMP_FACTOR) + VMEM_TEMP_BYTES


def _params(semantics, vmem_bytes):
    limit = min(_vmem_limit(vmem_bytes), VMEM_LIMIT_CAP)
    return pltpu.CompilerParams(dimension_semantics=semantics, vmem_limit_bytes=limit)


def _const_spec(shape):
    nd = len(shape)
    return pl.BlockSpec(shape, lambda *_: (0,) * nd, pipeline_mode=pl.Buffered(1))


def _rms(x, g):
    ms = jnp.mean(x * x, axis=-1, keepdims=True)
    return x * lax.rsqrt(ms + EPS) * g


def _dot(a, b):
    return jnp.dot(a, b, preferred_element_type=F32)


def _dot_nt(a, b):
    return lax.dot_general(a, b, (((1,), (1,)), ((), ())), preferred_element_type=F32)


def _rope_tables(L):
    inv = ROPE_THETA ** (-np.arange(0, ROPE_DIM, 2, dtype=np.float64) / ROPE_DIM)
    ang = np.arange(L, dtype=np.float64)[:, None] * inv[None, :]
    half = ROPE_DIM // 2
    cos = np.ones((L, LANES)); sa = np.zeros((L, LANES)); sb = np.zeros((L, LANES))
    for g in range(LANES // DA_HEAD_DIM):
        o = g * DA_HEAD_DIM
        cos[:, o:o + half] = np.cos(ang)
        cos[:, o + half:o + ROPE_DIM] = np.cos(ang)
        sb[:, o:o + half] = -np.sin(ang)
        sa[:, o + half:o + ROPE_DIM] = np.sin(ang)
    return (jnp.asarray(cos, F32), jnp.asarray(sa, F32), jnp.asarray(sb, F32),
            jnp.asarray(np.cos(ang).T, F32), jnp.asarray(np.sin(ang).T, F32))


def _inproj_kernel(x_ref, g_ref, w_ref, wqt_ref, wvt_ref, cos_ref, sa_ref, sb_ref, cost_ref, sint_ref,
                   qt_ref, k_ref, vt_ref, hy_ref, *, q_scale):
    xn = _rms(x_ref[...], g_ref[...]).astype(BF16)
    cos, sa, sb = cos_ref[...], sa_ref[...], sb_ref[...]
    half = ROPE_DIM // 2

    def rope(y):
        parts = []
        for s in range(y.shape[1] // LANES):
            ys = y[:, s * LANES:(s + 1) * LANES]
            parts.append(ys * cos + pltpu.roll(ys, half, 1) * sa
                         + pltpu.roll(ys, LANES - half, 1) * sb)
        return jnp.concatenate(parts, axis=1)

    def rope_t(y):
        ct, st = cost_ref[...], sint_ref[...]
        parts = []
        for o in range(0, y.shape[0], DA_HEAD_DIM):
            x1, x2 = y[o:o + half], y[o + half:o + ROPE_DIM]
            parts += [x1 * ct - x2 * st, x2 * ct + x1 * st, y[o + ROPE_DIM:o + DA_HEAD_DIM]]
        return jnp.concatenate(parts, axis=0)

    a = k_ref.shape[1]
    qt_ref[0, 0] = (rope_t(_dot_nt(wqt_ref[...], xn)) * q_scale).astype(BF16)
    k_ref[...] = rope(_dot(xn, w_ref[:, a:2 * a])).astype(BF16)
    vt_ref[0, 0] = _dot_nt(wvt_ref[...], xn).astype(BF16)
    hy_ref[...] = _dot(xn, w_ref[:, 3 * a:]).astype(BF16)


def _inproj(x, g, w, tables, B, L, tm, tk):
    T, D = x.shape
    n_out = w.shape[1]
    a = DA_HEADS * 2 * DA_HEAD_DIM
    hyw = n_out - 3 * a
    blocks_per_seq = L // tm
    tiles_per_chunk = tk // tm
    tab_spec = pl.BlockSpec((tm, LANES), lambda i: (i % blocks_per_seq, 0))
    tabt_spec = pl.BlockSpec((ROPE_DIM // 2, tm), lambda i: (0, i % blocks_per_seq))
    row = lambda width: pl.BlockSpec((tm, width), lambda i: (i, 0))
    qt_spec = pl.BlockSpec((1, 1, a, tm), lambda i: (i // blocks_per_seq, i % blocks_per_seq, 0, 0))
    vt_spec = pl.BlockSpec((1, 1, a, tm), lambda i: (i // blocks_per_seq, (i % blocks_per_seq) // tiles_per_chunk,
                                                     0, i % tiles_per_chunk))
    vmem = 2 * tm * D * 4 + D * (n_out + 2 * a) * 2 + 8 * tm * LANES * 4 + 2 * tm * n_out * 2 + 4 * tm * 1536 * 4
    q_scale = DA_HEAD_DIM ** -0.5 * LOG2E
    wqt, wvt = w[:, 0:a].T, w[:, 2 * a:3 * a].T
    return pl.pallas_call(
        functools.partial(_inproj_kernel, q_scale=q_scale),
        grid=(T // tm,),
        in_specs=[row(D), _const_spec((1, D)), _const_spec((D, n_out)), _const_spec((a, D)), _const_spec((a, D)),
                  tab_spec, tab_spec, tab_spec, tabt_spec, tabt_spec],
        out_specs=[qt_spec, row(a), vt_spec, row(hyw)],
        out_shape=[jax.ShapeDtypeStruct((B, L // tm, a, tm), BF16), jax.ShapeDtypeStruct((T, a), BF16),
                   jax.ShapeDtypeStruct((B, L // tk, a, tk), BF16), jax.ShapeDtypeStruct((T, hyw), BF16)],
        compiler_params=_params(("parallel",), vmem),
        name="inproj",
    )(x, g, w, wqt, wvt, *tables)


def _attn_kernel(lam_ref, qt_ref, k_ref, vt_ref, g_ref, o_ref, s_ref, *, tk, kb):
    nt, _, tq = qt_ref.shape[1:]
    L = k_ref.shape[1]
    feat = lax.broadcasted_iota(jnp.int32, (2 * DA_HEAD_DIM, tq), 0)

    def split(qt):
        zero = jnp.zeros_like(qt)
        return (jnp.where(feat < DA_HEAD_DIM, qt, zero), jnp.where(feat >= DA_HEAD_DIM, qt, zero))

    qz = [split(qt_ref[0, t]) for t in range(nt)]

    lp = lam_ref[...]
    lam_init = lp[4:5, 0:1]
    lam = (jnp.exp(jnp.sum(lp[0:1] * lp[1:2], axis=-1, keepdims=True))
           - jnp.exp(jnp.sum(lp[2:3] * lp[3:4], axis=-1, keepdims=True)) + lam_init)
    g = jnp.concatenate([g_ref[...]] * (tq // LANES), axis=1)

    nk, nb = L // tk, tk // kb
    neg = jnp.full((1, tq), -1e30, F32)
    fresh = (neg, jnp.zeros((1, tq), F32), jnp.zeros((DA_V_DIM, tq), F32))

    def score_block(t, j, c, b, mx):
        s = _dot(k_ref[0, j * tk + b * kb:j * tk + (b + 1) * kb, :], qz[t][c])
        s_ref[c, b * kb:(b + 1) * kb, :] = s
        return jnp.maximum(mx, jnp.max(s, axis=0, keepdims=True))

    def value_block(j, c, b, mn, lsum, pv):
        p = jnp.exp2(s_ref[c, b * kb:(b + 1) * kb, :] - mn)
        lsum = lsum + jnp.sum(p, axis=0, keepdims=True)
        pv = pv + _dot(vt_ref[0, j, :, b * kb:(b + 1) * kb], p.astype(BF16))
        return lsum, pv

    def step(score_of, value_of, mx_cur, state):
        m, l, acc = state
        mn = jnp.maximum(m, mx_cur)
        alpha = jnp.exp2(m - mn)
        lsum, pv, mx = jnp.zeros((1, tq), F32), jnp.zeros((DA_V_DIM, tq), F32), neg
        for b in range(nb):
            if score_of is not None:
                mx = score_block(*score_of, b, mx)
            lsum, pv = value_block(*value_of[1:], b, mn, lsum, pv)
        return mx, (mn, alpha * l + lsum, alpha * acc + pv)

    def finish(t, st0, st1):
        (_, l0, a0), (_, l1, a1) = st0, st1
        o = a0 / l0 - lam * (a1 / l1)
        ms = jnp.mean(o * o, axis=0, keepdims=True)
        o = o * lax.rsqrt(ms + EPS) * g * (1.0 - lam_init)
        o_ref[0, t * tq:(t + 1) * tq, :] = o.T.astype(BF16)

    passes = [(t, j, c) for t in range(nt) for j in range(nk) for c in range(2)]
    mx = neg
    for b in range(nb):
        mx = score_block(*passes[0], b, mx)
    states = {}
    for n, cur in enumerate(passes):
        t, j, c = cur
        nxt = passes[n + 1] if n + 1 < len(passes) else None
        mx, states[(t, c)] = step(nxt, cur, mx, states.get((t, c), fresh))
        if j == nk - 1 and c == 1:
            finish(t, states.pop((t, 0)), states.pop((t, 1)))


def _diff_attention(lam_params, qt, k, vt, g, tq, tk):
    B, L, A = k.shape
    nt = max(1, min(L // tq, ATTN_PASSES_PER_STEP // (2 * (L // tk))))
    qspec = pl.BlockSpec((1, nt, LANES, tq), lambda b, h, i: (b, i, h, 0))
    ospec = pl.BlockSpec((1, nt * tq, LANES), lambda b, h, i: (b, i, h))
    kspec = pl.BlockSpec((1, L, LANES), lambda b, h, i: (b, 0, h))
    vspec = pl.BlockSpec((1, L // tk, LANES, tk), lambda b, h, i: (b, 0, h, 0))
    g_cols = jnp.broadcast_to(g.reshape(DA_V_DIM, 1), (DA_V_DIM, LANES))
    kb = min(KEY_BLOCK, tk)
    vmem = (4 * L * LANES * 2 + 4 * nt * tq * LANES * 2 + 2 * tq * tk * 4 + 4 * tq * kb * 6
            + 16 * tq * LANES * 4)
    return pl.pallas_call(
        functools.partial(_attn_kernel, tk=tk, kb=kb),
        grid=(B, DA_HEADS, L // (nt * tq)),
        in_specs=[_const_spec(lam_params.shape), qspec, kspec, vspec, _const_spec((DA_V_DIM, LANES))],
        out_specs=ospec,
        out_shape=jax.ShapeDtypeStruct((B, L, A), BF16),
        scratch_shapes=[pltpu.VMEM((2, tk, tq), F32)],
        compiler_params=_params(("parallel", "parallel", "parallel"), vmem),
        name="diff_attn",
    )(lam_params, qt, k, vt, g_cols)


class _FftPlan:
    def __init__(self, L):
        n = 2 * L
        lg = int(math.log2(n))
        assert 2 ** lg == n
        self.L, self.n = L, n
        self.N1 = 2 ** (lg // 2)
        self.N2 = n // self.N1
        self.K1 = self.N1 // 2 + 1
        self.R = _round_up(2 * self.K1, BF16_SUBLANES)
        self.P = self.N2 + SUBLANES
        assert (self.P // SUBLANES) % 2 == 1
        assert self.N2 % BF16_SUBLANES == 0 and (self.N1 // 2) % BF16_SUBLANES == 0


@functools.lru_cache(maxsize=None)
def _fft_consts(L):
    p = _FftPlan(L)
    N1, N2, K1, R, n = p.N1, p.N2, p.K1, p.R, p.n
    t2 = np.arange(N2, dtype=np.float64)[:, None, None]
    k1 = np.arange(K1, dtype=np.float64)[None, :, None]
    t1 = np.arange(N1, dtype=np.float64)[None, None, :]
    ang = -2.0 * np.pi * (t1 * k1 / N1 + t2 * k1 / n)
    f1 = np.zeros((N2, R, N1))
    f1[:, 0:2 * K1:2, :] = np.cos(ang)
    f1[:, 1:2 * K1:2, :] = np.sin(ang)
    c = np.where((np.arange(K1) == 0) | (np.arange(K1) == N1 // 2), 1.0, 2.0)[None, :, None]
    th = -ang[:, :, :N1 // 2]
    g1 = np.zeros((N2, N1 // 2, R))
    g1[:, :, 0:2 * K1:2] = np.transpose(c * np.cos(th), (0, 2, 1))
    g1[:, :, 1:2 * K1:2] = np.transpose(-c * np.sin(th), (0, 2, 1))
    a2 = 2.0 * np.pi * np.outer(np.arange(N2), np.arange(N2)) / N2
    C, S = np.cos(a2), np.sin(a2)
    f2 = np.block([[C, S], [-S, C]])
    return (jnp.asarray(f1, BF16), jnp.asarray(g1, BF16), jnp.asarray(f2, BF16), jnp.asarray(f2.T, BF16))


def _load_rows(ref, idx):
    parts = [ref[g, idx, :] for g in range(ref.shape[0])]
    return parts[0] if len(parts) == 1 else jnp.concatenate(parts, axis=1)


def _store_rows(ref, idx, val):
    for g in range(ref.shape[0]):
        ref[g, idx, :] = val[:, g * LANES:(g + 1) * LANES]


def _store_slabs(ref, first_slab, val, plan):
    for s in range(val.shape[0] // plan.N2):
        start = pl.multiple_of((first_slab + s) * plan.P, SUBLANES)
        _store_rows(ref, pl.ds(start, plan.N2), val[s * plan.N2:(s + 1) * plan.N2])


def _load_slabs(ref, first_slab, count, plan):
    parts = [_load_rows(ref, pl.ds(pl.multiple_of((first_slab + s) * plan.P, SUBLANES), plan.N2))
             for s in range(count)]
    return parts[0] if count == 1 else jnp.concatenate(parts, axis=0)


def _dft_stage1(src_ref, f1_ref, a_ref, n_slabs, plan):
    def body(t2, _):
        rows = _load_rows(src_ref, pl.ds(t2, n_slabs, stride=plan.P)).astype(BF16)
        _store_rows(a_ref, pl.ds(t2, plan.R, stride=plan.P), _dot(f1_ref[t2], rows))
        return 0

    lax.fori_loop(0, plan.N2, body, 0, unroll=STAGE1_UNROLL)


@functools.lru_cache(maxsize=None)
def _filter_features(L):
    t = np.linspace(0.0, 1.0, L)[:, None]
    bands = (FILTER_EMB - 1) // 2
    w = 2.0 * np.pi * np.arange(L)[:, None] / L
    f = np.linspace(1e-4, bands - 1, bands)[None, :]
    z = np.concatenate([t, np.cos(f * w), -np.sin(f * w)], axis=-1)
    z_rev = np.concatenate([z[:1], z[:0:-1]], axis=0)
    zz = np.zeros((2 * L, LANES))
    zz[:L, :FILTER_EMB] = z
    zz[L:, :FILTER_EMB] = z_rev
    return jnp.asarray(zz, BF16)


def _decay_rates(width):
    max_decay = math.log(DECAY_TARGET) / FAST_DECAY_PCT
    min_decay = math.log(DECAY_TARGET) / SLOW_DECAY_PCT
    return jnp.asarray(np.abs(np.linspace(min_decay, max_decay, width))[None, :], F32)


def _filter_kernel(z_ref, w1_ref, b1_ref, fr1_ref, w2_ref, b2_ref, fr2_ref,
                   w3f_ref, w3b_ref, dl_ref, f1_ref, f2_ref, kf_ref,
                   h_ref, kern_ref, a_ref, *, plan, ch):
    L, N1, N2, K1 = plan.L, plan.N1, plan.N2, plan.K1
    n_ch = (2 * L) // ch

    @pl.when(pl.program_id(0) == 0)
    def _():
        def mlp(i, _):
            r0 = pl.multiple_of(i * ch, ch)
            h = jnp.sin(fr1_ref[...] * (_dot(z_ref[pl.ds(r0, ch), :], w1_ref[...]) + b1_ref[...]))
            h = jnp.sin(fr2_ref[...] * (_dot(h.astype(BF16), w2_ref[...]) + b2_ref[...]))
            h_ref[pl.ds(r0, ch), :] = h.astype(BF16)
            return 0
        lax.fori_loop(0, n_ch, mlp, 0)

    tc = kf_ref.shape[1]
    local = lax.broadcasted_iota(jnp.int32, (ch, tc), 0)

    def synth(i, asum):
        r0 = pl.multiple_of(i * ch, ch)
        row = local + r0
        h = h_ref[pl.ds(r0, ch), :]
        fwd = _dot(h, w3f_ref[...])
        bwd = _dot(h, w3b_ref[...])
        lag = jnp.where(row < L, row, 2 * L - row).astype(F32)
        win = jnp.exp(lag * (-1.0 / (L - 1)) * dl_ref[...])
        val = jnp.where(row < L, fwd + jnp.where(row == 0, bwd, 0.0), bwd) * win
        val = jnp.where(row == L, 0.0, val)
        _store_slabs(kern_ref, i * (ch // N2), val, plan)
        return asum + jnp.sum(jnp.abs(val), axis=0, keepdims=True)

    asum = lax.fori_loop(0, n_ch, synth, jnp.zeros((1, tc), F32))
    scale = 1.0 / (asum * float(plan.n))

    _dft_stage1(kern_ref, f1_ref, a_ref, N1, plan)

    def stage2(k1, _):
        r0 = pl.multiple_of(k1 * 2 * N2, 2 * N2)
        slab = _load_slabs(a_ref, 2 * k1, 2, plan).astype(BF16)
        kf_ref[pl.ds(r0, 2 * N2), :] = (_dot(f2_ref[...], slab) * scale).astype(kf_ref.dtype)
        return 0

    lax.fori_loop(0, K1, stage2, 0, unroll=STAGE2_UNROLL)
    pad = plan.R - 2 * K1
    if pad:
        kf_ref[pl.ds(2 * K1 * N2, pad * N2), :] = jnp.zeros((pad * N2, tc), kf_ref.dtype)


def _hyena_filter_spectrum(L, w1, b1, fr1, w2, b2, fr2, w3, tc):
    plan = _FftPlan(L)
    C = w3.shape[1] // 2
    zz = _filter_features(L)
    f1, _, f2, _ = _fft_consts(L)
    w1p = jnp.zeros((LANES, FILTER_ORDER), BF16).at[:FILTER_EMB].set(w1.astype(BF16))
    ch = min(CONV_CHUNK, L)
    rows_a = plan.R * plan.N2
    cblock = lambda off: pl.BlockSpec((FILTER_ORDER, tc), lambda c: (0, off + c))
    vmem = (2 * L * LANES * 2 * 2 + 2 * L * tc * 4 + rows_a * tc * 4 + 2 * rows_a * tc * 2
            + f1.size * 2 * 2 + 8 * ch * tc * 4)
    return pl.pallas_call(
        functools.partial(_filter_kernel, plan=plan, ch=ch),
        grid=(C // tc,),
        in_specs=[_const_spec(zz.shape),
                  _const_spec(w1p.shape), _const_spec((1, FILTER_ORDER)), _const_spec((1, FILTER_ORDER)),
                  _const_spec((FILTER_ORDER, FILTER_ORDER)), _const_spec((1, FILTER_ORDER)),
                  _const_spec((1, FILTER_ORDER)),
                  cblock(0), cblock(C // tc),
                  pl.BlockSpec((1, tc), lambda c: (0, c)),
                  _const_spec(f1.shape), _const_spec(f2.shape)],
        out_specs=pl.BlockSpec((rows_a, tc), lambda c: (0, c)),
        out_shape=jax.ShapeDtypeStruct((rows_a, C), BF16),
        scratch_shapes=[pltpu.VMEM((2 * L, FILTER_ORDER), BF16),
                        pltpu.VMEM((tc // LANES, plan.N1 * plan.P, LANES), F32),
                        pltpu.VMEM((tc // LANES, plan.R * plan.P, LANES), F32)],
        compiler_params=_params(("arbitrary",), vmem),
        name="hyena_filter",
    )(zz, w1p, b1, fr1, w2.astype(BF16), b2, fr2, w3.astype(BF16), w3.astype(BF16),
      _decay_rates(C), f1, f2)


def _short_conv_chunk(raw_ref, w_ref, b_ref, r0, ch, L):
    halo = BF16_SUBLANES
    c = raw_ref[0, pl.ds(r0, ch), :].astype(F32)
    p0 = pl.multiple_of(jnp.maximum(r0 - halo, 0), halo)
    n0 = pl.multiple_of(jnp.minimum(r0 + ch, L - halo), halo)
    prev = raw_ref[0, pl.ds(p0, halo), :].astype(F32)[halo - 1:halo]
    nxt = raw_ref[0, pl.ds(n0, halo), :].astype(F32)[0:1]
    prev = jnp.where(r0 == 0, 0.0, prev)
    nxt = jnp.where(r0 + ch == L, 0.0, nxt)
    rows = lax.broadcasted_iota(jnp.int32, c.shape, 0)
    up = jnp.where(rows == 0, prev, pltpu.roll(c, 1, 0))
    un = jnp.where(rows == ch - 1, nxt, pltpu.roll(c, ch - 1, 0))
    w = w_ref[...]
    return w[0:1] * up + w[1:2] * c + w[2:3] * un + b_ref[...]


def _hyena_kernel(x0_ref, x1_ref, vh_ref, w0_ref, w1_ref, wv_ref, b0_ref, b1_ref, bv_ref,
                  d_ref, kf_ref, f1_ref, g1_ref, f2_ref, f2i_ref, o_ref,
                  u_ref, a_ref, y_ref, *, plan, ch):
    L, N1, N2, K1, R = plan.L, plan.N1, plan.N2, plan.K1, plan.R
    n_ch = L // ch

    def gate_in(i, _):
        r0 = pl.multiple_of(i * ch, ch)
        x1 = _short_conv_chunk(x1_ref, w1_ref, b1_ref, r0, ch, L)
        vh = _short_conv_chunk(vh_ref, wv_ref, bv_ref, r0, ch, L)
        _store_slabs(u_ref, i * (ch // N2), x1 * vh, plan)
        return 0

    lax.fori_loop(0, n_ch, gate_in, 0)

    _dft_stage1(u_ref, f1_ref, a_ref, N1 // 2, plan)

    tc = o_ref.shape[2]
    group = max(1, MXU_WIDTH // tc)

    def spectral_group(first, count):
        x = _dot(f2_ref[...], jnp.concatenate(
            [_load_slabs(a_ref, 2 * (first + i), 2, plan) for i in range(count)], axis=1).astype(BF16))
        ys = []
        for i in range(count):
            r0 = pl.multiple_of((first + i) * 2 * N2, 2 * N2)
            xr, xi = x[:N2, i * tc:(i + 1) * tc], x[N2:, i * tc:(i + 1) * tc]
            kr = kf_ref[pl.ds(r0, N2), :].astype(F32)
            ki = kf_ref[pl.ds(r0 + N2, N2), :].astype(F32)
            ys.append(jnp.concatenate([xr * kr - xi * ki, xr * ki + xi * kr], axis=0))
        z = _dot(f2i_ref[...], jnp.concatenate(ys, axis=1).astype(BF16))
        for i in range(count):
            _store_slabs(a_ref, 2 * (first + i), z[:, i * tc:(i + 1) * tc], plan)

    def spectral(i, _):
        spectral_group(i * group, group)
        return 0

    lax.fori_loop(0, K1 // group, spectral, 0, unroll=STAGE2_UNROLL // group)
    for k1 in range(K1 - K1 % group, K1):
        spectral_group(k1, 1)

    def inverse1(t2, _):
        rows = _load_rows(a_ref, pl.ds(t2, R, stride=plan.P)).astype(BF16)
        _store_rows(y_ref, pl.ds(t2, N1 // 2, stride=plan.P), _dot(g1_ref[t2], rows))
        return 0

    lax.fori_loop(0, N2, inverse1, 0, unroll=STAGE1_UNROLL)

    def gate_out(i, _):
        r0 = pl.multiple_of(i * ch, ch)
        x0 = _short_conv_chunk(x0_ref, w0_ref, b0_ref, r0, ch, L)
        u = _load_slabs(u_ref, i * (ch // N2), ch // N2, plan)
        y = _load_slabs(y_ref, i * (ch // N2), ch // N2, plan)
        o_ref[0, pl.ds(r0, ch), :] = ((y + d_ref[...] * u) * x0).astype(o_ref.dtype)
        return 0

    lax.fori_loop(0, n_ch, gate_out, 0)


def _hyena_vmem_bytes(plan, tc, ch):
    L, rows_a = plan.L, plan.R * plan.N2
    pad_l = lambda m: _round_up(m, LANES)
    return (6 * L * tc * 2 + rows_a * tc * 2 + 2 * L * tc * 4 + rows_a * tc * 4 + 2 * L * tc * 2
            + plan.N2 * plan.R * pad_l(plan.N1 // 2) * 2 + plan.N2 * (plan.N1 // 2) * pad_l(plan.R) * 2
            + 16 * ch * tc * 4 + 16 * plan.N2 * tc * 4)


def _hyena_operator(hy, conv_w, conv_b, d, kf):
    B, L, W3 = hy.shape
    C = W3 // 3
    plan = _FftPlan(L)
    ch = min(CONV_CHUNK, L)
    tc = 2 * LANES if _vmem_limit(_hyena_vmem_bytes(plan, 2 * LANES, ch)) <= VMEM_LIMIT_CAP else LANES
    nb = C // tc
    f1full, g1, f2, f2i = _fft_consts(L)
    f1 = f1full[:, :, :plan.N1 // 2]
    rows_a = plan.R * plan.N2
    seq = lambda part: pl.BlockSpec((1, L, tc), lambda c, b: (b, 0, part * nb + c))
    wspec = lambda part: pl.BlockSpec((SHORT_CONV, tc), lambda c, b: (0, part * nb + c))
    bspec = lambda part: pl.BlockSpec((1, tc), lambda c, b: (0, part * nb + c))
    vmem = _hyena_vmem_bytes(plan, tc, ch)
    return pl.pallas_call(
        functools.partial(_hyena_kernel, plan=plan, ch=ch),
        grid=(nb, B),
        in_specs=[seq(0), seq(1), seq(2), wspec(0), wspec(1), wspec(2), bspec(0), bspec(1), bspec(2),
                  pl.BlockSpec((1, tc), lambda c, b: (0, c)),
                  pl.BlockSpec((rows_a, tc), lambda c, b: (0, c), pipeline_mode=pl.Buffered(1)),
                  _const_spec(f1.shape), _const_spec(g1.shape), _const_spec(f2.shape), _const_spec(f2i.shape)],
        out_specs=pl.BlockSpec((1, L, tc), lambda c, b: (b, 0, c)),
        out_shape=jax.ShapeDtypeStruct((B, L, C), BF16),
        scratch_shapes=[pltpu.VMEM((tc // LANES, (plan.N1 // 2) * plan.P, LANES), F32),
                        pltpu.VMEM((tc // LANES, plan.R * plan.P, LANES), F32),
                        pltpu.VMEM((tc // LANES, (plan.N1 // 2) * plan.P, LANES), F32)],
        compiler_params=_params(("parallel", "parallel"), vmem),
        name="hyena_op",
    )(hy, hy, hy, conv_w, conv_w, conv_w, conv_b, conv_b, conv_b, d, kf, f1, g1, f2, f2i)


def _memkv_kernel(m_ref, g_ref, wkt_ref, wv_ref, kt_ref, v_ref):
    mn = _rms(m_ref[...], g_ref[...]).astype(BF16)
    kt_ref[0] = _dot_nt(wkt_ref[...], mn).astype(BF16)
    v_ref[...] = _dot(mn, wv_ref[...]).astype(BF16)


def _memkv(mem, g, wk, wv, tm):
    T, D = mem.shape
    row = pl.BlockSpec((tm, D), lambda i: (i, 0))
    vmem = 2 * tm * D * 4 + 2 * D * D * 2 + 4 * tm * D * 2 + 3 * tm * D * 4
    return pl.pallas_call(
        _memkv_kernel,
        grid=(T // tm,),
        in_specs=[row, _const_spec((1, D)), _const_spec((D, D)), _const_spec((D, D))],
        out_specs=[pl.BlockSpec((1, D, tm), lambda i: (i, 0, 0)), row],
        out_shape=[jax.ShapeDtypeStruct((T // tm, D, tm), BF16), jax.ShapeDtypeStruct((T, D), BF16)],
        compiler_params=_params(("parallel",), vmem),
        name="mem_kv",
    )(mem, g, wk.T, wv)


def _mixout_cross_kernel(a_ref, y_ref, x_ref, wa_ref, wy_ref, gmix_ref, kt_ref, v_ref,
                         gpre_ref, wq_ref, wo_ref, gpost_ref, o_ref, *, q_scale):
    z = _dot(a_ref[...], wa_ref[...]) + _dot(y_ref[...], wy_ref[...])
    x = x_ref[...] + _rms(z, gmix_ref[...])
    xn = _rms(x, gpre_ref[...]).astype(BF16)
    q = (_dot(xn, wq_ref[...]) * q_scale).astype(BF16)
    hd = q.shape[1] // X_HEADS
    outs = []
    for h in range(X_HEADS):
        sl = slice(h * hd, (h + 1) * hd)
        s = _dot(q[:, sl], kt_ref[0, sl, :])
        p = jnp.exp2(s - jnp.max(s, axis=-1, keepdims=True))
        l = jnp.sum(p, axis=-1, keepdims=True)
        outs.append((_dot(p.astype(BF16), v_ref[0, :, sl]) / l).astype(BF16))
    z = _dot(jnp.concatenate(outs, axis=1), wo_ref[...])
    o_ref[...] = x + _rms(z, gpost_ref[...])


def _mixout_cross_block(attn, y, x, w_out, gmix, kt, v, gpre, wq, wo, gpost, L, tm):
    T, D = x.shape
    a, c = attn.shape[1], y.shape[1]
    n_mem = v.shape[1]
    blocks_per_seq = L // tm
    row = lambda width: pl.BlockSpec((tm, width), lambda i: (i, 0))
    kv = pl.BlockSpec((1, n_mem, D), lambda i: (i // blocks_per_seq, 0, 0))
    ktspec = pl.BlockSpec((1, D, n_mem), lambda i: (i // blocks_per_seq, 0, 0))
    vmem = (2 * tm * (a + c) * 2 + 4 * tm * D * 4 + 4 * n_mem * D * 2 + (a + c + 2 * D) * D * 2
            + 8 * tm * D * 4)
    q_scale = (D // X_HEADS) ** -0.5 * LOG2E
    return pl.pallas_call(
        functools.partial(_mixout_cross_kernel, q_scale=q_scale),
        grid=(T // tm,),
        in_specs=[row(a), row(c), row(D), _const_spec((a, D)), _const_spec((c, D)), _const_spec((1, D)),
                  ktspec, kv, _const_spec((1, D)), _const_spec((D, D)), _const_spec((D, D)), _const_spec((1, D))],
        out_specs=row(D),
        out_shape=jax.ShapeDtypeStruct((T, D), F32),
        compiler_params=_params(("parallel",), vmem),
        name="mixout_cross",
    )(attn, y, x, w_out[:a], w_out[a:], gmix, kt, v, gpre, wq, wo, gpost)


def _swiglu_kernel(x_ref, gpre_ref, wg_ref, wu_ref, wd_ref, gpost_ref, o_ref, *, n_split):
    x = x_ref[...]
    xn = _rms(x, gpre_ref[...]).astype(BF16)
    ff = wg_ref.shape[1]
    cw = ff // n_split
    z = None
    for c in range(n_split):
        sl = slice(c * cw, (c + 1) * cw)
        gate = _dot(xn, wg_ref[:, sl])
        up = _dot(xn, wu_ref[:, sl])
        h = (gate * (1.0 / (1.0 + jnp.exp(-gate))) * up).astype(BF16)
        part = _dot(h, wd_ref[sl, :])
        z = part if z is None else z + part
    o_ref[...] = x + _rms(z, gpost_ref[...])


def _swiglu_block(x, gpre, wg, wu, wd, gpost, tm):
    T, D = x.shape
    ff = wg.shape[1]
    n_split = ff // MXU_WIDTH if ff % MXU_WIDTH == 0 else 1
    row = pl.BlockSpec((tm, D), lambda i: (i, 0))
    vmem = 4 * tm * D * 4 + 3 * D * ff * 2 + 4 * tm * (ff // n_split) * 4 + 4 * tm * D * 4
    return pl.pallas_call(
        functools.partial(_swiglu_kernel, n_split=n_split),
        grid=(T // tm,),
        in_specs=[row, _const_spec((1, D)), _const_spec((D, ff)), _const_spec((D, ff)),
                  _const_spec((ff, D)), _const_spec((1, D))],
        out_specs=row,
        out_shape=jax.ShapeDtypeStruct((T, D), F32),
        compiler_params=_params(("parallel",), vmem),
        name="swiglu",
    )(x, gpre, wg, wu, wd, gpost)


def _trunk(x, mem, P):
    B, L, D = x.shape
    depth = P['w_in'].shape[0]
    n_mem = mem.shape[1]
    tm = min(ROW_TILE, L)
    tq, tk = tm, min(KEY_CHUNK, L)
    tc = LANES
    a = DA_HEADS * 2 * DA_HEAD_DIM
    tables = _rope_tables(L)
    xf = x.reshape(B * L, D)
    memf = mem.reshape(B * n_mem, D)
    row = lambda v: v.reshape(1, -1)
    for l in range(depth):
        lam_init = 0.8 - 0.6 * math.exp(-0.3 * l)
        qt, k, vt, hy = _inproj(xf, row(P['ln_mix_pre'][l]), P['w_in'][l].astype(BF16), tables, B, L, tm, tk)
        lam_params = jnp.stack([P['lambda_q1'][l], P['lambda_k1'][l], P['lambda_q2'][l], P['lambda_k2'][l],
                                jnp.full((DA_HEAD_DIM,), lam_init, F32)])
        attn = _diff_attention(lam_params, qt, k.reshape(B, L, a), vt, P['subln_g'][l], tq, tk)
        kf = _hyena_filter_spectrum(L, P['filt_w1'][l], row(P['filt_b1'][l]), row(P['filt_freq1'][l]),
                                    P['filt_w2'][l], row(P['filt_b2'][l]), row(P['filt_freq2'][l]),
                                    P['filt_w3'][l], tc)
        y = _hyena_operator(hy.reshape(B, L, -1), P['conv_w'][l], row(P['conv_b'][l]),
                            row(P['hyena_d'][l]), kf)
        kmt, vm = _memkv(memf, row(P['ln_mem'][l]), P['wk_x'][l].astype(BF16), P['wv_x'][l].astype(BF16), n_mem)
        xf = _mixout_cross_block(attn.reshape(B * L, a), y.reshape(B * L, -1), xf, P['w_out'][l].astype(BF16),
                                 row(P['ln_mix_post'][l]), kmt, vm.reshape(B, n_mem, D),
                                 row(P['ln_x_pre'][l]), P['wq_x'][l].astype(BF16), P['wo_x'][l].astype(BF16),
                                 row(P['ln_x_post'][l]), L, min(MIXOUT_ROW_TILE, L))
        xf = _swiglu_block(xf, row(P['ln_ffn_pre'][l]), P['w_gate'][l].astype(BF16), P['w_up'][l].astype(BF16),
                           P['w_down'][l].astype(BF16), row(P['ln_ffn_post'][l]), tm)
    return xf.reshape(B, L, D)


def kernel(x_prompt, x_sample, mem_prompt, mem_sample, ln_mix_pre, ln_mix_post, w_in, lambda_q1, lambda_k1, lambda_q2, lambda_k2, subln_g, conv_w, conv_b, filt_w1, filt_b1, filt_freq1, filt_w2, filt_b2, filt_freq2, filt_w3, hyena_d, w_out, ln_x_pre, ln_x_post, ln_mem, wq_x, wk_x, wv_x, wo_x, ln_ffn_pre, ln_ffn_post, w_gate, w_up, w_down):
    P = dict(ln_mix_pre=ln_mix_pre, ln_mix_post=ln_mix_post, w_in=w_in,
             lambda_q1=lambda_q1, lambda_k1=lambda_k1, lambda_q2=lambda_q2, lambda_k2=lambda_k2,
             subln_g=subln_g, conv_w=conv_w, conv_b=conv_b,
             filt_w1=filt_w1, filt_b1=filt_b1, filt_freq1=filt_freq1,
             filt_w2=filt_w2, filt_b2=filt_b2, filt_freq2=filt_freq2, filt_w3=filt_w3,
             hyena_d=hyena_d, w_out=w_out,
             ln_x_pre=ln_x_pre, ln_x_post=ln_x_post, ln_mem=ln_mem,
             wq_x=wq_x, wk_x=wk_x, wv_x=wv_x, wo_x=wo_x,
             ln_ffn_pre=ln_ffn_pre, ln_ffn_post=ln_ffn_post,
             w_gate=w_gate, w_up=w_up, w_down=w_down)
    return (_trunk(x_prompt, mem_prompt, P), _trunk(x_sample, mem_sample, P))
```
